```python
import math
import jax, jax.numpy as jnp
from jax import lax
import numpy as np

D_MODEL = 2048
BATCH = 2
SEQ = 16384
DEPTH = 1

CHUNK = 64
Q_BLOCK = 128
TOKEN_BLOCK = 128

SSM_WIDTH = 1024
SSM_GROUP = 16
SSM_GROUPS = SSM_WIDTH // SSM_GROUP
SSM_STATE = 64
DT_MIN = 1e-3
DT_MAX = 1e-1

N_HEADS = 16
Q_LORA = 512
KV_LORA = 512
QK_NOPE = 128
QK_ROPE = 64
V_HEAD = 128
QK_HEAD = QK_NOPE + QK_ROPE
ROPE_THETA = 10000.0

PEER_HEADS = 8
N_KEYS = 128
N_EXPERTS = N_KEYS * N_KEYS
PEER_KEY_HALF = 128
PEER_TOPK = 16

LN_EPS = 1e-5
RMS_EPS = 1e-6
DEEPNORM_ALPHA = (2.0 * DEPTH) ** 0.25
DEEPNORM_BETA = (8.0 * DEPTH) ** -0.25

OFF_SSM = 0
OFF_Q = OFF_SSM + SSM_WIDTH
OFF_KV = OFF_Q + Q_LORA
OFF_KR = OFF_KV + KV_LORA
OFF_GA = OFF_KR + QK_ROPE
OFF_GS = OFF_GA + D_MODEL
IN_WIDTH = OFF_GS + D_MODEL

kernel_name = "hybrid_s5_mla_peer_deepnorm"


def layer_norm(x, g, b):
    xf = x.astype(jnp.float32)
    mu = jnp.mean(xf, axis=-1, keepdims=True)
    var = jnp.mean(jnp.square(xf - mu), axis=-1, keepdims=True)
    y = (xf - mu) * lax.rsqrt(var + LN_EPS) * g.astype(jnp.float32) + b.astype(jnp.float32)
    return y.astype(x.dtype)


def rms_norm(x, g):
    xf = x.astype(jnp.float32)
    y = xf * lax.rsqrt(jnp.mean(jnp.square(xf), axis=-1, keepdims=True) + RMS_EPS) * g.astype(jnp.float32)
    return y.astype(x.dtype)


def apply_rope(x):
    S = x.shape[1]
    half = QK_ROPE // 2
    inv_freq = 1.0 / (ROPE_THETA ** (jnp.arange(half, dtype=jnp.float32) / half))
    ang = jnp.arange(S, dtype=jnp.float32)[:, None] * inv_freq[None, :]
    cos = jnp.cos(ang)[None, :, None, :]
    sin = jnp.sin(ang)[None, :, None, :]
    xf = x.astype(jnp.float32)
    x1, x2 = xf[..., :half], xf[..., half:]
    out = jnp.concatenate([x1 * cos - x2 * sin, x1 * sin + x2 * cos], axis=-1)
    return out.astype(x.dtype)


def chunk_causal_attention(q, k, v):
    Bsz, S, H, Dq = q.shape
    nb = S // Q_BLOCK
    scale = 1.0 / math.sqrt(QK_HEAD)
    q_blocks = q.reshape(Bsz, nb, Q_BLOCK, H, Dq).transpose(1, 0, 2, 3, 4)
    k_chunk = jnp.arange(S) // CHUNK
    neg = jnp.finfo(jnp.float32).min

    def one_block(args):
        qi, bi = args
        s = jnp.einsum('bqhd,bkhd->bhqk', qi, k).astype(jnp.float32) * scale
        q_chunk = (bi * Q_BLOCK + jnp.arange(Q_BLOCK)) // CHUNK
        mask = k_chunk[None, :] <= q_chunk[:, None]
        p = jax.nn.softmax(jnp.where(mask[None, None], s, neg), axis=-1)
        return jnp.einsum('bhqk,bkhd->bqhd', p.astype(v.dtype), v)

    out = lax.map(one_block, (q_blocks, jnp.arange(nb)))
    return out.transpose(1, 0, 2, 3, 4).reshape(Bsz, S, H, v.shape[-1])


def s5_ssm(u, lam_re, lam_im, log_dt, b_re, b_im, c_re, c_im, d_skip):
    Bsz, S, _ = u.shape
    G, P, Hg = SSM_GROUPS, SSM_STATE, SSM_GROUP
    uf = u.astype(jnp.float32).reshape(Bsz, S, G, Hg)
    dt = jnp.exp(log_dt.astype(jnp.float32))[:, None]
    lr = lam_re.astype(jnp.float32)
    li = lam_im.astype(jnp.float32)
    mag = jnp.exp(lr * dt)
    ab_re = mag * jnp.cos(li * dt)
    ab_im = mag * jnp.sin(li * dt)
    den = lr * lr + li * li
    nr = ab_re - 1.0
    ni = ab_im
    coef_re = (nr * lr + ni * li) / den
    coef_im = (ni * lr - nr * li) / den
    br = b_re.astype(jnp.float32)
    bi = b_im.astype(jnp.float32)
    bb_re = coef_re[..., None] * br - coef_im[..., None] * bi
    bb_im = coef_re[..., None] * bi + coef_im[..., None] * br
    bu_re = jnp.einsum('bsgh,gph->bsgp', uf, bb_re)
    bu_im = jnp.einsum('bsgh,gph->bsgp', uf, bb_im)
    a_re = jnp.broadcast_to(ab_re[None, None], (1, S, G, P))
    a_im = jnp.broadcast_to(ab_im[None, None], (1, S, G, P))

    def combine(e1, e2):
        a1r, a1i, b1r, b1i = e1
        a2r, a2i, b2r, b2i = e2
        return (a2r * a1r - a2i * a1i,
                a2r * a1i + a2i * a1r,
                a2r * b1r - a2i * b1i + b2r,
                a2r * b1i + a2i * b1r + b2i)

    _, _, x_re, x_im = lax.associative_scan(combine, (a_re, a_im, bu_re, bu_im), axis=1)
    y = (jnp.einsum('bsgp,ghp->bsgh', x_re, c_re.astype(jnp.float32))
         - jnp.einsum('bsgp,ghp->bsgh', x_im, c_im.astype(jnp.float32)))
    y = y + d_skip.astype(jnp.float32).reshape(G, Hg) * uf
    return y.reshape(Bsz, S, SSM_WIDTH).astype(u.dtype)


def hybrid_mixer(h, w_in, q_norm_g, w_uq, kv_norm_g, w_ukv, lam_re, lam_im, log_dt,
                 b_re, b_im, c_re, c_im, d_skip, w_glu, w_proj_attn, w_proj_ssm, w_out):
    Bsz, S, _ = h.shape
    z = h @ w_in
    u_ssm = z[..., OFF_SSM:OFF_Q]
    c_q = z[..., OFF_Q:OFF_KV]
    c_kv = z[..., OFF_KV:OFF_KR]
    k_r = z[..., OFF_KR:OFF_GA]
    g_attn = z[..., OFF_GA:OFF_GS]
    g_ssm = z[..., OFF_GS:IN_WIDTH]

    q = (rms_norm(c_q, q_norm_g) @ w_uq).reshape(Bsz, S, N_HEADS, QK_HEAD)
    q = jnp.concatenate([q[..., :QK_NOPE], apply_rope(q[..., QK_NOPE:])], axis=-1)
    kv = (rms_norm(c_kv, kv_norm_g) @ w_ukv).reshape(Bsz, S, N_HEADS, QK_NOPE + V_HEAD)
    k_rope = apply_rope(k_r[:, :, None, :])
    k = jnp.concatenate([kv[..., :QK_NOPE],
                         jnp.broadcast_to(k_rope, (Bsz, S, N_HEADS, QK_ROPE))], axis=-1)
    v = kv[..., QK_NOPE:]
    attn = chunk_causal_attention(q, k, v).reshape(Bsz, S, N_HEADS * V_HEAD)

    y = jax.nn.gelu(s5_ssm(u_ssm, lam_re, lam_im, log_dt, b_re, b_im, c_re, c_im, d_skip))
    ssm = y * jax.nn.sigmoid(y @ w_glu)

    merged = jax.nn.sigmoid(g_attn) * (attn @ w_proj_attn) + jax.nn.sigmoid(g_ssm) * (ssm @ w_proj_ssm)
    return merged @ w_out


def peer_ffn(h, w_pq, sub_keys, u_tab, v_tab):
    Bsz, S, D = h.shape
    K = PEER_TOPK
    q = (h @ w_pq).reshape(Bsz, S, PEER_HEADS, 2, PEER_KEY_HALF)
    s = jnp.einsum('bshcd,hcnd->bshcn', q, sub_keys).astype(jnp.float32)
    v1, i1 = lax.top_k(s[..., 0, :], K)
    v2, i2 = lax.top_k(s[..., 1, :], K)
    cand = (v1[..., :, None] + v2[..., None, :]).reshape(Bsz, S, PEER_HEADS, K * K)
    top, pos = lax.top_k(cand, K)
    expert = (jnp.take_along_axis(i1, pos // K, axis=-1) * N_KEYS
              + jnp.take_along_axis(i2, pos % K, axis=-1))
    gate = jax.nn.softmax(top, axis=-1)

    n_blk = (Bsz * S) // TOKEN_BLOCK
    hb = h.reshape(n_blk, TOKEN_BLOCK, D)
    eb = expert.reshape(n_blk, TOKEN_BLOCK, PEER_HEADS * K)
    gb = gate.reshape(n_blk, TOKEN_BLOCK, PEER_HEADS * K).astype(h.dtype)

    def one_block(args):
        xt, et, gt = args
        u = u_tab[et]
        act = jax.nn.gelu(jnp.einsum('td,tkd->tk', xt, u))
        return jnp.einsum('tk,tkd->td', gt * act, v_tab[et])

    out = lax.map(one_block, (hb, eb, gb))
    return out.reshape(Bsz, S, D)


def setup_inputs(seed: int = 0) -> dict:
    key = jax.random.key(seed)
    ks = jax.random.split(key, 40)
    f32 = jnp.float32
    L = DEPTH

    def nrm(k, shape, scale):
        return jax.random.normal(k, shape, f32) * scale

    n_idx = jnp.arange(SSM_STATE, dtype=f32)
    return {
        "x": nrm(ks[0], (BATCH, SEQ, D_MODEL), 1.0),
        "ln_in_g": 1.0 + nrm(ks[1], (D_MODEL,), 0.02),
        "ln_in_b": nrm(ks[2], (D_MODEL,), 0.02),
        "w_in": nrm(ks[3], (L, D_MODEL, IN_WIDTH), D_MODEL ** -0.5),
        "q_norm_g": 1.0 + nrm(ks[4], (L, Q_LORA), 0.02),
        "w_uq": nrm(ks[5], (L, Q_LORA, N_HEADS * QK_HEAD), Q_LORA ** -0.5),
        "kv_norm_g": 1.0 + nrm(ks[6], (L, KV_LORA), 0.02),
        "w_ukv": nrm(ks[7], (L, KV_LORA, N_HEADS * (QK_NOPE + V_HEAD)), KV_LORA ** -0.5),
        "ssm_lam_re": -0.5 + nrm(ks[8], (L, SSM_GROUPS, SSM_STATE), 0.01),
        "ssm_lam_im": math.pi * n_idx + nrm(ks[9], (L, SSM_GROUPS, SSM_STATE), 0.01),
        "ssm_log_dt": jax.random.uniform(ks[10], (L, SSM_GROUPS), f32, math.log(DT_MIN), math.log(DT_MAX)),
        "ssm_b_re": nrm(ks[11], (L, SSM_GROUPS, SSM_STATE, SSM_GROUP), (2.0 * SSM_GROUP) ** -0.5),
        "ssm_b_im": nrm(ks[12], (L, SSM_GROUPS, SSM_STATE, SSM_GROUP), (2.0 * SSM_GROUP) ** -0.5),
        "ssm_c_re": nrm(ks[13], (L, SSM_GROUPS, SSM_GROUP, SSM_STATE), SSM_STATE ** -0.5),
        "ssm_c_im": nrm(ks[14], (L, SSM_GROUPS, SSM_GROUP, SSM_STATE), SSM_STATE ** -0.5),
        "ssm_d": nrm(ks[15], (L, SSM_WIDTH), 1.0),
        "w_glu": nrm(ks[16], (L, SSM_WIDTH, SSM_WIDTH), SSM_WIDTH ** -0.5),
        "w_proj_attn": nrm(ks[17], (L, N_HEADS * V_HEAD, D_MODEL), (N_HEADS * V_HEAD) ** -0.5),
        "w_proj_ssm": nrm(ks[18], (L, SSM_WIDTH, D_MODEL), SSM_WIDTH ** -0.5),
        "w_out": nrm(ks[19], (L, D_MODEL, D_MODEL), DEEPNORM_BETA * D_MODEL ** -0.5),
        "ln_mix_g": 1.0 + nrm(ks[20], (L, D_MODEL), 0.02),
        "ln_mix_b": nrm(ks[21], (L, D_MODEL), 0.02),
        "w_peer_q": nrm(ks[22], (L, D_MODEL, PEER_HEADS * 2 * PEER_KEY_HALF), D_MODEL ** -0.5),
        "peer_sub_keys": nrm(ks[23], (L, PEER_HEADS, 2, N_KEYS, PEER_KEY_HALF), PEER_KEY_HALF ** -0.5),
        "peer_u": nrm(ks[24], (L, N_EXPERTS, D_MODEL), D_MODEL ** -0.5),
        "peer_v": nrm(ks[25], (L, N_EXPERTS, D_MODEL), DEEPNORM_BETA),
        "ln_ffn_g": 1.0 + nrm(ks[26], (L, D_MODEL), 0.02),
        "ln_ffn_b": nrm(ks[27], (L, D_MODEL), 0.02),
    }


def reference(x, ln_in_g, ln_in_b, w_in, q_norm_g, w_uq, kv_norm_g, w_ukv,
              ssm_lam_re, ssm_lam_im, ssm_log_dt, ssm_b_re, ssm_b_im, ssm_c_re, ssm_c_im, ssm_d,
              w_glu, w_proj_attn, w_proj_ssm, w_out, ln_mix_g, ln_mix_b,
              w_peer_q, peer_sub_keys, peer_u, peer_v, ln_ffn_g, ln_ffn_b):
    h = layer_norm(x, ln_in_g, ln_in_b)
    for l in range(DEPTH):
        y = hybrid_mixer(h, w_in[l], q_norm_g[l], w_uq[l], kv_norm_g[l], w_ukv[l],
                         ssm_lam_re[l], ssm_lam_im[l], ssm_log_dt[l], ssm_b_re[l], ssm_b_im[l],
                         ssm_c_re[l], ssm_c_im[l], ssm_d[l], w_glu[l],
                         w_proj_attn[l], w_proj_ssm[l], w_out[l])
        h = layer_norm(DEEPNORM_ALPHA * h + y, ln_mix_g[l], ln_mix_b[l])
        y = peer_ffn(h, w_peer_q[l], peer_sub_keys[l], peer_u[l], peer_v[l])
        h = layer_norm(DEEPNORM_ALPHA * h + y, ln_ffn_g[l], ln_ffn_b[l])
    return h
```

```python
import functools
import math

import jax
import jax.numpy as jnp
from jax import lax
from jax.experimental import pallas as pl
from jax.experimental.pallas import tpu as pltpu

F32 = jnp.float32
BF16 = jnp.bfloat16

D_MODEL = 2048
DEPTH = 1
CHUNK = 64
SSM_WIDTH = 1024
SSM_GROUP = 16
SSM_GROUPS = SSM_WIDTH // SSM_GROUP
SSM_STATE = 64
N_HEADS = 16
Q_LORA = 512
KV_LORA = 512
QK_NOPE = 128
QK_ROPE = 64
V_HEAD = 128
QK_HEAD = QK_NOPE + QK_ROPE
ROPE_THETA = 10000.0
PEER_HEADS = 8
N_KEYS = 128
N_EXPERTS = N_KEYS * N_KEYS
PEER_KEY_HALF = 128
PEER_TOPK = 16
LN_EPS = 1e-5
RMS_EPS = 1e-6
DEEPNORM_ALPHA = (2.0 * DEPTH) ** 0.25
OFF_SSM = 0
OFF_Q = OFF_SSM + SSM_WIDTH
OFF_KV = OFF_Q + Q_LORA
OFF_KR = OFF_KV + KV_LORA
OFF_GA = OFF_KR + QK_ROPE
OFF_GS = OFF_GA + D_MODEL
IN_WIDTH = OFF_GS + D_MODEL

LANES = 128
VMEM_LIMIT_BYTES = 56 * 1024 * 1024

SSM_CHUNK = 16
SSM_TILE_GROUPS = LANES // SSM_GROUP
SSM_TILES = SSM_WIDTH // LANES
SSM_TILE_STATE = SSM_TILE_GROUPS * SSM_STATE
NEG_BIG = -1e30


def _params(semantics):
    return pltpu.CompilerParams(dimension_semantics=semantics, vmem_limit_bytes=VMEM_LIMIT_BYTES)


def _dot(a, b):
    return jnp.dot(a, b, preferred_element_type=F32)


def _ln(x, g, b):
    mu = jnp.mean(x, axis=-1, keepdims=True)
    xc = x - mu
    var = jnp.mean(xc * xc, axis=-1, keepdims=True)
    return xc * lax.rsqrt(var + LN_EPS) * g + b


def _gelu(x):
    return 0.5 * x * (1.0 + jnp.tanh(0.7978845608028654 * (x + 0.044715 * (x * x * x))))


def _sigmoid(x):
    return 1.0 / (1.0 + jnp.exp(-x))


def _rope_table_kernel(invf_ref, o_ref):
    ts = o_ref.shape[0]
    pos = (lax.broadcasted_iota(jnp.int32, (ts, LANES), 0) + pl.program_id(0) * ts).astype(F32)
    ang = pos * invf_ref[...]
    lane = lax.broadcasted_iota(jnp.int32, (ts, LANES), 1)
    o_ref[...] = jnp.where(lane < 2 * (QK_ROPE // 2), jnp.cos(ang), jnp.sin(ang))


def _rope_table(seq):
    half = QK_ROPE // 2
    invf = 1.0 / (ROPE_THETA ** (jnp.arange(half, dtype=F32) / half))
    invf4 = jnp.tile(invf, 4)[None, :]
    ts = min(seq, 1024)
    return pl.pallas_call(
        _rope_table_kernel,
        grid=(seq // ts,),
        in_specs=[pl.BlockSpec((1, LANES), lambda i: (0, 0))],
        out_specs=pl.BlockSpec((ts, LANES), lambda i: (i, 0)),
        out_shape=jax.ShapeDtypeStruct((seq, LANES), F32),
        compiler_params=_params(("parallel",)),
        name="rope_table",
    )(invf4)


def _inproj_kernel(x_ref, g_ref, b_ref, w_ref, wkr_ref, u_ref, z_ref, kr_ref, h_scr):
    j = pl.program_id(1)

    @pl.when(j == 0)
    def _():
        h = _ln(x_ref[...], g_ref[...], b_ref[...]).astype(BF16)
        h_scr[...] = h
        u_ref[...] = _dot(h, w_ref[...])
        kr_ref[...] = _dot(h, wkr_ref[...]).astype(kr_ref.dtype)

    @pl.when(j > 0)
    def _():
        z_ref[...] = _dot(h_scr[...], w_ref[...]).astype(z_ref.dtype)


def _inproj(x2, ln_g, ln_b, w_main, w_kr):
    n = x2.shape[0]
    tm = min(n, 1024)
    tn = 1024
    nj = w_main.shape[1] // tn
    return pl.pallas_call(
        _inproj_kernel,
        grid=(n // tm, nj),
        in_specs=[
            pl.BlockSpec((tm, D_MODEL), lambda i, j: (i, 0)),
            pl.BlockSpec((1, D_MODEL), lambda i, j: (0, 0)),
            pl.BlockSpec((1, D_MODEL), lambda i, j: (0, 0)),
            pl.BlockSpec((D_MODEL, tn), lambda i, j: (0, j)),
            pl.BlockSpec((D_MODEL, LANES), lambda i, j: (0, 0)),
        ],
        out_specs=[
            pl.BlockSpec((tm, tn), lambda i, j: (i, 0)),
            pl.BlockSpec((tm, tn), lambda i, j: (i, jnp.maximum(j - 1, 0))),
            pl.BlockSpec((tm, LANES), lambda i, j: (i, 0)),
        ],
        out_shape=[
            jax.ShapeDtypeStruct((n, tn), F32),
            jax.ShapeDtypeStruct((n, (nj - 1) * tn), BF16),
            jax.ShapeDtypeStruct((n, LANES), BF16),
        ],
        scratch_shapes=[pltpu.VMEM((tm, D_MODEL), BF16)],
        compiler_params=_params(("parallel", "arbitrary")),
        name="ln_inproj",
    )(x2, ln_g, ln_b, w_main, w_kr)


def _rope_mix(t2, cs):
    t = t2 * cs
    return t + pltpu.roll(t, QK_ROPE, 1)


def _qkv_kernel(cq_ref, ckv_ref, kr_ref, cs_ref, gq_ref, gkv_ref, wq_ref, wkv_ref,
                q_ref, k_ref, v_ref, cqn_scr, ckvn_scr, kro_scr):
    cs = cs_ref[...]

    @pl.when(pl.program_id(1) == 0)
    def _():
        def rms(c, g):
            c = c.astype(F32)
            return (c * lax.rsqrt(jnp.mean(c * c, axis=-1, keepdims=True) + RMS_EPS) * g).astype(BF16)

        cqn_scr[...] = rms(cq_ref[...], gq_ref[...])
        ckvn_scr[...] = rms(ckv_ref[...], gkv_ref[...])
        kro_scr[...] = _rope_mix(kr_ref[...].astype(F32), cs).astype(BF16)

    scale = 1.0 / math.sqrt(QK_HEAD)
    qe = _dot(cqn_scr[...], wq_ref[0])
    qr = _rope_mix(qe[:, QK_NOPE:], cs)
    q_ref[0, 0, :, :QK_NOPE] = (qe[:, :QK_NOPE] * scale).astype(BF16)
    q_ref[0, 0, :, QK_NOPE:] = (qr[:, :QK_ROPE] * scale).astype(BF16)
    kv = _dot(ckvn_scr[...], wkv_ref[0])
    k_ref[0, 0, :, :QK_NOPE] = kv[:, :QK_NOPE].astype(BF16)
    k_ref[0, 0, :, QK_NOPE:] = kro_scr[:, :QK_ROPE]
    v_ref[0, 0] = kv[:, QK_NOPE:].astype(BF16)


def _qkv(z, kr, cs, gq, gkv, wq_ext, wkv, batch, seq):
    n = z.shape[0]
    tm = min(seq, 2048)
    nsb = seq // tm
    return pl.pallas_call(
        _qkv_kernel,
        grid=(n // tm, N_HEADS),
        in_specs=[
            pl.BlockSpec((tm, Q_LORA), lambda i, h: (i, 0)),
            pl.BlockSpec((tm, KV_LORA), lambda i, h: (i, 1)),
            pl.BlockSpec((tm, LANES), lambda i, h: (i, 0)),
            pl.BlockSpec((tm, LANES), lambda i, h: (i % nsb, 0)),
            pl.BlockSpec((1, Q_LORA), lambda i, h: (0, 0)),
            pl.BlockSpec((1, KV_LORA), lambda i, h: (0, 0)),
            pl.BlockSpec((1, Q_LORA, 2 * LANES), lambda i, h: (h, 0, 0)),
            pl.BlockSpec((1, KV_LORA, 2 * LANES), lambda i, h: (h, 0, 0)),
        ],
        out_specs=[
            pl.BlockSpec((1, 1, tm, QK_HEAD), lambda i, h: (i // nsb, h, i % nsb, 0)),
            pl.BlockSpec((1, 1, tm, QK_HEAD), lambda i, h: (i // nsb, h, i % nsb, 0)),
            pl.BlockSpec((1, 1, tm, V_HEAD), lambda i, h: (i // nsb, h, i % nsb, 0)),
        ],
        out_shape=[
            jax.ShapeDtypeStruct((batch, N_HEADS, seq, QK_HEAD), BF16),
            jax.ShapeDtypeStruct((batch, N_HEADS, seq, QK_HEAD), BF16),
            jax.ShapeDtypeStruct((batch, N_HEADS, seq, V_HEAD), BF16),
        ],
        scratch_shapes=[
            pltpu.VMEM((tm, Q_LORA), BF16),
            pltpu.VMEM((tm, KV_LORA), BF16),
            pltpu.VMEM((tm, LANES), BF16),
        ],
        compiler_params=_params(("parallel", "arbitrary")),
        name="qkv_rope",
    )(z, z, kr, cs, gq, gkv, wq_ext, wkv)


def _attn_kernel(q_ref, k_ref, v_ref, o_ref, m_scr, l_scr, acc_scr, *, bq):
    qi = pl.program_id(2)
    q = q_ref[0, 0]
    m_scr[...] = jnp.full(m_scr.shape, -jnp.inf, F32)
    l_scr[...] = jnp.zeros(l_scr.shape, F32)
    acc_scr[...] = jnp.zeros(acc_scr.shape, F32)

    def step(j, masked):
        start = pl.multiple_of(j * bq, bq)
        kj = k_ref[0, 0, pl.ds(start, bq), :]
        vj = v_ref[0, 0, pl.ds(start, bq), :]
        s = lax.dot_general(q, kj, (((1,), (1,)), ((), ())), preferred_element_type=F32)
        if masked:
            rc = lax.broadcasted_iota(jnp.int32, (bq, bq), 0) // CHUNK
            cc = lax.broadcasted_iota(jnp.int32, (bq, bq), 1) // CHUNK
            s = jnp.where(cc <= rc, s, NEG_BIG)
        m_prev = m_scr[...]
        m_new = jnp.maximum(m_prev, jnp.max(s, axis=-1, keepdims=True))
        alpha = jnp.exp(m_prev - m_new)
        p = jnp.exp(s - m_new)
        l_scr[...] = alpha * l_scr[...] + jnp.sum(p, axis=-1, keepdims=True)
        acc_scr[...] = alpha * acc_scr[...] + _dot(p.astype(BF16), vj)
        m_scr[...] = m_new

    def body(j, carry):
        step(j, False)
        return carry

    lax.fori_loop(0, qi, body, 0)
    step(qi, True)
    o_ref[0] = (acc_scr[...] / l_scr[...]).astype(o_ref.dtype)


def _attention(q, k, v):
    batch, _, seq, _ = q.shape
    bq = min(seq, 512)
    return pl.pallas_call(
        functools.partial(_attn_kernel, bq=bq),
        grid=(batch, N_HEADS, seq // bq),
        in_specs=[
            pl.BlockSpec((1, 1, bq, QK_HEAD), lambda b, h, i: (b, h, i, 0)),
            pl.BlockSpec((1, 1, seq, QK_HEAD), lambda b, h, i: (b, h, 0, 0)),
            pl.BlockSpec((1, 1, seq, V_HEAD), lambda b, h, i: (b, h, 0, 0)),
        ],
        out_specs=pl.BlockSpec((1, bq, V_HEAD), lambda b, h, i: (b, i, h)),
        out_shape=jax.ShapeDtypeStruct((batch, seq, N_HEADS * V_HEAD), BF16),
        scratch_shapes=[
            pltpu.VMEM((bq, 1), F32),
            pltpu.VMEM((bq, 1), F32),
            pltpu.VMEM((bq, V_HEAD), F32),
        ],
        compiler_params=_params(("parallel", "parallel", "arbitrary")),
        name="flash_attn",
    )(q, k, v)


def _cpow(lr, li, ldt, k):
    dt = jnp.exp(ldt)
    mag = jnp.exp(lr * dt * k)
    ang = li * dt * k
    return mag * jnp.cos(ang), mag * jnp.sin(ang)


def _ssm_wgen_kernel(lr_r, li_r, ldt_r, lr_c, li_c, ldt_c, bre_ref, bim_ref, cre_ref, cim_ref,
                     w_ref, ws_ref, wx_ref, a16_ref):
    hp = lax.Precision.HIGHEST
    ns = SSM_TILE_STATE
    lr, li, ldt = lr_r[0], li_r[0], ldt_r[0]
    a_re, a_im = _cpow(lr, li, ldt, 1.0)
    den = lr * lr + li * li
    nr = a_re - 1.0
    ni = a_im
    cf_re = (nr * lr + ni * li) / den
    cf_im = (ni * lr - nr * li) / den
    bre, bim = bre_ref[0], bim_ref[0]
    cre, cim = cre_ref[0], cim_ref[0]
    w_ref[...] = jnp.zeros(w_ref.shape, w_ref.dtype)
    for k in range(SSM_CHUNK):
        p_re, p_im = _cpow(lr, li, ldt, float(k))
        m_re = cf_re * p_re - cf_im * p_im
        m_im = cf_re * p_im + cf_im * p_re
        bm_re = bre * m_re - bim * m_im
        bm_im = bre * m_im + bim * m_re
        bd = (jnp.dot(bm_re, cre, precision=hp, preferred_element_type=F32)
              - jnp.dot(bm_im, cim, precision=hp, preferred_element_type=F32)).astype(BF16)
        for j in range(SSM_CHUNK - k):
            i = j + k
            w_ref[0, j * LANES:(j + 1) * LANES, i * LANES:(i + 1) * LANES] = bd
        jj = SSM_CHUNK - 1 - k
        ws_ref[0, jj * LANES:(jj + 1) * LANES, :ns] = bm_re.astype(BF16)
        ws_ref[0, jj * LANES:(jj + 1) * LANES, ns:] = bm_im.astype(BF16)
    lrc, lic, ldtc = lr_c[0], li_c[0], ldt_c[0]
    for i in range(SSM_CHUNK):
        p_re, p_im = _cpow(lrc, lic, ldtc, float(i + 1))
        wx_ref[0, :ns, i * LANES:(i + 1) * LANES] = (p_re * cre - p_im * cim).astype(BF16)
        wx_ref[0, ns:, i * LANES:(i + 1) * LANES] = (-(p_re * cim + p_im * cre)).astype(BF16)
    q_re, q_im = _cpow(lr, li, ldt, float(SSM_CHUNK))
    a16_ref[0, :, :ns] = q_re
    a16_ref[0, :, ns:] = q_im


def _ssm_wgen(lam_re, lam_im, log_dt, b_re, b_im, c_re, c_im):
    nt, tg, ns = SSM_TILES, SSM_TILE_GROUPS, SSM_TILE_STATE
    eye = jnp.eye(tg, dtype=F32)
    bbd = lambda b: jnp.einsum("cgph,gk->cghkp", b.reshape(nt, tg, SSM_STATE, SSM_GROUP), eye).reshape(nt, LANES, ns)
    cbd = lambda c: jnp.einsum("cghp,gk->cgpkh", c.reshape(nt, tg, SSM_GROUP, SSM_STATE), eye).reshape(nt, ns, LANES)
    ldt = jnp.repeat(log_dt, SSM_STATE)
    rows = [a.reshape(nt, 1, ns) for a in (lam_re, lam_im, ldt)]
    cols = [a.reshape(nt, ns, 1) for a in (lam_re, lam_im, ldt)]
    kc = SSM_CHUNK * LANES
    row_spec = pl.BlockSpec((1, 1, ns), lambda c: (c, 0, 0))
    col_spec = pl.BlockSpec((1, ns, 1), lambda c: (c, 0, 0))
    return pl.pallas_call(
        _ssm_wgen_kernel,
        grid=(nt,),
        in_specs=[row_spec] * 3 + [col_spec] * 3 + [
            pl.BlockSpec((1, LANES, ns), lambda c: (c, 0, 0)),
            pl.BlockSpec((1, LANES, ns), lambda c: (c, 0, 0)),
            pl.BlockSpec((1, ns, LANES), lambda c: (c, 0, 0)),
            pl.BlockSpec((1, ns, LANES), lambda c: (c, 0, 0)),
        ],
        out_specs=[
            pl.BlockSpec((1, kc, kc), lambda c: (c, 0, 0)),
            pl.BlockSpec((1, kc, 2 * ns), lambda c: (c, 0, 0)),
            pl.BlockSpec((1, 2 * ns, kc), lambda c: (c, 0, 0)),
            pl.BlockSpec((1, 1, 2 * ns), lambda c: (c, 0, 0)),
        ],
        out_shape=[
            jax.ShapeDtypeStruct((nt, kc, kc), BF16),
            jax.ShapeDtypeStruct((nt, kc, 2 * ns), BF16),
            jax.ShapeDtypeStruct((nt, 2 * ns, kc), BF16),
            jax.ShapeDtypeStruct((nt, 1, 2 * ns), F32),
        ],
        compiler_params=_params(("parallel",)),
        name="ssm_wgen",
    )(*rows, *cols, bbd(b_re), bbd(b_im), cbd(c_re), cbd(c_im))


def _chunk_rows(u_ref, tb):
    return [u_ref[pl.ds(j, tb, stride=SSM_CHUNK), :] for j in range(SSM_CHUNK)]


def _ssm_sum_kernel(u_ref, ws_ref, s_ref, *, tb):
    u16 = jnp.concatenate(_chunk_rows(u_ref, tb), axis=1).astype(BF16)
    s_ref[0] = _dot(u16, ws_ref[0])


def _ssm_scan_kernel(s_ref, a16_ref, x_ref):
    ns = SSM_TILE_STATE
    a_re = a16_ref[0, :, :ns]
    a_im = a16_ref[0, :, ns:]
    nrow = s_ref.shape[1]

    def body(r, carry):
        xr, xi = carry
        x_ref[0, pl.ds(r, 1), :ns] = xr
        x_ref[0, pl.ds(r, 1), ns:] = xi
        sr = s_ref[0, pl.ds(r, 1), :ns]
        si = s_ref[0, pl.ds(r, 1), ns:]
        return a_re * xr - a_im * xi + sr, a_re * xi + a_im * xr + si

    zero = jnp.zeros((1, ns), F32)
    lax.fori_loop(0, nrow, body, (zero, zero), unroll=8)


def _ssm_out_kernel(u_ref, x_ref, w_ref, wx_ref, d_ref, g_ref, *, tb):
    us = _chunk_rows(u_ref, tb)
    u16 = jnp.concatenate(us, axis=1).astype(BF16)
    y16 = _dot(u16, w_ref[0]) + _dot(x_ref[0].astype(BF16), wx_ref[0])
    d = d_ref[...]
    for i in range(SSM_CHUNK):
        y = y16[:, i * LANES:(i + 1) * LANES] + d * us[i]
        g_ref[pl.ds(i, tb, stride=SSM_CHUNK), :] = _gelu(y)


def _ssm_branch(u, batch, seq, w, ws, wx, a16, d_skip):
    n = u.shape[0]
    nt, ns = SSM_TILES, SSM_TILE_STATE
    kc = SSM_CHUNK * LANES
    nb = n // SSM_CHUNK
    nbs = seq // SSM_CHUNK
    tb = min(nbs, 256)
    tok = tb * SSM_CHUNK
    s = pl.pallas_call(
        functools.partial(_ssm_sum_kernel, tb=tb),
        grid=(nt, nb // tb),
        in_specs=[
            pl.BlockSpec((tok, LANES), lambda c, r: (r, c)),
            pl.BlockSpec((1, kc, 2 * ns), lambda c, r: (c, 0, 0)),
        ],
        out_specs=pl.BlockSpec((1, tb, 2 * ns), lambda c, r: (c, r, 0)),
        out_shape=jax.ShapeDtypeStruct((nt, nb, 2 * ns), F32),
        compiler_params=_params(("parallel", "parallel")),
        name="ssm_chunk_sum",
    )(u, ws)
    x = pl.pallas_call(
        _ssm_scan_kernel,
        grid=(nt, batch),
        in_specs=[
            pl.BlockSpec((1, nbs, 2 * ns), lambda c, b: (c, b, 0)),
            pl.BlockSpec((1, 1, 2 * ns), lambda c, b: (c, 0, 0)),
        ],
        out_specs=pl.BlockSpec((1, nbs, 2 * ns), lambda c, b: (c, b, 0)),
        out_shape=jax.ShapeDtypeStruct((nt, nb, 2 * ns), F32),
        compiler_params=_params(("parallel", "parallel")),
        name="ssm_state_scan",
    )(s, a16)
    return pl.pallas_call(
        functools.partial(_ssm_out_kernel, tb=tb),
        grid=(nt, nb // tb),
        in_specs=[
            pl.BlockSpec((tok, LANES), lambda c, r: (r, c)),
            pl.BlockSpec((1, tb, 2 * ns), lambda c, r: (c, r, 0)),
            pl.BlockSpec((1, kc, kc), lambda c, r: (c, 0, 0)),
            pl.BlockSpec((1, 2 * ns, kc), lambda c, r: (c, 0, 0)),
            pl.BlockSpec((1, LANES), lambda c, r: (0, c)),
        ],
        out_specs=pl.BlockSpec((tok, LANES), lambda c, r: (r, c)),
        out_shape=jax.ShapeDtypeStruct((n, SSM_WIDTH), F32),
        compiler_params=_params(("parallel", "parallel")),
        name="ssm_chunk_out",
    )(u, x, w, wx, d_skip)


def _merge_kernel(attn_ref, g_ref, ga_ref, gs_ref, wglu_ref, wpa_ref, wps_ref, o_ref, ssm_scr):
    @pl.when(pl.program_id(1) == 0)
    def _():
        g = g_ref[...]
        ssm_scr[...] = (g * _sigmoid(_dot(g.astype(BF16), wglu_ref[...]))).astype(BF16)

    pa = _dot(attn_ref[...], wpa_ref[...])
    ps = _dot(ssm_scr[...], wps_ref[...])
    o_ref[...] = (_sigmoid(ga_ref[...].astype(F32)) * pa + _sigmoid(gs_ref[...].astype(F32)) * ps).astype(o_ref.dtype)


def _merge(attn, g, z, w_glu, w_pa, w_ps):
    n = attn.shape[0]
    tm = min(n, 512)
    tn = 1024
    nj = D_MODEL // tn
    return pl.pallas_call(
        _merge_kernel,
        grid=(n // tm, nj),
        in_specs=[
            pl.BlockSpec((tm, N_HEADS * V_HEAD), lambda i, j: (i, 0)),
            pl.BlockSpec((tm, SSM_WIDTH), lambda i, j: (i, 0)),
            pl.BlockSpec((tm, tn), lambda i, j: (i, 1 + j)),
            pl.BlockSpec((tm, tn), lambda i, j: (i, 1 + nj + j)),
            pl.BlockSpec((SSM_WIDTH, SSM_WIDTH), lambda i, j: (0, 0)),
            pl.BlockSpec((N_HEADS * V_HEAD, tn), lambda i, j: (0, j)),
            pl.BlockSpec((SSM_WIDTH, tn), lambda i, j: (0, j)),
        ],
        out_specs=pl.BlockSpec((tm, tn), lambda i, j: (i, j)),
        out_shape=jax.ShapeDtypeStruct((n, D_MODEL), BF16),
        scratch_shapes=[pltpu.VMEM((tm, SSM_WIDTH), BF16)],
        compiler_params=_params(("parallel", "arbitrary")),
        name="glu_merge",
    )(attn, g, z, z, w_glu, w_pa, w_ps)


def _outproj_kernel(m_ref, x_ref, g1_ref, b1_ref, g2_ref, b2_ref, w_ref, h2_ref, h2t_ref):
    y = _dot(m_ref[...], w_ref[...])
    h = _ln(x_ref[...], g1_ref[...], b1_ref[...])
    h2 = _ln(DEEPNORM_ALPHA * h + y, g2_ref[...], b2_ref[...])
    h2_ref[...] = h2.astype(h2_ref.dtype)
    h2t_ref[...] = h2.T.astype(h2t_ref.dtype)


def _outproj(merged, x2, g1, b1, g2, b2, w_out):
    n = merged.shape[0]
    tm = min(n, 512)
    vec = pl.BlockSpec((1, D_MODEL), lambda i: (0, 0))
    return pl.pallas_call(
        _outproj_kernel,
        grid=(n // tm,),
        in_specs=[
            pl.BlockSpec((tm, D_MODEL), lambda i: (i, 0)),
            pl.BlockSpec((tm, D_MODEL), lambda i: (i, 0)),
            vec, vec, vec, vec,
            pl.BlockSpec((D_MODEL, D_MODEL), lambda i: (0, 0)),
        ],
        out_specs=[
            pl.BlockSpec((tm, D_MODEL), lambda i: (i, 0)),
            pl.BlockSpec((D_MODEL, tm), lambda i: (0, i)),
        ],
        out_shape=[
            jax.ShapeDtypeStruct((n, D_MODEL), BF16),
            jax.ShapeDtypeStruct((D_MODEL, n), BF16),
        ],
        compiler_params=_params(("parallel",)),
        name="outproj_ln",
    )(merged, x2, g1, b1, g2, b2, w_out)


def _peer_wcomb_kernel(keys_ref, wpq_ref, o_ref):
    o_ref[...] = lax.dot_general(keys_ref[0].astype(BF16), wpq_ref[...], (((1,), (1,)), ((), ())),
                                 preferred_element_type=F32).astype(o_ref.dtype)


def _peer_wcomb(sub_keys, w_pq):
    nhc = PEER_HEADS * 2
    keys = sub_keys.reshape(nhc, N_KEYS, PEER_KEY_HALF)
    return pl.pallas_call(
        _peer_wcomb_kernel,
        grid=(nhc,),
        in_specs=[
            pl.BlockSpec((1, N_KEYS, PEER_KEY_HALF), lambda i: (i, 0, 0)),
            pl.BlockSpec((D_MODEL, PEER_KEY_HALF), lambda i: (0, i)),
        ],
        out_specs=pl.BlockSpec((N_KEYS, D_MODEL), lambda i: (i, 0)),
        out_shape=jax.ShapeDtypeStruct((nhc * N_KEYS, D_MODEL), BF16),
        compiler_params=_params(("parallel",)),
        name="peer_wcomb",
    )(keys, w_pq)


def _peer_scores_kernel(w_ref, xt_ref, o_ref):
    o_ref[...] = _dot(w_ref[...], xt_ref[...])


def _peer_scores(wcomb, h2t):
    n = h2t.shape[1]
    tl = min(n, 512)
    rows = wcomb.shape[0]
    return pl.pallas_call(
        _peer_scores_kernel,
        grid=(n // tl,),
        in_specs=[
            pl.BlockSpec((rows, D_MODEL), lambda i: (0, 0)),
            pl.BlockSpec((D_MODEL, tl), lambda i: (0, i)),
        ],
        out_specs=pl.BlockSpec((rows, tl), lambda i: (0, i)),
        out_shape=jax.ShapeDtypeStruct((rows, n), F32),
        compiler_params=_params(("parallel",)),
        name="peer_scores",
    )(wcomb, h2t)


PEER_LIST = PEER_TOPK + 1


def _insert_sorted(t, v):
    out = []
    for r in range(len(t)):
        out.append(jnp.maximum(t[r], v))
        v = jnp.minimum(t[r], v)
    return out


def _peer_thresh_kernel(s_ref, e1_ref, th_ref, e2_ref):
    shape = s_ref.shape[1:]
    neg = jnp.full(shape, -jnp.inf, F32)

    def top_list(base):
        def body(n, t):
            return tuple(_insert_sorted(list(t), s_ref[base + n]))
        return lax.fori_loop(0, N_KEYS, body, (neg,) * PEER_LIST, unroll=4)

    def head(h, carry):
        base1 = h * (2 * N_KEYS)
        base2 = base1 + N_KEYS
        a = top_list(base1)
        b = top_list(base2)
        t = [neg] * PEER_LIST
        for i in range(1, PEER_LIST + 1):
            for j in range(1, PEER_LIST // i + 1):
                t = _insert_sorted(t, a[i - 1] + b[j - 1])
        tau = 0.5 * (t[PEER_TOPK - 1] + t[PEER_TOPK])
        top = a[0] + b[0]
        z = jnp.exp(t[0] - top)
        for r in range(1, PEER_TOPK):
            z = z + jnp.exp(t[r] - top)
        rz = 1.0 / z
        a0, b0 = a[0], b[0]

        def write(n, c):
            s1 = s_ref[base1 + n]
            s2 = s_ref[base2 + n]
            e1_ref[h * N_KEYS + n] = jnp.exp(s1 - a0)
            th_ref[h * N_KEYS + n] = tau - s1
            e2_ref[h * N_KEYS + n] = jnp.exp(s2 - b0) * rz
            return c

        lax.fori_loop(0, N_KEYS, write, 0, unroll=4)
        return carry

    lax.fori_loop(0, PEER_HEADS, head, 0)


def _peer_thresh(st):
    rows, n = st.shape
    sub = 8
    st3 = st.reshape(rows, n // LANES, LANES)
    nb = min(sub, n // LANES)
    half = rows // 2
    outs = pl.pallas_call(
        _peer_thresh_kernel,
        grid=(n // (nb * LANES),),
        in_specs=[pl.BlockSpec((rows, nb, LANES), lambda i: (0, i, 0))],
        out_specs=[pl.BlockSpec((half, nb, LANES), lambda i: (0, i, 0))] * 3,
        out_shape=[jax.ShapeDtypeStruct((half, n // LANES, LANES), F32)] * 3,
        compiler_params=_params(("parallel",)),
        name="peer_thresh",
    )(st3)
    return [o.reshape(PEER_HEADS, N_KEYS, n) for o in outs]


def _peer_dense_kernel(xt_ref, u_ref, vt_ref, s2_ref, e2_ref, e1_ref, th_ref, h2_ref, g_ref, b_ref,
                       o_ref, acc_scr, a_scr, *, sub):
    e = pl.program_id(1)
    eb = u_ref.shape[0]

    @pl.when(e == 0)
    def _():
        acc_scr[...] = jnp.zeros(acc_scr.shape, F32)

    xt = xt_ref[...]
    for s in range(eb // sub):
        st = _dot(u_ref[s * sub:(s + 1) * sub, :], xt)
        for ii in range(sub // N_KEYS):
            il = s * (sub // N_KEYS) + ii
            gate = None
            for h in range(PEER_HEADS):
                term = e1_ref[h, il:il + 1, :] * jnp.where(s2_ref[h] >= th_ref[h, il:il + 1, :], e2_ref[h], 0.0)
                gate = term if gate is None else gate + term
            a = _gelu(st[ii * N_KEYS:(ii + 1) * N_KEYS, :]) * gate
            a_scr[s * sub + ii * N_KEYS:s * sub + (ii + 1) * N_KEYS, :] = a.astype(BF16)
        acc_scr[...] += _dot(vt_ref[:, s * sub:(s + 1) * sub], a_scr[s * sub:(s + 1) * sub, :])

    @pl.when(e == pl.num_programs(1) - 1)
    def _():
        y = acc_scr[...].T
        o_ref[...] = _ln(DEEPNORM_ALPHA * h2_ref[...].astype(F32) + y, g_ref[...], b_ref[...])


def _peer_dense(h2, h2t, st, e1, th, e2, u_tab, vt_tab, ln_g, ln_b):
    n = h2.shape[0]
    t = min(n, 512)
    eb = 8 * N_KEYS
    sub = 4 * N_KEYS
    st4 = st.reshape(PEER_HEADS, 2, N_KEYS, n)
    vec = pl.BlockSpec((1, D_MODEL), lambda i, e: (0, 0))
    return pl.pallas_call(
        functools.partial(_peer_dense_kernel, sub=sub),
        grid=(n // t, N_EXPERTS // eb),
        in_specs=[
            pl.BlockSpec((D_MODEL, t), lambda i, e: (0, i)),
            pl.BlockSpec((eb, D_MODEL), lambda i, e: (e, 0)),
            pl.BlockSpec((D_MODEL, eb), lambda i, e: (0, e)),
            pl.BlockSpec((PEER_HEADS, None, N_KEYS, t), lambda i, e: (0, 1, 0, i)),
            pl.BlockSpec((PEER_HEADS, N_KEYS, t), lambda i, e: (0, 0, i)),
            pl.BlockSpec((PEER_HEADS, eb // N_KEYS, t), lambda i, e: (0, e, i)),
            pl.BlockSpec((PEER_HEADS, eb // N_KEYS, t), lambda i, e: (0, e, i)),
            pl.BlockSpec((t, D_MODEL), lambda i, e: (i, 0)),
            vec, vec,
        ],
        out_specs=pl.BlockSpec((t, D_MODEL), lambda i, e: (i, 0)),
        out_shape=jax.ShapeDtypeStruct((n, D_MODEL), F32),
        scratch_shapes=[
            pltpu.VMEM((D_MODEL, t), F32),
            pltpu.VMEM((eb, t), BF16),
        ],
        compiler_params=_params(("parallel", "arbitrary")),
        name="peer_dense",
    )(h2t, u_tab, vt_tab, st4, e2, e1, th, h2, ln_g, ln_b)


def _rot_half_cols(w):
    half = QK_ROPE // 2
    return jnp.concatenate([-w[..., half:], w[..., :half]], axis=-1)


def kernel(x, ln_in_g, ln_in_b, w_in, q_norm_g, w_uq, kv_norm_g, w_ukv, ssm_lam_re, ssm_lam_im, ssm_log_dt,
           ssm_b_re, ssm_b_im, ssm_c_re, ssm_c_im, ssm_d, w_glu, w_proj_attn, w_proj_ssm, w_out, ln_mix_g, ln_mix_b,
           w_peer_q, peer_sub_keys, peer_u, peer_v, ln_ffn_g, ln_ffn_b):
    batch, seq, _ = x.shape
    n = batch * seq
    x2 = x.reshape(n, D_MODEL)
    row = lambda v: v.reshape(1, -1).astype(F32)
    l = 0

    wi = w_in[l]
    w_main = jnp.concatenate([wi[:, OFF_SSM:OFF_KR], wi[:, OFF_GA:IN_WIDTH]], axis=1).astype(BF16)
    w_krc = wi[:, OFF_KR:OFF_GA]
    w_kr = jnp.concatenate([w_krc, _rot_half_cols(w_krc)], axis=1).astype(BF16)
    wq = w_uq[l].reshape(Q_LORA, N_HEADS, QK_HEAD)
    wq_ext = jnp.concatenate([wq, _rot_half_cols(wq[..., QK_NOPE:])], axis=-1)
    wq_ext = wq_ext.transpose(1, 0, 2).astype(BF16)
    wkv = w_ukv[l].reshape(KV_LORA, N_HEADS, QK_NOPE + V_HEAD).transpose(1, 0, 2).astype(BF16)

    cs = _rope_table(seq)
    u, z, kr = _inproj(x2, row(ln_in_g), row(ln_in_b), w_main, w_kr)

    q, k, v = _qkv(z, kr, cs, row(q_norm_g[l]), row(kv_norm_g[l]), wq_ext, wkv, batch, seq)
    attn = _attention(q, k, v).reshape(n, N_HEADS * V_HEAD)

    w, ws, wx, a16 = _ssm_wgen(ssm_lam_re[l], ssm_lam_im[l], ssm_log_dt[l], ssm_b_re[l], ssm_b_im[l],
                               ssm_c_re[l], ssm_c_im[l])
    g = _ssm_branch(u, batch, seq, w, ws, wx, a16, row(ssm_d[l]))

    merged = _merge(attn, g, z, w_glu[l].astype(BF16), w_proj_attn[l].astype(BF16), w_proj_ssm[l].astype(BF16))
    h2, h2t = _outproj(merged, x2, row(ln_in_g), row(ln_in_b), row(ln_mix_g[l]), row(ln_mix_b[l]),
                       w_out[l].astype(BF16))

    wcomb = _peer_wcomb(peer_sub_keys[l], w_peer_q[l].astype(BF16))
    st = _peer_scores(wcomb, h2t)
    e1, th, e2 = _peer_thresh(st)
    out = _peer_dense(h2, h2t, st, e1, th, e2, peer_u[l].astype(BF16), peer_v[l].T.astype(BF16),
                      row(ln_ffn_g[l]), row(ln_ffn_b[l]))
    return out.reshape(batch, seq, D_MODEL)
```

```python
import functools
import math

import jax
import jax.numpy as jnp
from jax import lax
from jax.experimental import pallas as pl
from jax.experimental.pallas import tpu as pltpu

F32 = jnp.float32
BF16 = jnp.bfloat16

D_MODEL = 2048
DEPTH = 1
CHUNK = 64
SSM_WIDTH = 1024
SSM_GROUP = 16
SSM_GROUPS = SSM_WIDTH // SSM_GROUP
SSM_STATE = 64
N_HEADS = 16
Q_LORA = 512
KV_LORA = 512
QK_NOPE = 128
QK_ROPE = 64
V_HEAD = 128
QK_HEAD = QK_NOPE + QK_ROPE
ROPE_THETA = 10000.0
PEER_HEADS = 8
N_KEYS = 128
N_EXPERTS = N_KEYS * N_KEYS
PEER_KEY_HALF = 128
PEER_TOPK = 16
LN_EPS = 1e-5
RMS_EPS = 1e-6
DEEPNORM_ALPHA = (2.0 * DEPTH) ** 0.25
OFF_SSM = 0
OFF_Q = OFF_SSM + SSM_WIDTH
OFF_KV = OFF_Q + Q_LORA
OFF_KR = OFF_KV + KV_LORA
OFF_GA = OFF_KR + QK_ROPE
OFF_GS = OFF_GA + D_MODEL
IN_WIDTH = OFF_GS + D_MODEL

LANES = 128
VMEM_LIMIT_BYTES = 56 * 1024 * 1024

SSM_CHUNK = 16
SSM_TILE_GROUPS = LANES // SSM_GROUP
SSM_TILES = SSM_WIDTH // LANES
SSM_TILE_STATE = SSM_TILE_GROUPS * SSM_STATE
NEG_BIG = -1e30


def _params(semantics):
    return pltpu.CompilerParams(dimension_semantics=semantics, vmem_limit_bytes=VMEM_LIMIT_BYTES)


def _dot(a, b):
    return jnp.dot(a, b, preferred_element_type=F32)


def _ln(x, g, b):
    mu = jnp.mean(x, axis=-1, keepdims=True)
    xc = x - mu
    var = jnp.mean(xc * xc, axis=-1, keepdims=True)
    return xc * lax.rsqrt(var + LN_EPS) * g + b


GELU_C = 0.7978845608028654
GELU_A = 0.044715


def _gelu(x):
    return 0.5 * x * (1.0 + jnp.tanh(GELU_C * (x + GELU_A * (x * x * x))))


def _gelu_x2(x):
    return x * (1.0 + jnp.tanh(x * (GELU_C + (GELU_C * GELU_A) * (x * x))))


def _sigmoid(x):
    return 1.0 / (1.0 + jnp.exp(-x))


def _rope_table_kernel(invf_ref, o_ref):
    ts = o_ref.shape[0]
    pos = (lax.broadcasted_iota(jnp.int32, (ts, LANES), 0) + pl.program_id(0) * ts).astype(F32)
    ang = pos * invf_ref[...]
    lane = lax.broadcasted_iota(jnp.int32, (ts, LANES), 1)
    o_ref[...] = jnp.where(lane < 2 * (QK_ROPE // 2), jnp.cos(ang), jnp.sin(ang))


def _rope_table(seq):
    half = QK_ROPE // 2
    invf = 1.0 / (ROPE_THETA ** (jnp.arange(half, dtype=F32) / half))
    invf4 = jnp.tile(invf, 4)[None, :]
    ts = min(seq, 1024)
    return pl.pallas_call(
        _rope_table_kernel,
        grid=(seq // ts,),
        in_specs=[pl.BlockSpec((1, LANES), lambda i: (0, 0))],
        out_specs=pl.BlockSpec((ts, LANES), lambda i: (i, 0)),
        out_shape=jax.ShapeDtypeStruct((seq, LANES), F32),
        compiler_params=_params(("parallel",)),
        name="rope_table",
    )(invf4)


def _inproj_kernel(x_ref, g_ref, b_ref, w_ref, wkr_ref, u_ref, z_ref, kr_ref, h_scr):
    j = pl.program_id(1)

    @pl.when(j == 0)
    def _():
        h = _ln(x_ref[...], g_ref[...], b_ref[...]).astype(BF16)
        h_scr[...] = h
        u_ref[...] = _dot(h, w_ref[...])
        kr_ref[...] = _dot(h, wkr_ref[...]).astype(kr_ref.dtype)

    @pl.when(j > 0)
    def _():
        z_ref[...] = _dot(h_scr[...], w_ref[...]).astype(z_ref.dtype)


def _inproj(x2, ln_g, ln_b, w_main, w_kr):
    n = x2.shape[0]
    tm = min(n, 1024)
    tn = 1024
    nj = w_main.shape[1] // tn
    return pl.pallas_call(
        _inproj_kernel,
        grid=(n // tm, nj),
        in_specs=[
            pl.BlockSpec((tm, D_MODEL), lambda i, j: (i, 0)),
            pl.BlockSpec((1, D_MODEL), lambda i, j: (0, 0)),
            pl.BlockSpec((1, D_MODEL), lambda i, j: (0, 0)),
            pl.BlockSpec((D_MODEL, tn), lambda i, j: (0, j)),
            pl.BlockSpec((D_MODEL, LANES), lambda i, j: (0, 0)),
        ],
        out_specs=[
            pl.BlockSpec((tm, tn), lambda i, j: (i, 0)),
            pl.BlockSpec((tm, tn), lambda i, j: (i, jnp.maximum(j - 1, 0))),
            pl.BlockSpec((tm, LANES), lambda i, j: (i, 0)),
        ],
        out_shape=[
            jax.ShapeDtypeStruct((n, tn), F32),
            jax.ShapeDtypeStruct((n, (nj - 1) * tn), BF16),
            jax.ShapeDtypeStruct((n, LANES), BF16),
        ],
        scratch_shapes=[pltpu.VMEM((tm, D_MODEL), BF16)],
        compiler_params=_params(("parallel", "arbitrary")),
        name="ln_inproj",
    )(x2, ln_g, ln_b, w_main, w_kr)


def _rope_mix(t2, cs):
    t = t2 * cs
    return t + pltpu.roll(t, QK_ROPE, 1)


def _qkv_kernel(cq_ref, ckv_ref, kr_ref, cs_ref, gq_ref, gkv_ref, wq_ref, wkv_ref,
                q_ref, k_ref, v_ref, cqn_scr, ckvn_scr, kro_scr):
    cs = cs_ref[...]

    @pl.when(pl.program_id(1) == 0)
    def _():
        def rms(c, g):
            c = c.astype(F32)
            return (c * lax.rsqrt(jnp.mean(c * c, axis=-1, keepdims=True) + RMS_EPS) * g).astype(BF16)

        cqn_scr[...] = rms(cq_ref[...], gq_ref[...])
        ckvn_scr[...] = rms(ckv_ref[...], gkv_ref[...])
        kro_scr[...] = _rope_mix(kr_ref[...].astype(F32), cs).astype(BF16)

    scale = math.log2(math.e) / math.sqrt(QK_HEAD)
    qe = _dot(cqn_scr[...], wq_ref[0])
    qr = _rope_mix(qe[:, QK_NOPE:], cs)
    q_ref[0, 0, :, :QK_NOPE] = (qe[:, :QK_NOPE] * scale).astype(BF16)
    q_ref[0, 0, :, QK_NOPE:] = (qr[:, :QK_ROPE] * scale).astype(BF16)
    kv = _dot(ckvn_scr[...], wkv_ref[0])
    k_ref[0, 0, :, :QK_NOPE] = kv[:, :QK_NOPE].astype(BF16)
    k_ref[0, 0, :, QK_NOPE:] = kro_scr[:, :QK_ROPE]
    v_ref[0, 0] = kv[:, QK_NOPE:].astype(BF16)


def _qkv(z, kr, cs, gq, gkv, wq_ext, wkv, batch, seq):
    n = z.shape[0]
    tm = min(seq, 2048)
    nsb = seq // tm
    return pl.pallas_call(
        _qkv_kernel,
        grid=(n // tm, N_HEADS),
        in_specs=[
            pl.BlockSpec((tm, Q_LORA), lambda i, h: (i, 0)),
            pl.BlockSpec((tm, KV_LORA), lambda i, h: (i, 1)),
            pl.BlockSpec((tm, LANES), lambda i, h: (i, 0)),
            pl.BlockSpec((tm, LANES), lambda i, h: (i % nsb, 0)),
            pl.BlockSpec((1, Q_LORA), lambda i, h: (0, 0)),
            pl.BlockSpec((1, KV_LORA), lambda i, h: (0, 0)),
            pl.BlockSpec((1, Q_LORA, 2 * LANES), lambda i, h: (h, 0, 0)),
            pl.BlockSpec((1, KV_LORA, 2 * LANES), lambda i, h: (h, 0, 0)),
        ],
        out_specs=[
            pl.BlockSpec((1, 1, tm, QK_HEAD), lambda i, h: (i // nsb, h, i % nsb, 0)),
            pl.BlockSpec((1, 1, tm, QK_HEAD), lambda i, h: (i // nsb, h, i % nsb, 0)),
            pl.BlockSpec((1, 1, tm, V_HEAD), lambda i, h: (i // nsb, h, i % nsb, 0)),
        ],
        out_shape=[
            jax.ShapeDtypeStruct((batch, N_HEADS, seq, QK_HEAD), BF16),
            jax.ShapeDtypeStruct((batch, N_HEADS, seq, QK_HEAD), BF16),
            jax.ShapeDtypeStruct((batch, N_HEADS, seq, V_HEAD), BF16),
        ],
        scratch_shapes=[
            pltpu.VMEM((tm, Q_LORA), BF16),
            pltpu.VMEM((tm, KV_LORA), BF16),
            pltpu.VMEM((tm, LANES), BF16),
        ],
        compiler_params=_params(("parallel", "arbitrary")),
        name="qkv_rope",
    )(z, z, kr, cs, gq, gkv, wq_ext, wkv)


ATT_ROWS = 512
ATT_CHAINS = 4
ATT_BK_MAIN = 1024


def _attn_kernel(q_ref, k_ref, v_ref, o_ref, m_scr, acc_scr, *, rows, chains, bkm):
    qi = pl.program_id(2)
    m_scr[...] = jnp.full(m_scr.shape, -jnp.inf, F32)
    acc_scr[...] = jnp.zeros(acc_scr.shape, F32)

    def load_kv(j, bk):
        start = pl.multiple_of(j * bk, bk)
        kj = k_ref[0, 0, pl.ds(start, bk), :]
        vj = jnp.concatenate([v_ref[0, 0, pl.ds(start, bk), :], jnp.ones((bk, LANES), BF16)], axis=1)
        return kj, vj

    def block(c, kj, vj, masked):
        q = q_ref[0, 0, c * rows:(c + 1) * rows, :]
        s = lax.dot_general(q, kj, (((1,), (1,)), ((), ())), preferred_element_type=F32)
        if masked:
            rc = lax.broadcasted_iota(jnp.int32, s.shape, 0) // CHUNK
            cc = lax.broadcasted_iota(jnp.int32, s.shape, 1) // CHUNK
            s = jnp.where(cc <= rc, s, NEG_BIG)
        m_prev = m_scr[c]
        m_new = jnp.maximum(m_prev, jnp.max(s, axis=-1, keepdims=True))
        alpha = jnp.exp2(m_prev - m_new)
        p = jnp.exp2(s - jnp.tile(m_new, (1, s.shape[1] // LANES)))
        acc_scr[c] = jnp.tile(alpha, (1, 2)) * acc_scr[c] + _dot(p.astype(BF16), vj)
        m_scr[c] = m_new

    def body(j, carry):
        kj, vj = load_kv(j, bkm)
        for c in range(chains):
            block(c, kj, vj, False)
        return carry

    lax.fori_loop(0, qi * (chains * rows // bkm), body, 0)
    for jj in range(chains):
        kj, vj = load_kv(qi * chains + jj, rows)
        for c in range(jj, chains):
            block(c, kj, vj, c == jj)
    for c in range(chains):
        a = acc_scr[c]
        o_ref[0, c * rows:(c + 1) * rows, :] = (a[:, :V_HEAD] / a[:, V_HEAD:]).astype(o_ref.dtype)


def _attention(q, k, v):
    batch, _, seq, _ = q.shape
    rows = min(seq, ATT_ROWS)
    chains = min(ATT_CHAINS, seq // rows)
    bq = rows * chains
    bkm = min(ATT_BK_MAIN, bq)
    return pl.pallas_call(
        functools.partial(_attn_kernel, rows=rows, chains=chains, bkm=bkm),
        grid=(batch, N_HEADS, seq // bq),
        in_specs=[
            pl.BlockSpec((1, 1, bq, QK_HEAD), lambda b, h, i: (b, h, i, 0)),
            pl.BlockSpec((1, 1, seq, QK_HEAD), lambda b, h, i: (b, h, 0, 0)),
            pl.BlockSpec((1, 1, seq, V_HEAD), lambda b, h, i: (b, h, 0, 0)),
        ],
        out_specs=pl.BlockSpec((1, bq, V_HEAD), lambda b, h, i: (b, i, h)),
        out_shape=jax.ShapeDtypeStruct((batch, seq, N_HEADS * V_HEAD), BF16),
        scratch_shapes=[
            pltpu.VMEM((chains, rows, LANES), F32),
            pltpu.VMEM((chains, rows, 2 * V_HEAD), F32),
        ],
        compiler_params=_params(("parallel", "parallel", "arbitrary")),
        name="flash_attn",
    )(q, k, v)


def _cpow(lr, li, ldt, k):
    dt = jnp.exp(ldt)
    mag = jnp.exp(lr * dt * k)
    ang = li * dt * k
    return mag * jnp.cos(ang), mag * jnp.sin(ang)


def _ssm_wgen_kernel(lr_r, li_r, ldt_r, lr_c, li_c, ldt_c, bre_ref, bim_ref, cre_ref, cim_ref,
                     w_ref, ws_ref, wx_ref, a16_ref):
    hp = lax.Precision.HIGHEST
    ns = SSM_TILE_STATE
    lr, li, ldt = lr_r[0], li_r[0], ldt_r[0]
    a_re, a_im = _cpow(lr, li, ldt, 1.0)
    den = lr * lr + li * li
    nr = a_re - 1.0
    ni = a_im
    cf_re = (nr * lr + ni * li) / den
    cf_im = (ni * lr - nr * li) / den
    bre, bim = bre_ref[0], bim_ref[0]
    cre, cim = cre_ref[0], cim_ref[0]
    w_ref[...] = jnp.zeros(w_ref.shape, w_ref.dtype)
    for k in range(SSM_CHUNK):
        p_re, p_im = _cpow(lr, li, ldt, float(k))
        m_re = cf_re * p_re - cf_im * p_im
        m_im = cf_re * p_im + cf_im * p_re
        bm_re = bre * m_re - bim * m_im
        bm_im = bre * m_im + bim * m_re
        bd = (jnp.dot(bm_re, cre, precision=hp, preferred_element_type=F32)
              - jnp.dot(bm_im, cim, precision=hp, preferred_element_type=F32)).astype(BF16)
        for j in range(SSM_CHUNK - k):
            i = j + k
            w_ref[0, j * LANES:(j + 1) * LANES, i * LANES:(i + 1) * LANES] = bd
        jj = SSM_CHUNK - 1 - k
        ws_ref[0, jj * LANES:(jj + 1) * LANES, :ns] = bm_re.astype(BF16)
        ws_ref[0, jj * LANES:(jj + 1) * LANES, ns:] = bm_im.astype(BF16)
    lrc, lic, ldtc = lr_c[0], li_c[0], ldt_c[0]
    for i in range(SSM_CHUNK):
        p_re, p_im = _cpow(lrc, lic, ldtc, float(i + 1))
        wx_ref[0, :ns, i * LANES:(i + 1) * LANES] = (p_re * cre - p_im * cim).astype(BF16)
        wx_ref[0, ns:, i * LANES:(i + 1) * LANES] = (-(p_re * cim + p_im * cre)).astype(BF16)
    q_re, q_im = _cpow(lr, li, ldt, float(SSM_CHUNK))
    a16_ref[0, :, :ns] = q_re
    a16_ref[0, :, ns:] = q_im


def _ssm_wgen(lam_re, lam_im, log_dt, b_re, b_im, c_re, c_im):
    nt, tg, ns = SSM_TILES, SSM_TILE_GROUPS, SSM_TILE_STATE
    eye = jnp.eye(tg, dtype=F32)
    bbd = lambda b: jnp.einsum("cgph,gk->cghkp", b.reshape(nt, tg, SSM_STATE, SSM_GROUP), eye).reshape(nt, LANES, ns)
    cbd = lambda c: jnp.einsum("cghp,gk->cgpkh", c.reshape(nt, tg, SSM_GROUP, SSM_STATE), eye).reshape(nt, ns, LANES)
    ldt = jnp.repeat(log_dt, SSM_STATE)
    rows = [a.reshape(nt, 1, ns) for a in (lam_re, lam_im, ldt)]
    cols = [a.reshape(nt, ns, 1) for a in (lam_re, lam_im, ldt)]
    kc = SSM_CHUNK * LANES
    row_spec = pl.BlockSpec((1, 1, ns), lambda c: (c, 0, 0))
    col_spec = pl.BlockSpec((1, ns, 1), lambda c: (c, 0, 0))
    return pl.pallas_call(
        _ssm_wgen_kernel,
        grid=(nt,),
        in_specs=[row_spec] * 3 + [col_spec] * 3 + [
            pl.BlockSpec((1, LANES, ns), lambda c: (c, 0, 0)),
            pl.BlockSpec((1, LANES, ns), lambda c: (c, 0, 0)),
            pl.BlockSpec((1, ns, LANES), lambda c: (c, 0, 0)),
            pl.BlockSpec((1, ns, LANES), lambda c: (c, 0, 0)),
        ],
        out_specs=[
            pl.BlockSpec((1, kc, kc), lambda c: (c, 0, 0)),
            pl.BlockSpec((1, kc, 2 * ns), lambda c: (c, 0, 0)),
            pl.BlockSpec((1, 2 * ns, kc), lambda c: (c, 0, 0)),
            pl.BlockSpec((1, 1, 2 * ns), lambda c: (c, 0, 0)),
        ],
        out_shape=[
            jax.ShapeDtypeStruct((nt, kc, kc), BF16),
            jax.ShapeDtypeStruct((nt, kc, 2 * ns), BF16),
            jax.ShapeDtypeStruct((nt, 2 * ns, kc), BF16),
            jax.ShapeDtypeStruct((nt, 1, 2 * ns), F32),
        ],
        compiler_params=_params(("parallel",)),
        name="ssm_wgen",
    )(*rows, *cols, bbd(b_re), bbd(b_im), cbd(c_re), cbd(c_im))


def _chunk_rows(u_ref, tb):
    return [u_ref[pl.ds(j, tb, stride=SSM_CHUNK), :] for j in range(SSM_CHUNK)]


def _ssm_sum_kernel(u_ref, ws_ref, s_ref, *, tb):
    u16 = jnp.concatenate(_chunk_rows(u_ref, tb), axis=1).astype(BF16)
    s_ref[0] = _dot(u16, ws_ref[0])


def _ssm_scan_kernel(s_ref, a16_ref, x_ref):
    ns = SSM_TILE_STATE
    a_re = a16_ref[0, :, :ns]
    a_im = a16_ref[0, :, ns:]
    nrow = s_ref.shape[1]

    def body(r, carry):
        xr, xi = carry
        x_ref[0, pl.ds(r, 1), :ns] = xr
        x_ref[0, pl.ds(r, 1), ns:] = xi
        sr = s_ref[0, pl.ds(r, 1), :ns]
        si = s_ref[0, pl.ds(r, 1), ns:]
        return a_re * xr - a_im * xi + sr, a_re * xi + a_im * xr + si

    zero = jnp.zeros((1, ns), F32)
    lax.fori_loop(0, nrow, body, (zero, zero), unroll=8)


def _ssm_out_kernel(u_ref, x_ref, w_ref, wx_ref, d_ref, g_ref, *, tb):
    us = _chunk_rows(u_ref, tb)
    u16 = jnp.concatenate(us, axis=1).astype(BF16)
    y16 = _dot(u16, w_ref[0]) + _dot(x_ref[0].astype(BF16), wx_ref[0])
    d = d_ref[...]
    for i in range(SSM_CHUNK):
        y = y16[:, i * LANES:(i + 1) * LANES] + d * us[i]
        g_ref[pl.ds(i, tb, stride=SSM_CHUNK), :] = _gelu(y)


def _ssm_branch(u, batch, seq, w, ws, wx, a16, d_skip):
    n = u.shape[0]
    nt, ns = SSM_TILES, SSM_TILE_STATE
    kc = SSM_CHUNK * LANES
    nb = n // SSM_CHUNK
    nbs = seq // SSM_CHUNK
    tb = min(nbs, 256)
    tok = tb * SSM_CHUNK
    s = pl.pallas_call(
        functools.partial(_ssm_sum_kernel, tb=tb),
        grid=(nt, nb // tb),
        in_specs=[
            pl.BlockSpec((tok, LANES), lambda c, r: (r, c)),
            pl.BlockSpec((1, kc, 2 * ns), lambda c, r: (c, 0, 0)),
        ],
        out_specs=pl.BlockSpec((1, tb, 2 * ns), lambda c, r: (c, r, 0)),
        out_shape=jax.ShapeDtypeStruct((nt, nb, 2 * ns), F32),
        compiler_params=_params(("parallel", "parallel")),
        name="ssm_chunk_sum",
    )(u, ws)
    x = pl.pallas_call(
        _ssm_scan_kernel,
        grid=(nt, batch),
        in_specs=[
            pl.BlockSpec((1, nbs, 2 * ns), lambda c, b: (c, b, 0)),
            pl.BlockSpec((1, 1, 2 * ns), lambda c, b: (c, 0, 0)),
        ],
        out_specs=pl.BlockSpec((1, nbs, 2 * ns), lambda c, b: (c, b, 0)),
        out_shape=jax.ShapeDtypeStruct((nt, nb, 2 * ns), F32),
        compiler_params=_params(("parallel", "parallel")),
        name="ssm_state_scan",
    )(s, a16)
    return pl.pallas_call(
        functools.partial(_ssm_out_kernel, tb=tb),
        grid=(nt, nb // tb),
        in_specs=[
            pl.BlockSpec((tok, LANES), lambda c, r: (r, c)),
            pl.BlockSpec((1, tb, 2 * ns), lambda c, r: (c, r, 0)),
            pl.BlockSpec((1, kc, kc), lambda c, r: (c, 0, 0)),
            pl.BlockSpec((1, 2 * ns, kc), lambda c, r: (c, 0, 0)),
            pl.BlockSpec((1, LANES), lambda c, r: (0, c)),
        ],
        out_specs=pl.BlockSpec((tok, LANES), lambda c, r: (r, c)),
        out_shape=jax.ShapeDtypeStruct((n, SSM_WIDTH), F32),
        compiler_params=_params(("parallel", "parallel")),
        name="ssm_chunk_out",
    )(u, x, w, wx, d_skip)


def _merge_kernel(attn_ref, g_ref, ga_ref, gs_ref, wglu_ref, wpa_ref, wps_ref, o_ref, ssm_scr):
    @pl.when(pl.program_id(1) == 0)
    def _():
        g = g_ref[...]
        ssm_scr[...] = (g * _sigmoid(_dot(g.astype(BF16), wglu_ref[...]))).astype(BF16)

    pa = _dot(attn_ref[...], wpa_ref[...])
    ps = _dot(ssm_scr[...], wps_ref[...])
    o_ref[...] = (_sigmoid(ga_ref[...].astype(F32)) * pa + _sigmoid(gs_ref[...].astype(F32)) * ps).astype(o_ref.dtype)


def _merge(attn, g, z, w_glu, w_pa, w_ps):
    n = attn.shape[0]
    tm = min(n, 512)
    tn = 1024
    nj = D_MODEL // tn
    return pl.pallas_call(
        _merge_kernel,
        grid=(n // tm, nj),
        in_specs=[
            pl.BlockSpec((tm, N_HEADS * V_HEAD), lambda i, j: (i, 0)),
            pl.BlockSpec((tm, SSM_WIDTH), lambda i, j: (i, 0)),
            pl.BlockSpec((tm, tn), lambda i, j: (i, 1 + j)),
            pl.BlockSpec((tm, tn), lambda i, j: (i, 1 + nj + j)),
            pl.BlockSpec((SSM_WIDTH, SSM_WIDTH), lambda i, j: (0, 0)),
            pl.BlockSpec((N_HEADS * V_HEAD, tn), lambda i, j: (0, j)),
            pl.BlockSpec((SSM_WIDTH, tn), lambda i, j: (0, j)),
        ],
        out_specs=pl.BlockSpec((tm, tn), lambda i, j: (i, j)),
        out_shape=jax.ShapeDtypeStruct((n, D_MODEL), BF16),
        scratch_shapes=[pltpu.VMEM((tm, SSM_WIDTH), BF16)],
        compiler_params=_params(("parallel", "arbitrary")),
        name="glu_merge",
    )(attn, g, z, z, w_glu, w_pa, w_ps)


def _outproj_kernel(m_ref, x_ref, g1_ref, b1_ref, g2_ref, b2_ref, w_ref, h2_ref, h2t_ref):
    y = _dot(m_ref[...], w_ref[...])
    h = _ln(x_ref[...], g1_ref[...], b1_ref[...])
    h2 = _ln(DEEPNORM_ALPHA * h + y, g2_ref[...], b2_ref[...])
    h2_ref[...] = h2.astype(h2_ref.dtype)
    h2t_ref[...] = h2.T.astype(h2t_ref.dtype)


def _outproj(merged, x2, g1, b1, g2, b2, w_out):
    n = merged.shape[0]
    tm = min(n, 512)
    vec = pl.BlockSpec((1, D_MODEL), lambda i: (0, 0))
    return pl.pallas_call(
        _outproj_kernel,
        grid=(n // tm,),
        in_specs=[
            pl.BlockSpec((tm, D_MODEL), lambda i: (i, 0)),
            pl.BlockSpec((tm, D_MODEL), lambda i: (i, 0)),
            vec, vec, vec, vec,
            pl.BlockSpec((D_MODEL, D_MODEL), lambda i: (0, 0)),
        ],
        out_specs=[
            pl.BlockSpec((tm, D_MODEL), lambda i: (i, 0)),
            pl.BlockSpec((D_MODEL, tm), lambda i: (0, i)),
        ],
        out_shape=[
            jax.ShapeDtypeStruct((n, D_MODEL), BF16),
            jax.ShapeDtypeStruct((D_MODEL, n), BF16),
        ],
        compiler_params=_params(("parallel",)),
        name="outproj_ln",
    )(merged, x2, g1, b1, g2, b2, w_out)


def _peer_wcomb_kernel(keys_ref, wpq_ref, o_ref):
    o_ref[...] = lax.dot_general(keys_ref[0].astype(BF16), wpq_ref[...], (((1,), (1,)), ((), ())),
                                 preferred_element_type=F32).astype(o_ref.dtype)


def _peer_wcomb(sub_keys, w_pq):
    nhc = PEER_HEADS * 2
    keys = sub_keys.reshape(nhc, N_KEYS, PEER_KEY_HALF)
    return pl.pallas_call(
        _peer_wcomb_kernel,
        grid=(nhc,),
        in_specs=[
            pl.BlockSpec((1, N_KEYS, PEER_KEY_HALF), lambda i: (i, 0, 0)),
            pl.BlockSpec((D_MODEL, PEER_KEY_HALF), lambda i: (0, i)),
        ],
        out_specs=pl.BlockSpec((N_KEYS, D_MODEL), lambda i: (i, 0)),
        out_shape=jax.ShapeDtypeStruct((nhc * N_KEYS, D_MODEL), BF16),
        compiler_params=_params(("parallel",)),
        name="peer_wcomb",
    )(keys, w_pq)


def _peer_scores_kernel(w_ref, xt_ref, o_ref):
    o_ref[...] = _dot(w_ref[...], xt_ref[...])


def _peer_scores(wcomb, h2t):
    n = h2t.shape[1]
    tl = min(n, 512)
    rows = wcomb.shape[0]
    return pl.pallas_call(
        _peer_scores_kernel,
        grid=(n // tl,),
        in_specs=[
            pl.BlockSpec((rows, D_MODEL), lambda i: (0, 0)),
            pl.BlockSpec((D_MODEL, tl), lambda i: (0, i)),
        ],
        out_specs=pl.BlockSpec((rows, tl), lambda i: (0, i)),
        out_shape=jax.ShapeDtypeStruct((rows, n), F32),
        compiler_params=_params(("parallel",)),
        name="peer_scores",
    )(wcomb, h2t)


PEER_LIST = PEER_TOPK + 1
PEER_SUB = 4 * N_KEYS


def _insert_sorted(t, v):
    out = []
    for r in range(len(t)):
        out.append(jnp.maximum(t[r], v))
        v = jnp.minimum(t[r], v)
    return out


def _peer_thresh_kernel(s_ref, e1_ref, th_ref, e2_ref):
    shape = s_ref.shape[1:]
    neg = jnp.full(shape, -jnp.inf, F32)

    def top_list(base):
        def body(n, t):
            return tuple(_insert_sorted(list(t), s_ref[base + n]))
        return lax.fori_loop(0, N_KEYS, body, (neg,) * PEER_LIST, unroll=4)

    def head(h, carry):
        base1 = h * (2 * N_KEYS)
        base2 = base1 + N_KEYS
        a = top_list(base1)
        b = top_list(base2)
        t = [neg] * PEER_LIST
        for i in range(1, PEER_LIST + 1):
            for j in range(1, PEER_LIST // i + 1):
                t = _insert_sorted(t, a[i - 1] + b[j - 1])
        tau = 0.5 * (t[PEER_TOPK - 1] + t[PEER_TOPK])
        top = a[0] + b[0]
        z = jnp.exp(t[0] - top)
        for r in range(1, PEER_TOPK):
            z = z + jnp.exp(t[r] - top)
        rz = 0.5 / z
        a0, b0 = a[0], b[0]

        def write(n, c):
            s1 = s_ref[base1 + n]
            s2 = s_ref[base2 + n]
            e1_ref[h * N_KEYS + n] = jnp.exp(s1 - a0)
            th_ref[h * N_KEYS + n] = tau - s1
            e2_ref[h * N_KEYS + n] = jnp.exp(s2 - b0) * rz
            return c

        lax.fori_loop(0, N_KEYS, write, 0, unroll=4)
        return carry

    lax.fori_loop(0, PEER_HEADS, head, 0)


def _peer_thresh(st):
    rows, n = st.shape
    sub = 8
    st3 = st.reshape(rows, n // LANES, LANES)
    nb = min(sub, n // LANES)
    half = rows // 2
    outs = pl.pallas_call(
        _peer_thresh_kernel,
        grid=(n // (nb * LANES),),
        in_specs=[pl.BlockSpec((rows, nb, LANES), lambda i: (0, i, 0))],
        out_specs=[pl.BlockSpec((half, nb, LANES), lambda i: (0, i, 0))] * 3,
        out_shape=[jax.ShapeDtypeStruct((half, n // LANES, LANES), F32)] * 3,
        compiler_params=_params(("parallel",)),
        name="peer_thresh",
    )(st3)
    return [o.reshape(PEER_HEADS, N_KEYS, n) for o in outs]


def _peer_dense_kernel(xt_ref, u_ref, vt_ref, s2_ref, e2_ref, e1_ref, th_ref, h2_ref, g_ref, b_ref,
                       o_ref, acc_scr, a_scr, *, sub):
    e = pl.program_id(1)
    eb = u_ref.shape[0]

    @pl.when(e == 0)
    def _():
        acc_scr[...] = jnp.zeros(acc_scr.shape, F32)

    xt = xt_ref[...]
    nsub = eb // sub
    score = lambda s: _dot(u_ref[s * sub:(s + 1) * sub, :], xt)
    st_next = score(0)
    for s in range(nsub):
        st, st_next = st_next, (score(s + 1) if s + 1 < nsub else None)
        for ii in range(sub // N_KEYS):
            il = s * (sub // N_KEYS) + ii
            gate = None
            for h in range(PEER_HEADS):
                term = e1_ref[h, il:il + 1, :] * jnp.where(s2_ref[h] >= th_ref[h, il:il + 1, :], e2_ref[h], 0.0)
                gate = term if gate is None else gate + term
            a = _gelu_x2(st[ii * N_KEYS:(ii + 1) * N_KEYS, :]) * gate
            a_scr[s * sub + ii * N_KEYS:s * sub + (ii + 1) * N_KEYS, :] = a.astype(BF16)
        acc_scr[...] += _dot(vt_ref[:, s * sub:(s + 1) * sub], a_scr[s * sub:(s + 1) * sub, :])

    @pl.when(e == pl.num_programs(1) - 1)
    def _():
        y = acc_scr[...].T
        o_ref[...] = _ln(DEEPNORM_ALPHA * h2_ref[...].astype(F32) + y, g_ref[...], b_ref[...])


def _peer_dense(h2, h2t, st, e1, th, e2, u_tab, vt_tab, ln_g, ln_b):
    n = h2.shape[0]
    t = min(n, 512)
    eb = 8 * N_KEYS
    sub = PEER_SUB
    st4 =st.reshape(PEER_HEADS, 2, N_KEYS, n)
    vec = pl.BlockSpec((1, D_MODEL), lambda i, e: (0, 0))
    return pl.pallas_call(
        functools.partial(_peer_dense_kernel, sub=sub),
        grid=(n // t, N_EXPERTS // eb),
        in_specs=[
            pl.BlockSpec((D_MODEL, t), lambda i, e: (0, i)),
            pl.BlockSpec((eb, D_MODEL), lambda i, e: (e, 0)),
            pl.BlockSpec((D_MODEL, eb), lambda i, e: (0, e)),
            pl.BlockSpec((PEER_HEADS, None, N_KEYS, t), lambda i, e: (0, 1, 0, i)),
            pl.BlockSpec((PEER_HEADS, N_KEYS, t), lambda i, e: (0, 0, i)),
            pl.BlockSpec((PEER_HEADS, eb // N_KEYS, t), lambda i, e: (0, e, i)),
            pl.BlockSpec((PEER_HEADS, eb // N_KEYS, t), lambda i, e: (0, e, i)),
            pl.BlockSpec((t, D_MODEL), lambda i, e: (i, 0)),
            vec, vec,
        ],
        out_specs=pl.BlockSpec((t, D_MODEL), lambda i, e: (i, 0)),
        out_shape=jax.ShapeDtypeStruct((n, D_MODEL), F32),
        scratch_shapes=[
            pltpu.VMEM((D_MODEL, t), F32),
            pltpu.VMEM((eb, t), BF16),
        ],
        compiler_params=_params(("parallel", "arbitrary")),
        name="peer_dense",
    )(h2t, u_tab, vt_tab, st4, e2, e1, th, h2, ln_g, ln_b)


def _rot_half_cols(w):
    half = QK_ROPE // 2
    return jnp.concatenate([-w[..., half:], w[..., :half]], axis=-1)


def kernel(x, ln_in_g, ln_in_b, w_in, q_norm_g, w_uq, kv_norm_g, w_ukv, ssm_lam_re, ssm_lam_im, ssm_log_dt,
           ssm_b_re, ssm_b_im, ssm_c_re, ssm_c_im, ssm_d, w_glu, w_proj_attn, w_proj_ssm, w_out, ln_mix_g, ln_mix_b,
           w_peer_q, peer_sub_keys, peer_u, peer_v, ln_ffn_g, ln_ffn_b):
    batch, seq, _ = x.shape
    n = batch * seq
    x2 = x.reshape(n, D_MODEL)
    row = lambda v: v.reshape(1, -1).astype(F32)
    l = 0

    wi = w_in[l]
    w_main = jnp.concatenate([wi[:, OFF_SSM:OFF_KR], wi[:, OFF_GA:IN_WIDTH]], axis=1).astype(BF16)
    w_krc = wi[:, OFF_KR:OFF_GA]
    w_kr = jnp.concatenate([w_krc, _rot_half_cols(w_krc)], axis=1).astype(BF16)
    wq = w_uq[l].reshape(Q_LORA, N_HEADS, QK_HEAD)
    wq_ext = jnp.concatenate([wq, _rot_half_cols(wq[..., QK_NOPE:])], axis=-1)
    wq_ext = wq_ext.transpose(1, 0, 2).astype(BF16)
    wkv = w_ukv[l].reshape(KV_LORA, N_HEADS, QK_NOPE + V_HEAD).transpose(1, 0, 2).astype(BF16)

    cs = _rope_table(seq)
    u, z, kr = _inproj(x2, row(ln_in_g), row(ln_in_b), w_main, w_kr)

    q, k, v = _qkv(z, kr, cs, row(q_norm_g[l]), row(kv_norm_g[l]), wq_ext, wkv, batch, seq)
    attn = _attention(q, k, v).reshape(n, N_HEADS * V_HEAD)

    w, ws, wx, a16 = _ssm_wgen(ssm_lam_re[l], ssm_lam_im[l], ssm_log_dt[l], ssm_b_re[l], ssm_b_im[l],
                               ssm_c_re[l], ssm_c_im[l])
    g = _ssm_branch(u, batch, seq, w, ws, wx, a16, row(ssm_d[l]))

    merged = _merge(attn, g, z, w_glu[l].astype(BF16), w_proj_attn[l].astype(BF16), w_proj_ssm[l].astype(BF16))
    h2, h2t = _outproj(merged, x2, row(ln_in_g), row(ln_in_b), row(ln_mix_g[l]), row(ln_mix_b[l]),
                       w_out[l].astype(BF16))

    wcomb = _peer_wcomb(peer_sub_keys[l], w_peer_q[l].astype(BF16))
    st = _peer_scores(wcomb, h2t)
    e1, th, e2 = _peer_thresh(st)
    out = _peer_dense(h2, h2t, st, e1, th, e2, peer_u[l].astype(BF16), peer_v[l].T.astype(BF16),
                      row(ln_ffn_g[l]), row(ln_ffn_b[l]))
    return out.reshape(batch, seq, D_MODEL)
```

```python
import functools
import math

import jax
import jax.numpy as jnp
from jax import lax
from jax.experimental import pallas as pl
from jax.experimental.pallas import tpu as pltpu

F32 = jnp.float32
BF16 = jnp.bfloat16

D_MODEL = 2048
DEPTH = 1
CHUNK = 64
SSM_WIDTH = 1024
SSM_GROUP = 16
SSM_GROUPS = SSM_WIDTH // SSM_GROUP
SSM_STATE = 64
N_HEADS = 16
Q_LORA = 512
KV_LORA = 512
QK_NOPE = 128
QK_ROPE = 64
V_HEAD = 128
QK_HEAD = QK_NOPE + QK_ROPE
ROPE_THETA = 10000.0
PEER_HEADS = 8
N_KEYS = 128
N_EXPERTS = N_KEYS * N_KEYS
PEER_KEY_HALF = 128
PEER_TOPK = 16
LN_EPS = 1e-5
RMS_EPS = 1e-6
DEEPNORM_ALPHA = (2.0 * DEPTH) ** 0.25
OFF_SSM = 0
OFF_Q = OFF_SSM + SSM_WIDTH
OFF_KV = OFF_Q + Q_LORA
OFF_KR = OFF_KV + KV_LORA
OFF_GA = OFF_KR + QK_ROPE
OFF_GS = OFF_GA + D_MODEL
IN_WIDTH = OFF_GS + D_MODEL

LANES = 128
VMEM_LIMIT_BYTES = 56 * 1024 * 1024

SSM_CHUNK = 16
SSM_TILE_GROUPS = LANES // SSM_GROUP
SSM_TILES = SSM_WIDTH // LANES
SSM_TILE_STATE = SSM_TILE_GROUPS * SSM_STATE
NEG_BIG = -1e30


def _params(semantics):
    return pltpu.CompilerParams(dimension_semantics=semantics, vmem_limit_bytes=VMEM_LIMIT_BYTES)


def _dot(a, b):
    return jnp.dot(a, b, preferred_element_type=F32)


def _ln(x, g, b):
    mu = jnp.mean(x, axis=-1, keepdims=True)
    xc = x - mu
    var = jnp.mean(xc * xc, axis=-1, keepdims=True)
    return xc * lax.rsqrt(var + LN_EPS) * g + b


GELU_C = 0.7978845608028654
GELU_A = 0.044715


def _gelu(x):
    return 0.5 * x * (1.0 + jnp.tanh(GELU_C * (x + GELU_A * (x * x * x))))


def _gelu_x2(x):
    return x * (1.0 + jnp.tanh(x * (GELU_C + (GELU_C * GELU_A) * (x * x))))


def _sigmoid(x):
    return 1.0 / (1.0 + jnp.exp(-x))


def _rope_table_kernel(invf_ref, o_ref):
    ts = o_ref.shape[0]
    pos = (lax.broadcasted_iota(jnp.int32, (ts, LANES), 0) + pl.program_id(0) * ts).astype(F32)
    ang = pos * invf_ref[...]
    lane = lax.broadcasted_iota(jnp.int32, (ts, LANES), 1)
    o_ref[...] = jnp.where(lane < 2 * (QK_ROPE // 2), jnp.cos(ang), jnp.sin(ang))


def _rope_table(seq):
    half = QK_ROPE // 2
    invf = 1.0 / (ROPE_THETA ** (jnp.arange(half, dtype=F32) / half))
    invf4 = jnp.tile(invf, 4)[None, :]
    ts = min(seq, 1024)
    return pl.pallas_call(
        _rope_table_kernel,
        grid=(seq // ts,),
        in_specs=[pl.BlockSpec((1, LANES), lambda i: (0, 0))],
        out_specs=pl.BlockSpec((ts, LANES), lambda i: (i, 0)),
        out_shape=jax.ShapeDtypeStruct((seq, LANES), F32),
        compiler_params=_params(("parallel",)),
        name="rope_table",
    )(invf4)


def _inproj_kernel(x_ref, g_ref, b_ref, w_ref, wkr_ref, u_ref, z_ref, kr_ref, h_scr):
    j = pl.program_id(1)

    @pl.when(j == 0)
    def _():
        h = _ln(x_ref[...], g_ref[...], b_ref[...]).astype(BF16)
        h_scr[...] = h
        u_ref[...] = _dot(h, w_ref[...])
        kr_ref[...] = _dot(h, wkr_ref[...]).astype(kr_ref.dtype)

    @pl.when(j > 0)
    def _():
        z_ref[...] = _dot(h_scr[...], w_ref[...]).astype(z_ref.dtype)


def _inproj(x2, ln_g, ln_b, w_main, w_kr):
    n = x2.shape[0]
    tm = min(n, 1024)
    tn = 1024
    nj = w_main.shape[1] // tn
    return pl.pallas_call(
        _inproj_kernel,
        grid=(n // tm, nj),
        in_specs=[
            pl.BlockSpec((tm, D_MODEL), lambda i, j: (i, 0)),
            pl.BlockSpec((1, D_MODEL), lambda i, j: (0, 0)),
            pl.BlockSpec((1, D_MODEL), lambda i, j: (0, 0)),
            pl.BlockSpec((D_MODEL, tn), lambda i, j: (0, j)),
            pl.BlockSpec((D_MODEL, LANES), lambda i, j: (0, 0)),
        ],
        out_specs=[
            pl.BlockSpec((tm, tn), lambda i, j: (i, 0)),
            pl.BlockSpec((tm, tn), lambda i, j: (i, jnp.maximum(j - 1, 0))),
            pl.BlockSpec((tm, LANES), lambda i, j: (i, 0)),
        ],
        out_shape=[
            jax.ShapeDtypeStruct((n, tn), F32),
            jax.ShapeDtypeStruct((n, (nj - 1) * tn), BF16),
            jax.ShapeDtypeStruct((n, LANES), BF16),
        ],
        scratch_shapes=[pltpu.VMEM((tm, D_MODEL), BF16)],
        compiler_params=_params(("parallel", "arbitrary")),
        name="ln_inproj",
    )(x2, ln_g, ln_b, w_main, w_kr)


def _rope_mix(t2, cs):
    t = t2 * cs
    return t + pltpu.roll(t, QK_ROPE, 1)


def _qkv_kernel(cq_ref, ckv_ref, kr_ref, cs_ref, gq_ref, gkv_ref, wq_ref, wkv_ref,
                q_ref, k_ref, v_ref, cqn_scr, ckvn_scr, kro_scr):
    cs = cs_ref[...]

    @pl.when(pl.program_id(1) == 0)
    def _():
        def rms(c, g):
            c = c.astype(F32)
            return (c * lax.rsqrt(jnp.mean(c * c, axis=-1, keepdims=True) + RMS_EPS) * g).astype(BF16)

        cqn_scr[...] = rms(cq_ref[...], gq_ref[...])
        ckvn_scr[...] = rms(ckv_ref[...], gkv_ref[...])
        kro_scr[...] = _rope_mix(kr_ref[...].astype(F32), cs).astype(BF16)

    scale = math.log2(math.e) / math.sqrt(QK_HEAD)
    qe = _dot(cqn_scr[...], wq_ref[0])
    qr = _rope_mix(qe[:, QK_NOPE:], cs)
    q_ref[0, 0, :, :QK_NOPE] = (qe[:, :QK_NOPE] * scale).astype(BF16)
    q_ref[0, 0, :, QK_NOPE:] = (qr[:, :QK_ROPE] * scale).astype(BF16)
    kv = _dot(ckvn_scr[...], wkv_ref[0])
    k_ref[0, 0, :, :QK_NOPE] = kv[:, :QK_NOPE].astype(BF16)
    k_ref[0, 0, :, QK_NOPE:] = kro_scr[:, :QK_ROPE]
    v_ref[0, 0] = kv[:, QK_NOPE:].astype(BF16)


def _qkv(z, kr, cs, gq, gkv, wq_ext, wkv, batch, seq):
    n = z.shape[0]
    tm = min(seq, 2048)
    nsb = seq // tm
    return pl.pallas_call(
        _qkv_kernel,
        grid=(n // tm, N_HEADS),
        in_specs=[
            pl.BlockSpec((tm, Q_LORA), lambda i, h: (i, 0)),
            pl.BlockSpec((tm, KV_LORA), lambda i, h: (i, 1)),
            pl.BlockSpec((tm, LANES), lambda i, h: (i, 0)),
            pl.BlockSpec((tm, LANES), lambda i, h: (i % nsb, 0)),
            pl.BlockSpec((1, Q_LORA), lambda i, h: (0, 0)),
            pl.BlockSpec((1, KV_LORA), lambda i, h: (0, 0)),
            pl.BlockSpec((1, Q_LORA, 2 * LANES), lambda i, h: (h, 0, 0)),
            pl.BlockSpec((1, KV_LORA, 2 * LANES), lambda i, h: (h, 0, 0)),
        ],
        out_specs=[
            pl.BlockSpec((1, 1, tm, QK_HEAD), lambda i, h: (i // nsb, h, i % nsb, 0)),
            pl.BlockSpec((1, 1, tm, QK_HEAD), lambda i, h: (i // nsb, h, i % nsb, 0)),
            pl.BlockSpec((1, 1, tm, V_HEAD), lambda i, h: (i // nsb, h, i % nsb, 0)),
        ],
        out_shape=[
            jax.ShapeDtypeStruct((batch, N_HEADS, seq, QK_HEAD), BF16),
            jax.ShapeDtypeStruct((batch, N_HEADS, seq, QK_HEAD), BF16),
            jax.ShapeDtypeStruct((batch, N_HEADS, seq, V_HEAD), BF16),
        ],
        scratch_shapes=[
            pltpu.VMEM((tm, Q_LORA), BF16),
            pltpu.VMEM((tm, KV_LORA), BF16),
            pltpu.VMEM((tm, LANES), BF16),
        ],
        compiler_params=_params(("parallel", "arbitrary")),
        name="qkv_rope",
    )(z, z, kr, cs, gq, gkv, wq_ext, wkv)


ATT_ROWS = 512
ATT_CHAINS = 8
ATT_BK_MAIN = 1024


def _attn_kernel(q_ref, k_ref, v_ref, o_ref, m_scr, acc_scr, *, rows, chains, bkm):
    qi = pl.program_id(2)
    m_scr[...] = jnp.full(m_scr.shape, -jnp.inf, F32)
    acc_scr[...] = jnp.zeros(acc_scr.shape, F32)

    def load_kv(j, bk):
        start = pl.multiple_of(j * bk, bk)
        kj = k_ref[0, 0, pl.ds(start, bk), :]
        vj = jnp.concatenate([v_ref[0, 0, pl.ds(start, bk), :], jnp.ones((bk, LANES), BF16)], axis=1)
        return kj, vj

    def block(c, kj, vj, masked):
        q = q_ref[0, 0, c * rows:(c + 1) * rows, :]
        s = lax.dot_general(q, kj, (((1,), (1,)), ((), ())), preferred_element_type=F32)
        if masked:
            rc = lax.broadcasted_iota(jnp.int32, s.shape, 0) // CHUNK
            cc = lax.broadcasted_iota(jnp.int32, s.shape, 1) // CHUNK
            s = jnp.where(cc <= rc, s, NEG_BIG)
        m_prev = m_scr[c]
        m_new = jnp.maximum(m_prev, jnp.max(s, axis=-1, keepdims=True))
        alpha = jnp.exp2(m_prev - m_new)
        p = jnp.exp2(s - jnp.tile(m_new, (1, s.shape[1] // LANES)))
        acc_scr[c] = jnp.tile(alpha, (1, 2)) * acc_scr[c] + _dot(p.astype(BF16), vj)
        m_scr[c] = m_new

    def body(j, carry):
        kj, vj = load_kv(j, bkm)
        for c in range(chains):
            block(c, kj, vj, False)
        return carry

    lax.fori_loop(0, qi * (chains * rows // bkm), body, 0)
    for jj in range(chains):
        kj, vj = load_kv(qi * chains + jj, rows)
        for c in range(jj, chains):
            block(c, kj, vj, c == jj)
    for c in range(chains):
        a = acc_scr[c]
        o_ref[0, c * rows:(c + 1) * rows, :] = (a[:, :V_HEAD] / a[:, V_HEAD:]).astype(o_ref.dtype)


def _attention(q, k, v):
    batch, _, seq, _ = q.shape
    rows = min(seq, ATT_ROWS)
    chains = min(ATT_CHAINS, seq // rows)
    bq = rows * chains
    bkm = min(ATT_BK_MAIN, bq)
    return pl.pallas_call(
        functools.partial(_attn_kernel, rows=rows, chains=chains, bkm=bkm),
        grid=(batch, N_HEADS, seq // bq),
        in_specs=[
            pl.BlockSpec((1, 1, bq, QK_HEAD), lambda b, h, i: (b, h, i, 0)),
            pl.BlockSpec((1, 1, seq, QK_HEAD), lambda b, h, i: (b, h, 0, 0)),
            pl.BlockSpec((1, 1, seq, V_HEAD), lambda b, h, i: (b, h, 0, 0)),
        ],
        out_specs=pl.BlockSpec((1, bq, V_HEAD), lambda b, h, i: (b, i, h)),
        out_shape=jax.ShapeDtypeStruct((batch, seq, N_HEADS * V_HEAD), BF16),
        scratch_shapes=[
            pltpu.VMEM((chains, rows, LANES), F32),
            pltpu.VMEM((chains, rows, 2 * V_HEAD), F32),
        ],
        compiler_params=_params(("parallel", "parallel", "arbitrary")),
        name="flash_attn",
    )(q, k, v)


def _cpow(lr, li, ldt, k):
    dt = jnp.exp(ldt)
    mag = jnp.exp(lr * dt * k)
    ang = li * dt * k
    return mag * jnp.cos(ang), mag * jnp.sin(ang)


def _ssm_wgen_kernel(lr_r, li_r, ldt_r, lr_c, li_c, ldt_c, bre_ref, bim_ref, cre_ref, cim_ref,
                     w_ref, ws_ref, wx_ref, a16_ref):
    hp = lax.Precision.HIGHEST
    ns = SSM_TILE_STATE
    lr, li, ldt = lr_r[0], li_r[0], ldt_r[0]
    a_re, a_im = _cpow(lr, li, ldt, 1.0)
    den = lr * lr + li * li
    nr = a_re - 1.0
    ni = a_im
    cf_re = (nr * lr + ni * li) / den
    cf_im = (ni * lr - nr * li) / den
    bre, bim = bre_ref[0], bim_ref[0]
    cre, cim = cre_ref[0], cim_ref[0]
    w_ref[...] = jnp.zeros(w_ref.shape, w_ref.dtype)
    for k in range(SSM_CHUNK):
        p_re, p_im = _cpow(lr, li, ldt, float(k))
        m_re = cf_re * p_re - cf_im * p_im
        m_im = cf_re * p_im + cf_im * p_re
        bm_re = bre * m_re - bim * m_im
        bm_im = bre * m_im + bim * m_re
        bd = (jnp.dot(bm_re, cre, precision=hp, preferred_element_type=F32)
              - jnp.dot(bm_im, cim, precision=hp, preferred_element_type=F32)).astype(BF16)
        for j in range(SSM_CHUNK - k):
            i = j + k
            w_ref[0, j * LANES:(j + 1) * LANES, i * LANES:(i + 1) * LANES] = bd
        jj = SSM_CHUNK - 1 - k
        ws_ref[0, jj * LANES:(jj + 1) * LANES, :ns] = bm_re.astype(BF16)
        ws_ref[0, jj * LANES:(jj + 1) * LANES, ns:] = bm_im.astype(BF16)
    lrc, lic, ldtc = lr_c[0], li_c[0], ldt_c[0]
    for i in range(SSM_CHUNK):
        p_re, p_im = _cpow(lrc, lic, ldtc, float(i + 1))
        wx_ref[0, :ns, i * LANES:(i + 1) * LANES] = (p_re * cre - p_im * cim).astype(BF16)
        wx_ref[0, ns:, i * LANES:(i + 1) * LANES] = (-(p_re * cim + p_im * cre)).astype(BF16)
    q_re, q_im = _cpow(lr, li, ldt, float(SSM_CHUNK))
    a16_ref[0, :, :ns] = q_re
    a16_ref[0, :, ns:] = q_im


def _ssm_wgen(lam_re, lam_im, log_dt, b_re, b_im, c_re, c_im):
    nt, tg, ns = SSM_TILES, SSM_TILE_GROUPS, SSM_TILE_STATE
    eye = jnp.eye(tg, dtype=F32)
    bbd = lambda b: jnp.einsum("cgph,gk->cghkp", b.reshape(nt, tg, SSM_STATE, SSM_GROUP), eye).reshape(nt, LANES, ns)
    cbd = lambda c: jnp.einsum("cghp,gk->cgpkh", c.reshape(nt, tg, SSM_GROUP, SSM_STATE), eye).reshape(nt, ns, LANES)
    ldt = jnp.repeat(log_dt, SSM_STATE)
    rows = [a.reshape(nt, 1, ns) for a in (lam_re, lam_im, ldt)]
    cols = [a.reshape(nt, ns, 1) for a in (lam_re, lam_im, ldt)]
    kc = SSM_CHUNK * LANES
    row_spec = pl.BlockSpec((1, 1, ns), lambda c: (c, 0, 0))
    col_spec = pl.BlockSpec((1, ns, 1), lambda c: (c, 0, 0))
    return pl.pallas_call(
        _ssm_wgen_kernel,
        grid=(nt,),
        in_specs=[row_spec] * 3 + [col_spec] * 3 + [
            pl.BlockSpec((1, LANES, ns), lambda c: (c, 0, 0)),
            pl.BlockSpec((1, LANES, ns), lambda c: (c, 0, 0)),
            pl.BlockSpec((1, ns, LANES), lambda c: (c, 0, 0)),
            pl.BlockSpec((1, ns, LANES), lambda c: (c, 0, 0)),
        ],
        out_specs=[
            pl.BlockSpec((1, kc, kc), lambda c: (c, 0, 0)),
            pl.BlockSpec((1, kc, 2 * ns), lambda c: (c, 0, 0)),
            pl.BlockSpec((1, 2 * ns, kc), lambda c: (c, 0, 0)),
            pl.BlockSpec((1, 1, 2 * ns), lambda c: (c, 0, 0)),
        ],
        out_shape=[
            jax.ShapeDtypeStruct((nt, kc, kc), BF16),
            jax.ShapeDtypeStruct((nt, kc, 2 * ns), BF16),
            jax.ShapeDtypeStruct((nt, 2 * ns, kc), BF16),
            jax.ShapeDtypeStruct((nt, 1, 2 * ns), F32),
        ],
        compiler_params=_params(("parallel",)),
        name="ssm_wgen",
    )(*rows, *cols, bbd(b_re), bbd(b_im), cbd(c_re), cbd(c_im))


def _chunk_rows(u_ref, tb):
    return [u_ref[pl.ds(j, tb, stride=SSM_CHUNK), :] for j in range(SSM_CHUNK)]


def _ssm_sum_kernel(u_ref, ws_ref, s_ref, *, tb):
    u16 = jnp.concatenate(_chunk_rows(u_ref, tb), axis=1).astype(BF16)
    s_ref[0] = _dot(u16, ws_ref[0])


def _ssm_scan_kernel(s_ref, a16_ref, x_ref):
    ns = SSM_TILE_STATE
    a_re = a16_ref[0, :, :ns]
    a_im = a16_ref[0, :, ns:]
    nrow = s_ref.shape[1]

    def body(r, carry):
        xr, xi = carry
        x_ref[0, pl.ds(r, 1), :ns] = xr
        x_ref[0, pl.ds(r, 1), ns:] = xi
        sr = s_ref[0, pl.ds(r, 1), :ns]
        si = s_ref[0, pl.ds(r, 1), ns:]
        return a_re * xr - a_im * xi + sr, a_re * xi + a_im * xr + si

    zero = jnp.zeros((1, ns), F32)
    lax.fori_loop(0, nrow, body, (zero, zero), unroll=8)


def _ssm_out_kernel(u_ref, x_ref, w_ref, wx_ref, d_ref, g_ref, *, tb):
    us = _chunk_rows(u_ref, tb)
    u16 = jnp.concatenate(us, axis=1).astype(BF16)
    y16 = _dot(u16, w_ref[0]) + _dot(x_ref[0].astype(BF16), wx_ref[0])
    d = d_ref[...]
    for i in range(SSM_CHUNK):
        y = y16[:, i * LANES:(i + 1) * LANES] + d * us[i]
        g_ref[pl.ds(i, tb, stride=SSM_CHUNK), :] = _gelu(y)


def _ssm_branch(u, batch, seq, w, ws, wx, a16, d_skip):
    n = u.shape[0]
    nt, ns = SSM_TILES, SSM_TILE_STATE
    kc = SSM_CHUNK * LANES
    nb = n // SSM_CHUNK
    nbs = seq // SSM_CHUNK
    tb = min(nbs, 256)
    tok = tb * SSM_CHUNK
    s = pl.pallas_call(
        functools.partial(_ssm_sum_kernel, tb=tb),
        grid=(nt, nb // tb),
        in_specs=[
            pl.BlockSpec((tok, LANES), lambda c, r: (r, c)),
            pl.BlockSpec((1, kc, 2 * ns), lambda c, r: (c, 0, 0)),
        ],
        out_specs=pl.BlockSpec((1, tb, 2 * ns), lambda c, r: (c, r, 0)),
        out_shape=jax.ShapeDtypeStruct((nt, nb, 2 * ns), F32),
        compiler_params=_params(("parallel", "parallel")),
        name="ssm_chunk_sum",
    )(u, ws)
    x = pl.pallas_call(
        _ssm_scan_kernel,
        grid=(nt, batch),
        in_specs=[
            pl.BlockSpec((1, nbs, 2 * ns), lambda c, b: (c, b, 0)),
            pl.BlockSpec((1, 1, 2 * ns), lambda c, b: (c, 0, 0)),
        ],
        out_specs=pl.BlockSpec((1, nbs, 2 * ns), lambda c, b: (c, b, 0)),
        out_shape=jax.ShapeDtypeStruct((nt, nb, 2 * ns), F32),
        compiler_params=_params(("parallel", "parallel")),
        name="ssm_state_scan",
    )(s, a16)
    return pl.pallas_call(
        functools.partial(_ssm_out_kernel, tb=tb),
        grid=(nt, nb // tb),
        in_specs=[
            pl.BlockSpec((tok, LANES), lambda c, r: (r, c)),
            pl.BlockSpec((1, tb, 2 * ns), lambda c, r: (c, r, 0)),
            pl.BlockSpec((1, kc, kc), lambda c, r: (c, 0, 0)),
            pl.BlockSpec((1, 2 * ns, kc), lambda c, r: (c, 0, 0)),
            pl.BlockSpec((1, LANES), lambda c, r: (0, c)),
        ],
        out_specs=pl.BlockSpec((tok, LANES), lambda c, r: (r, c)),
        out_shape=jax.ShapeDtypeStruct((n, SSM_WIDTH), F32),
        compiler_params=_params(("parallel", "parallel")),
        name="ssm_chunk_out",
    )(u, x, w, wx, d_skip)


def _merge_kernel(attn_ref, g_ref, ga_ref, gs_ref, wglu_ref, wpa_ref, wps_ref, o_ref, ssm_scr):
    @pl.when(pl.program_id(1) == 0)
    def _():
        g = g_ref[...]
        ssm_scr[...] = (g * _sigmoid(_dot(g.astype(BF16), wglu_ref[...]))).astype(BF16)

    pa = _dot(attn_ref[...], wpa_ref[...])
    ps = _dot(ssm_scr[...], wps_ref[...])
    o_ref[...] = (_sigmoid(ga_ref[...].astype(F32)) * pa + _sigmoid(gs_ref[...].astype(F32)) * ps).astype(o_ref.dtype)


def _merge(attn, g, z, w_glu, w_pa, w_ps):
    n = attn.shape[0]
    tm = min(n, 512)
    tn = 1024
    nj = D_MODEL // tn
    return pl.pallas_call(
        _merge_kernel,
        grid=(n // tm, nj),
        in_specs=[
            pl.BlockSpec((tm, N_HEADS * V_HEAD), lambda i, j: (i, 0)),
            pl.BlockSpec((tm, SSM_WIDTH), lambda i, j: (i, 0)),
            pl.BlockSpec((tm, tn), lambda i, j: (i, 1 + j)),
            pl.BlockSpec((tm, tn), lambda i, j: (i, 1 + nj + j)),
            pl.BlockSpec((SSM_WIDTH, SSM_WIDTH), lambda i, j: (0, 0)),
            pl.BlockSpec((N_HEADS * V_HEAD, tn), lambda i, j: (0, j)),
            pl.BlockSpec((SSM_WIDTH, tn), lambda i, j: (0, j)),
        ],
        out_specs=pl.BlockSpec((tm, tn), lambda i, j: (i, j)),
        out_shape=jax.ShapeDtypeStruct((n, D_MODEL), BF16),
        scratch_shapes=[pltpu.VMEM((tm, SSM_WIDTH), BF16)],
        compiler_params=_params(("parallel", "arbitrary")),
        name="glu_merge",
    )(attn, g, z, z, w_glu, w_pa, w_ps)


def _outproj_kernel(m_ref, x_ref, g1_ref, b1_ref, g2_ref, b2_ref, w_ref, h2_ref, h2t_ref):
    y = _dot(m_ref[...], w_ref[...])
    h = _ln(x_ref[...], g1_ref[...], b1_ref[...])
    h2 = _ln(DEEPNORM_ALPHA * h + y, g2_ref[...], b2_ref[...])
    h2_ref[...] = h2.astype(h2_ref.dtype)
    h2t_ref[...] = h2.T.astype(h2t_ref.dtype)


def _outproj(merged, x2, g1, b1, g2, b2, w_out):
    n = merged.shape[0]
    tm = min(n, 512)
    vec = pl.BlockSpec((1, D_MODEL), lambda i: (0, 0))
    return pl.pallas_call(
        _outproj_kernel,
        grid=(n // tm,),
        in_specs=[
            pl.BlockSpec((tm, D_MODEL), lambda i: (i, 0)),
            pl.BlockSpec((tm, D_MODEL), lambda i: (i, 0)),
            vec, vec, vec, vec,
            pl.BlockSpec((D_MODEL, D_MODEL), lambda i: (0, 0)),
        ],
        out_specs=[
            pl.BlockSpec((tm, D_MODEL), lambda i: (i, 0)),
            pl.BlockSpec((D_MODEL, tm), lambda i: (0, i)),
        ],
        out_shape=[
            jax.ShapeDtypeStruct((n, D_MODEL), BF16),
            jax.ShapeDtypeStruct((D_MODEL, n), BF16),
        ],
        compiler_params=_params(("parallel",)),
        name="outproj_ln",
    )(merged, x2, g1, b1, g2, b2, w_out)


def _peer_wcomb_kernel(keys_ref, wpq_ref, o_ref):
    o_ref[...] = lax.dot_general(keys_ref[0].astype(BF16), wpq_ref[...], (((1,), (1,)), ((), ())),
                                 preferred_element_type=F32).astype(o_ref.dtype)


def _peer_wcomb(sub_keys, w_pq):
    nhc = PEER_HEADS * 2
    keys = sub_keys.reshape(nhc, N_KEYS, PEER_KEY_HALF)
    return pl.pallas_call(
        _peer_wcomb_kernel,
        grid=(nhc,),
        in_specs=[
            pl.BlockSpec((1, N_KEYS, PEER_KEY_HALF), lambda i: (i, 0, 0)),
            pl.BlockSpec((D_MODEL, PEER_KEY_HALF), lambda i: (0, i)),
        ],
        out_specs=pl.BlockSpec((N_KEYS, D_MODEL), lambda i: (i, 0)),
        out_shape=jax.ShapeDtypeStruct((nhc * N_KEYS, D_MODEL), BF16),
        compiler_params=_params(("parallel",)),
        name="peer_wcomb",
    )(keys, w_pq)


PEER_TOK = 1024


def _peer_scores_kernel(w_ref, xt_ref, s3_ref, s2_ref):
    hrows = 2 * N_KEYS
    group = 4
    for g in range(PEER_HEADS // group):
        st = _dot(w_ref[g * group * hrows:(g + 1) * group * hrows, :], xt_ref[...])
        for c in range(s3_ref.shape[1]):
            s3_ref[g * group * hrows:(g + 1) * group * hrows, c, :] = st[:, c * LANES:(c + 1) * LANES]
        for hh in range(group):
            s2_ref[g * group + hh] = st[hh * hrows + N_KEYS:(hh + 1) * hrows, :]


def _peer_scores(wcomb, h2t):
    n = h2t.shape[1]
    tl = min(n, PEER_TOK)
    rows = wcomb.shape[0]
    return pl.pallas_call(
        _peer_scores_kernel,
        grid=(n // tl,),
        in_specs=[
            pl.BlockSpec((rows, D_MODEL), lambda i: (0, 0)),
            pl.BlockSpec((D_MODEL, tl), lambda i: (0, i)),
        ],
        out_specs=[
            pl.BlockSpec((rows, tl // LANES, LANES), lambda i: (0, i, 0)),
            pl.BlockSpec((PEER_HEADS, N_KEYS, tl), lambda i: (0, 0, i)),
        ],
        out_shape=[
            jax.ShapeDtypeStruct((rows, n // LANES, LANES), F32),
            jax.ShapeDtypeStruct((PEER_HEADS, N_KEYS, n), F32),
        ],
        compiler_params=_params(("parallel",)),
        name="peer_scores",
    )(wcomb, h2t)


PEER_LIST = PEER_TOPK + 1
PEER_SUB = 4 * N_KEYS


def _insert_sorted(t, v):
    out = []
    for r in range(len(t)):
        out.append(jnp.maximum(t[r], v))
        v = jnp.minimum(t[r], v)
    return out


PEER_ROW = 4


def _peer_thresh_kernel(s_ref, e1_ref, th_ref, cv_ref):
    shape = s_ref.shape[1:]
    neg = jnp.full(shape, -jnp.inf, F32)

    def top_list(base):
        def body(n, t):
            return tuple(_insert_sorted(list(t), s_ref[base + n]))
        return lax.fori_loop(0, N_KEYS, body, (neg,) * PEER_LIST, unroll=4)

    def head(h, carry):
        base1 = h * (2 * N_KEYS)
        base2 = base1 + N_KEYS
        a = top_list(base1)
        b = top_list(base2)
        t = [neg] * PEER_LIST
        for i in range(1, PEER_LIST + 1):
            for j in range(1, PEER_LIST // i + 1):
                t = _insert_sorted(t, a[i - 1] + b[j - 1])
        tau = 0.5 * (t[PEER_TOPK - 1] + t[PEER_TOPK])
        top = a[0] + b[0]
        z = jnp.exp(t[0] - top)
        for r in range(1, PEER_TOPK):
            z = z + jnp.exp(t[r] - top)
        cv_ref[h] = b[0] + jnp.log(2.0 * z)
        a0 = a[0]

        def write(n, c):
            s1 = s_ref[base1 + n]
            e1 = jnp.exp(s1 - a0)
            th = tau - s1
            for gq in range(shape[0] // PEER_ROW):
                e1_ref[h * N_KEYS + n, gq] = e1[gq * PEER_ROW:(gq + 1) * PEER_ROW, :]
                th_ref[h * N_KEYS + n, gq] = th[gq * PEER_ROW:(gq + 1) * PEER_ROW, :]
            return c

        lax.fori_loop(0, N_KEYS, write, 0, unroll=4)
        return carry

    lax.fori_loop(0, PEER_HEADS, head, 0)


def _peer_thresh(s3):
    rows, nl, _ = s3.shape
    n = nl * LANES
    nb = min(PEER_TOK // LANES, nl)
    half = rows // 2
    ng = nb // PEER_ROW
    row_spec = pl.BlockSpec((half, ng, PEER_ROW, LANES), lambda i: (0, i, 0, 0))
    row_shape = jax.ShapeDtypeStruct((half, nl // PEER_ROW, PEER_ROW, LANES), F32)
    e1, th, cv = pl.pallas_call(
        _peer_thresh_kernel,
        grid=(nl // nb,),
        in_specs=[pl.BlockSpec((rows, nb, LANES), lambda i: (0, i, 0))],
        out_specs=[row_spec, row_spec, pl.BlockSpec((PEER_HEADS, nb, LANES), lambda i: (0, i, 0))],
        out_shape=[row_shape, row_shape, jax.ShapeDtypeStruct((PEER_HEADS, nl, LANES), F32)],
        compiler_params=_params(("parallel",)),
        name="peer_thresh",
    )(s3)
    shape5 = (PEER_HEADS, N_KEYS, nl // PEER_ROW, PEER_ROW, LANES)
    return e1.reshape(shape5), th.reshape(shape5), cv.reshape(PEER_HEADS, n)


def _peer_dense_kernel(xt_ref, u_ref, vt_ref, s2_ref, cv_ref, e1_ref, th_ref, h2_ref, g_ref, b_ref,
                       o_ref, acc_scr, a_scr, e2_scr, *, sub):
    e = pl.program_id(1)
    eb = u_ref.shape[0]

    @pl.when(e == 0)
    def _():
        acc_scr[...] = jnp.zeros(acc_scr.shape, F32)
        for h in range(PEER_HEADS):
            e2_scr[h] = jnp.exp(s2_ref[h] - cv_ref[h:h + 1, :])

    def row(ref, h, il):
        return jnp.concatenate([ref[h, il, 0, c:c + 1, :] for c in range(ref.shape[3])], axis=1)

    xt = xt_ref[...]
    nsub = eb // sub
    score = lambda s: _dot(u_ref[s * sub:(s + 1) * sub, :], xt)
    st_next = score(0)
    for s in range(nsub):
        st, st_next = st_next, (score(s + 1) if s + 1 < nsub else None)
        for ii in range(sub // N_KEYS):
            il = s * (sub // N_KEYS) + ii
            gate = None
            for h in range(PEER_HEADS):
                term = row(e1_ref, h, il) * jnp.where(s2_ref[h] >= row(th_ref, h, il), e2_scr[h], 0.0)
                gate = term if gate is None else gate + term
            a = _gelu_x2(st[ii * N_KEYS:(ii + 1) * N_KEYS, :]) * gate
            a_scr[s * sub + ii * N_KEYS:s * sub + (ii + 1) * N_KEYS, :] = a.astype(BF16)
        acc_scr[...] += _dot(vt_ref[:, s * sub:(s + 1) * sub], a_scr[s * sub:(s + 1) * sub, :])

    @pl.when(e == pl.num_programs(1) - 1)
    def _():
        y = acc_scr[...].T
        o_ref[...] = _ln(DEEPNORM_ALPHA * h2_ref[...].astype(F32) + y, g_ref[...], b_ref[...])


def _peer_dense(h2, h2t, s2, cv, e1, th, u_tab, vt_tab, ln_g, ln_b):
    n = h2.shape[0]
    t = min(n, PEER_ROW * LANES)
    eb = 8 * N_KEYS
    sub = PEER_SUB
    vec = pl.BlockSpec((1, D_MODEL), lambda i, e: (0, 0))
    row_spec = pl.BlockSpec((PEER_HEADS, eb // N_KEYS, 1, PEER_ROW, LANES), lambda i, e: (0, e, i, 0, 0))
    return pl.pallas_call(
        functools.partial(_peer_dense_kernel, sub=sub),
        grid=(n // t, N_EXPERTS // eb),
        in_specs=[
            pl.BlockSpec((D_MODEL, t), lambda i, e: (0, i)),
            pl.BlockSpec((eb, D_MODEL), lambda i, e: (e, 0)),
            pl.BlockSpec((D_MODEL, eb), lambda i, e: (0, e)),
            pl.BlockSpec((PEER_HEADS, N_KEYS, t), lambda i, e: (0, 0, i)),
            pl.BlockSpec((PEER_HEADS, t), lambda i, e: (0, i)),
            row_spec, row_spec,
            pl.BlockSpec((t, D_MODEL), lambda i, e: (i, 0)),
            vec, vec,
        ],
        out_specs=pl.BlockSpec((t, D_MODEL), lambda i, e: (i, 0)),
        out_shape=jax.ShapeDtypeStruct((n, D_MODEL), F32),
        scratch_shapes=[
            pltpu.VMEM((D_MODEL, t), F32),
            pltpu.VMEM((eb, t), BF16),
            pltpu.VMEM((PEER_HEADS, N_KEYS, t), F32),
        ],
        compiler_params=_params(("parallel", "arbitrary")),
        name="peer_dense",
    )(h2t, u_tab, vt_tab, s2, cv, e1, th, h2, ln_g, ln_b)


def _rot_half_cols(w):
    half = QK_ROPE // 2
    return jnp.concatenate([-w[..., half:], w[..., :half]], axis=-1)


def kernel(x, ln_in_g, ln_in_b, w_in, q_norm_g, w_uq, kv_norm_g, w_ukv, ssm_lam_re, ssm_lam_im, ssm_log_dt,
           ssm_b_re, ssm_b_im, ssm_c_re, ssm_c_im, ssm_d, w_glu, w_proj_attn, w_proj_ssm, w_out, ln_mix_g, ln_mix_b,
           w_peer_q, peer_sub_keys, peer_u, peer_v, ln_ffn_g, ln_ffn_b):
    batch, seq, _ = x.shape
    n = batch * seq
    x2 = x.reshape(n, D_MODEL)
    row = lambda v: v.reshape(1, -1).astype(F32)
    l = 0

    wi = w_in[l]
    w_main = jnp.concatenate([wi[:, OFF_SSM:OFF_KR], wi[:, OFF_GA:IN_WIDTH]], axis=1).astype(BF16)
    w_krc = wi[:, OFF_KR:OFF_GA]
    w_kr = jnp.concatenate([w_krc, _rot_half_cols(w_krc)], axis=1).astype(BF16)
    wq = w_uq[l].reshape(Q_LORA, N_HEADS, QK_HEAD)
    wq_ext = jnp.concatenate([wq, _rot_half_cols(wq[..., QK_NOPE:])], axis=-1)
    wq_ext = wq_ext.transpose(1, 0, 2).astype(BF16)
    wkv = w_ukv[l].reshape(KV_LORA, N_HEADS, QK_NOPE + V_HEAD).transpose(1, 0, 2).astype(BF16)

    cs = _rope_table(seq)
    u, z, kr = _inproj(x2, row(ln_in_g), row(ln_in_b), w_main, w_kr)

    q, k, v = _qkv(z, kr, cs, row(q_norm_g[l]), row(kv_norm_g[l]), wq_ext, wkv, batch, seq)
    attn = _attention(q, k, v).reshape(n, N_HEADS * V_HEAD)

    w, ws, wx, a16 = _ssm_wgen(ssm_lam_re[l], ssm_lam_im[l], ssm_log_dt[l], ssm_b_re[l], ssm_b_im[l],
                               ssm_c_re[l], ssm_c_im[l])
    g = _ssm_branch(u, batch, seq, w, ws, wx, a16, row(ssm_d[l]))

    merged = _merge(attn, g, z, w_glu[l].astype(BF16), w_proj_attn[l].astype(BF16), w_proj_ssm[l].astype(BF16))
    h2, h2t = _outproj(merged, x2, row(ln_in_g), row(ln_in_b), row(ln_mix_g[l]), row(ln_mix_b[l]),
                       w_out[l].astype(BF16))

    wcomb = _peer_wcomb(peer_sub_keys[l], w_peer_q[l].astype(BF16))
    s3, s2 = _peer_scores(wcomb, h2t)
    e1, th, cv = _peer_thresh(s3)
    out = _peer_dense(h2, h2t, s2, cv, e1, th, peer_u[l].astype(BF16), peer_v[l].T.astype(BF16),
                      row(ln_ffn_g[l]), row(ln_ffn_b[l]))
    return out.reshape(batch, seq, D_MODEL)
```

```python
import functools
import math

import jax
import jax.numpy as jnp
from jax import lax
from jax.experimental import pallas as pl
from jax.experimental.pallas import tpu as pltpu

F32 = jnp.float32
BF16 = jnp.bfloat16

D_MODEL = 2048
DEPTH = 1
CHUNK = 64
SSM_WIDTH = 1024
SSM_GROUP = 16
SSM_GROUPS = SSM_WIDTH // SSM_GROUP
SSM_STATE = 64
N_HEADS = 16
Q_LORA = 512
KV_LORA = 512
QK_NOPE = 128
QK_ROPE = 64
V_HEAD = 128
QK_HEAD = QK_NOPE + QK_ROPE
ROPE_THETA = 10000.0
PEER_HEADS = 8
N_KEYS = 128
N_EXPERTS = N_KEYS * N_KEYS
PEER_KEY_HALF = 128
PEER_TOPK = 16
LN_EPS = 1e-5
RMS_EPS = 1e-6
DEEPNORM_ALPHA = (2.0 * DEPTH) ** 0.25
OFF_SSM = 0
OFF_Q = OFF_SSM + SSM_WIDTH
OFF_KV = OFF_Q + Q_LORA
OFF_KR = OFF_KV + KV_LORA
OFF_GA = OFF_KR + QK_ROPE
OFF_GS = OFF_GA + D_MODEL
IN_WIDTH = OFF_GS + D_MODEL

LANES = 128
VMEM_LIMIT_BYTES = 56 * 1024 * 1024

SSM_CHUNK = 16
SSM_TILE_GROUPS = LANES // SSM_GROUP
SSM_TILES = SSM_WIDTH // LANES
SSM_TILE_STATE = SSM_TILE_GROUPS * SSM_STATE
NEG_BIG = -1e30


def _params(semantics):
    return pltpu.CompilerParams(dimension_semantics=semantics, vmem_limit_bytes=VMEM_LIMIT_BYTES)


def _dot(a, b):
    return jnp.dot(a, b, preferred_element_type=F32)


def _ln(x, g, b):
    mu = jnp.mean(x, axis=-1, keepdims=True)
    xc = x - mu
    var = jnp.mean(xc * xc, axis=-1, keepdims=True)
    return xc * lax.rsqrt(var + LN_EPS) * g + b


GELU_C = 0.7978845608028654
GELU_A = 0.044715


def _gelu(x):
    return 0.5 * x * (1.0 + jnp.tanh(GELU_C * (x + GELU_A * (x * x * x))))


def _gelu_x2(x):
    return x * (1.0 + jnp.tanh(x * (GELU_C + (GELU_C * GELU_A) * (x * x))))


def _sigmoid(x):
    return 1.0 / (1.0 + jnp.exp(-x))


def _rope_table_kernel(invf_ref, o_ref):
    ts = o_ref.shape[0]
    pos = (lax.broadcasted_iota(jnp.int32, (ts, LANES), 0) + pl.program_id(0) * ts).astype(F32)
    ang = pos * invf_ref[...]
    lane = lax.broadcasted_iota(jnp.int32, (ts, LANES), 1)
    o_ref[...] = jnp.where(lane < 2 * (QK_ROPE // 2), jnp.cos(ang), jnp.sin(ang))


def _rope_table(seq):
    half = QK_ROPE // 2
    invf = 1.0 / (ROPE_THETA ** (jnp.arange(half, dtype=F32) / half))
    invf4 = jnp.tile(invf, 4)[None, :]
    ts = min(seq, 1024)
    return pl.pallas_call(
        _rope_table_kernel,
        grid=(seq // ts,),
        in_specs=[pl.BlockSpec((1, LANES), lambda i: (0, 0))],
        out_specs=pl.BlockSpec((ts, LANES), lambda i: (i, 0)),
        out_shape=jax.ShapeDtypeStruct((seq, LANES), F32),
        compiler_params=_params(("parallel",)),
        name="rope_table",
    )(invf4)


def _inproj_kernel(x_ref, g_ref, b_ref, w_ref, wkr_ref, u_ref, z_ref, kr_ref, h_scr):
    j = pl.program_id(1)

    @pl.when(j == 0)
    def _():
        h = _ln(x_ref[...], g_ref[...], b_ref[...]).astype(BF16)
        h_scr[...] = h
        u_ref[...] = _dot(h, w_ref[...])
        kr_ref[...] = _dot(h, wkr_ref[...]).astype(kr_ref.dtype)

    @pl.when(j > 0)
    def _():
        z_ref[...] = _dot(h_scr[...], w_ref[...]).astype(z_ref.dtype)


def _inproj(x2, ln_g, ln_b, w_main, w_kr):
    n = x2.shape[0]
    tm = min(n, 1024)
    tn = 1024
    nj = w_main.shape[1] // tn
    return pl.pallas_call(
        _inproj_kernel,
        grid=(n // tm, nj),
        in_specs=[
            pl.BlockSpec((tm, D_MODEL), lambda i, j: (i, 0)),
            pl.BlockSpec((1, D_MODEL), lambda i, j: (0, 0)),
            pl.BlockSpec((1, D_MODEL), lambda i, j: (0, 0)),
            pl.BlockSpec((D_MODEL, tn), lambda i, j: (0, j)),
            pl.BlockSpec((D_MODEL, LANES), lambda i, j: (0, 0)),
        ],
        out_specs=[
            pl.BlockSpec((tm, tn), lambda i, j: (i, 0)),
            pl.BlockSpec((tm, tn), lambda i, j: (i, jnp.maximum(j - 1, 0))),
            pl.BlockSpec((tm, LANES), lambda i, j: (i, 0)),
        ],
        out_shape=[
            jax.ShapeDtypeStruct((n, tn), F32),
            jax.ShapeDtypeStruct((n, (nj - 1) * tn), BF16),
            jax.ShapeDtypeStruct((n, LANES), BF16),
        ],
        scratch_shapes=[pltpu.VMEM((tm, D_MODEL), BF16)],
        compiler_params=_params(("parallel", "arbitrary")),
        name="ln_inproj",
    )(x2, ln_g, ln_b, w_main, w_kr)


def _rope_mix(t2, cs):
    t = t2 * cs
    return t + pltpu.roll(t, QK_ROPE, 1)


def _qkv_kernel(cq_ref, ckv_ref, kr_ref, cs_ref, gq_ref, gkv_ref, wq_ref, wkv_ref,
                q_ref, k_ref, v_ref, cqn_scr, ckvn_scr, kro_scr):
    cs = cs_ref[...]

    @pl.when(pl.program_id(1) == 0)
    def _():
        def rms(c, g):
            c = c.astype(F32)
            return (c * lax.rsqrt(jnp.mean(c * c, axis=-1, keepdims=True) + RMS_EPS) * g).astype(BF16)

        cqn_scr[...] = rms(cq_ref[...], gq_ref[...])
        ckvn_scr[...] = rms(ckv_ref[...], gkv_ref[...])
        kro_scr[...] = _rope_mix(kr_ref[...].astype(F32), cs).astype(BF16)

    scale = math.log2(math.e) / math.sqrt(QK_HEAD)
    qe = _dot(cqn_scr[...], wq_ref[0])
    qr = _rope_mix(qe[:, QK_NOPE:], cs)
    q_ref[0, 0, :, :QK_NOPE] = (qe[:, :QK_NOPE] * scale).astype(BF16)
    q_ref[0, 0, :, QK_NOPE:] = (qr[:, :QK_ROPE] * scale).astype(BF16)
    kv = _dot(ckvn_scr[...], wkv_ref[0])
    k_ref[0, 0, :, :QK_NOPE] = kv[:, :QK_NOPE].astype(BF16)
    k_ref[0, 0, :, QK_NOPE:] = kro_scr[:, :QK_ROPE]
    v_ref[0, 0] = kv[:, QK_NOPE:].astype(BF16)


def _qkv(z, kr, cs, gq, gkv, wq_ext, wkv, batch, seq):
    n = z.shape[0]
    tm = min(seq, 2048)
    nsb = seq // tm
    return pl.pallas_call(
        _qkv_kernel,
        grid=(n // tm, N_HEADS),
        in_specs=[
            pl.BlockSpec((tm, Q_LORA), lambda i, h: (i, 0)),
            pl.BlockSpec((tm, KV_LORA), lambda i, h: (i, 1)),
            pl.BlockSpec((tm, LANES), lambda i, h: (i, 0)),
            pl.BlockSpec((tm, LANES), lambda i, h: (i % nsb, 0)),
            pl.BlockSpec((1, Q_LORA), lambda i, h: (0, 0)),
            pl.BlockSpec((1, KV_LORA), lambda i, h: (0, 0)),
            pl.BlockSpec((1, Q_LORA, 2 * LANES), lambda i, h: (h, 0, 0)),
            pl.BlockSpec((1, KV_LORA, 2 * LANES), lambda i, h: (h, 0, 0)),
        ],
        out_specs=[
            pl.BlockSpec((1, 1, tm, QK_HEAD), lambda i, h: (i // nsb, h, i % nsb, 0)),
            pl.BlockSpec((1, 1, tm, QK_HEAD), lambda i, h: (i // nsb, h, i % nsb, 0)),
            pl.BlockSpec((1, 1, tm, V_HEAD), lambda i, h: (i // nsb, h, i % nsb, 0)),
        ],
        out_shape=[
            jax.ShapeDtypeStruct((batch, N_HEADS, seq, QK_HEAD), BF16),
            jax.ShapeDtypeStruct((batch, N_HEADS, seq, QK_HEAD), BF16),
            jax.ShapeDtypeStruct((batch, N_HEADS, seq, V_HEAD), BF16),
        ],
        scratch_shapes=[
            pltpu.VMEM((tm, Q_LORA), BF16),
            pltpu.VMEM((tm, KV_LORA), BF16),
            pltpu.VMEM((tm, LANES), BF16),
        ],
        compiler_params=_params(("parallel", "arbitrary")),
        name="qkv_rope",
    )(z, z, kr, cs, gq, gkv, wq_ext, wkv)


ATT_ROWS = 512
ATT_CHAINS = 8
ATT_BK_MAIN = 1024


def _attn_kernel(q_ref, k_ref, v_ref, o_ref, m_scr, acc_scr, *, rows, chains, bkm):
    qi = pl.program_id(2)
    m_scr[...] = jnp.full(m_scr.shape, -jnp.inf, F32)
    acc_scr[...] = jnp.zeros(acc_scr.shape, F32)

    def load_kv(j, bk):
        start = pl.multiple_of(j * bk, bk)
        kj = k_ref[0, 0, pl.ds(start, bk), :]
        vj = jnp.concatenate([v_ref[0, 0, pl.ds(start, bk), :], jnp.ones((bk, LANES), BF16)], axis=1)
        return kj, vj

    def block(c, kj, vj, masked):
        q = q_ref[0, 0, c * rows:(c + 1) * rows, :]
        s = lax.dot_general(q, kj, (((1,), (1,)), ((), ())), preferred_element_type=F32)
        if masked:
            rc = lax.broadcasted_iota(jnp.int32, s.shape, 0) // CHUNK
            cc = lax.broadcasted_iota(jnp.int32, s.shape, 1) // CHUNK
            s = jnp.where(cc <= rc, s, NEG_BIG)
        m_prev = m_scr[c]
        m_new = jnp.maximum(m_prev, jnp.max(s, axis=-1, keepdims=True))
        alpha = jnp.exp2(m_prev - m_new)
        p = jnp.exp2(s - jnp.tile(m_new, (1, s.shape[1] // LANES)))
        acc_scr[c] = jnp.tile(alpha, (1, 2)) * acc_scr[c] + _dot(p.astype(BF16), vj)
        m_scr[c] = m_new

    def body(j, carry):
        kj, vj = load_kv(j, bkm)
        for c in range(chains):
            block(c, kj, vj, False)
        return carry

    lax.fori_loop(0, qi * (chains * rows // bkm), body, 0)
    for jj in range(chains):
        kj, vj = load_kv(qi * chains + jj, rows)
        for c in range(jj, chains):
            block(c, kj, vj, c == jj)
    for c in range(chains):
        a = acc_scr[c]
        o_ref[0, c * rows:(c + 1) * rows, :] = (a[:, :V_HEAD] / a[:, V_HEAD:]).astype(o_ref.dtype)


def _attention(q, k, v):
    batch, _, seq, _ = q.shape
    rows = min(seq, ATT_ROWS)
    chains = min(ATT_CHAINS, seq // rows)
    bq = rows * chains
    bkm = min(ATT_BK_MAIN, bq)
    return pl.pallas_call(
        functools.partial(_attn_kernel, rows=rows, chains=chains, bkm=bkm),
        grid=(batch, N_HEADS, seq // bq),
        in_specs=[
            pl.BlockSpec((1, 1, bq, QK_HEAD), lambda b, h, i: (b, h, i, 0)),
            pl.BlockSpec((1, 1, seq, QK_HEAD), lambda b, h, i: (b, h, 0, 0)),
            pl.BlockSpec((1, 1, seq, V_HEAD), lambda b, h, i: (b, h, 0, 0)),
        ],
        out_specs=pl.BlockSpec((1, bq, V_HEAD), lambda b, h, i: (b, i, h)),
        out_shape=jax.ShapeDtypeStruct((batch, seq, N_HEADS * V_HEAD), BF16),
        scratch_shapes=[
            pltpu.VMEM((chains, rows, LANES), F32),
            pltpu.VMEM((chains, rows, 2 * V_HEAD), F32),
        ],
        compiler_params=_params(("parallel", "parallel", "arbitrary")),
        name="flash_attn",
    )(q, k, v)


def _cpow(lr, li, ldt, k):
    dt = jnp.exp(ldt)
    mag = jnp.exp(lr * dt * k)
    ang = li * dt * k
    return mag * jnp.cos(ang), mag * jnp.sin(ang)


def _ssm_wgen_kernel(lr_r, li_r, ldt_r, lr_c, li_c, ldt_c, bre_ref, bim_ref, cre_ref, cim_ref,
                     w_ref, ws_ref, wx_ref, a16_ref):
    hp = lax.Precision.HIGHEST
    ns = SSM_TILE_STATE
    lr, li, ldt = lr_r[0], li_r[0], ldt_r[0]
    a_re, a_im = _cpow(lr, li, ldt, 1.0)
    den = lr * lr + li * li
    nr = a_re - 1.0
    ni = a_im
    cf_re = (nr * lr + ni * li) / den
    cf_im = (ni * lr - nr * li) / den
    bre, bim = bre_ref[0], bim_ref[0]
    cre, cim = cre_ref[0], cim_ref[0]
    w_ref[...] = jnp.zeros(w_ref.shape, w_ref.dtype)
    for k in range(SSM_CHUNK):
        p_re, p_im = _cpow(lr, li, ldt, float(k))
        m_re = cf_re * p_re - cf_im * p_im
        m_im = cf_re * p_im + cf_im * p_re
        bm_re = bre * m_re - bim * m_im
        bm_im = bre * m_im + bim * m_re
        bd = (jnp.dot(bm_re, cre, precision=hp, preferred_element_type=F32)
              - jnp.dot(bm_im, cim, precision=hp, preferred_element_type=F32)).astype(BF16)
        for j in range(SSM_CHUNK - k):
            i = j + k
            w_ref[0, j * LANES:(j + 1) * LANES, i * LANES:(i + 1) * LANES] = bd
        jj = SSM_CHUNK - 1 - k
        ws_ref[0, jj * LANES:(jj + 1) * LANES, :ns] = bm_re.astype(BF16)
        ws_ref[0, jj * LANES:(jj + 1) * LANES, ns:] = bm_im.astype(BF16)
    lrc, lic, ldtc = lr_c[0], li_c[0], ldt_c[0]
    for i in range(SSM_CHUNK):
        p_re, p_im = _cpow(lrc, lic, ldtc, float(i + 1))
        wx_ref[0, :ns, i * LANES:(i + 1) * LANES] = (p_re * cre - p_im * cim).astype(BF16)
        wx_ref[0, ns:, i * LANES:(i + 1) * LANES] = (-(p_re * cim + p_im * cre)).astype(BF16)
    q_re, q_im = _cpow(lr, li, ldt, float(SSM_CHUNK))
    a16_ref[0, :, :ns] = q_re
    a16_ref[0, :, ns:] = q_im


def _ssm_wgen(lam_re, lam_im, log_dt, b_re, b_im, c_re, c_im):
    nt, tg, ns = SSM_TILES, SSM_TILE_GROUPS, SSM_TILE_STATE
    eye = jnp.eye(tg, dtype=F32)
    bbd = lambda b: jnp.einsum("cgph,gk->cghkp", b.reshape(nt, tg, SSM_STATE, SSM_GROUP), eye).reshape(nt, LANES, ns)
    cbd = lambda c: jnp.einsum("cghp,gk->cgpkh", c.reshape(nt, tg, SSM_GROUP, SSM_STATE), eye).reshape(nt, ns, LANES)
    ldt = jnp.repeat(log_dt, SSM_STATE)
    rows = [a.reshape(nt, 1, ns) for a in (lam_re, lam_im, ldt)]
    cols = [a.reshape(nt, ns, 1) for a in (lam_re, lam_im, ldt)]
    kc = SSM_CHUNK * LANES
    row_spec = pl.BlockSpec((1, 1, ns), lambda c: (c, 0, 0))
    col_spec = pl.BlockSpec((1, ns, 1), lambda c: (c, 0, 0))
    return pl.pallas_call(
        _ssm_wgen_kernel,
        grid=(nt,),
        in_specs=[row_spec] * 3 + [col_spec] * 3 + [
            pl.BlockSpec((1, LANES, ns), lambda c: (c, 0, 0)),
            pl.BlockSpec((1, LANES, ns), lambda c: (c, 0, 0)),
            pl.BlockSpec((1, ns, LANES), lambda c: (c, 0, 0)),
            pl.BlockSpec((1, ns, LANES), lambda c: (c, 0, 0)),
        ],
        out_specs=[
            pl.BlockSpec((1, kc, kc), lambda c: (c, 0, 0)),
            pl.BlockSpec((1, kc, 2 * ns), lambda c: (c, 0, 0)),
            pl.BlockSpec((1, 2 * ns, kc), lambda c: (c, 0, 0)),
            pl.BlockSpec((1, 1, 2 * ns), lambda c: (c, 0, 0)),
        ],
        out_shape=[
            jax.ShapeDtypeStruct((nt, kc, kc), BF16),
            jax.ShapeDtypeStruct((nt, kc, 2 * ns), BF16),
            jax.ShapeDtypeStruct((nt, 2 * ns, kc), BF16),
            jax.ShapeDtypeStruct((nt, 1, 2 * ns), F32),
        ],
        compiler_params=_params(("parallel",)),
        name="ssm_wgen",
    )(*rows, *cols, bbd(b_re), bbd(b_im), cbd(c_re), cbd(c_im))


def _chunk_rows(u_ref, tb):
    return [u_ref[pl.ds(j, tb, stride=SSM_CHUNK), :] for j in range(SSM_CHUNK)]


def _ssm_sum_kernel(u_ref, ws_ref, s_ref, *, tb):
    u16 = jnp.concatenate(_chunk_rows(u_ref, tb), axis=1).astype(BF16)
    s_ref[0] = _dot(u16, ws_ref[0])


def _ssm_scan_kernel(s_ref, a16_ref, x_ref):
    ns = SSM_TILE_STATE
    a_re = a16_ref[0, :, :ns]
    a_im = a16_ref[0, :, ns:]
    nrow = s_ref.shape[1]

    def body(r, carry):
        xr, xi = carry
        x_ref[0, pl.ds(r, 1), :ns] = xr
        x_ref[0, pl.ds(r, 1), ns:] = xi
        sr = s_ref[0, pl.ds(r, 1), :ns]
        si = s_ref[0, pl.ds(r, 1), ns:]
        return a_re * xr - a_im * xi + sr, a_re * xi + a_im * xr + si

    zero = jnp.zeros((1, ns), F32)
    lax.fori_loop(0, nrow, body, (zero, zero), unroll=8)


def _ssm_out_kernel(u_ref, x_ref, w_ref, wx_ref, d_ref, g_ref, *, tb):
    us = _chunk_rows(u_ref, tb)
    u16 = jnp.concatenate(us, axis=1).astype(BF16)
    y16 = _dot(u16, w_ref[0]) + _dot(x_ref[0].astype(BF16), wx_ref[0])
    d = d_ref[...]
    for i in range(SSM_CHUNK):
        y = y16[:, i * LANES:(i + 1) * LANES] + d * us[i]
        g_ref[pl.ds(i, tb, stride=SSM_CHUNK), :] = _gelu(y)


def _ssm_branch(u, batch, seq, w, ws, wx, a16, d_skip):
    n = u.shape[0]
    nt, ns = SSM_TILES, SSM_TILE_STATE
    kc = SSM_CHUNK * LANES
    nb = n // SSM_CHUNK
    nbs = seq // SSM_CHUNK
    tb = min(nbs, 256)
    tok = tb * SSM_CHUNK
    s = pl.pallas_call(
        functools.partial(_ssm_sum_kernel, tb=tb),
        grid=(nt, nb // tb),
        in_specs=[
            pl.BlockSpec((tok, LANES), lambda c, r: (r, c)),
            pl.BlockSpec((1, kc, 2 * ns), lambda c, r: (c, 0, 0)),
        ],
        out_specs=pl.BlockSpec((1, tb, 2 * ns), lambda c, r: (c, r, 0)),
        out_shape=jax.ShapeDtypeStruct((nt, nb, 2 * ns), F32),
        compiler_params=_params(("parallel", "parallel")),
        name="ssm_chunk_sum",
    )(u, ws)
    x = pl.pallas_call(
        _ssm_scan_kernel,
        grid=(nt, batch),
        in_specs=[
            pl.BlockSpec((1, nbs, 2 * ns), lambda c, b: (c, b, 0)),
            pl.BlockSpec((1, 1, 2 * ns), lambda c, b: (c, 0, 0)),
        ],
        out_specs=pl.BlockSpec((1, nbs, 2 * ns), lambda c, b: (c, b, 0)),
        out_shape=jax.ShapeDtypeStruct((nt, nb, 2 * ns), F32),
        compiler_params=_params(("parallel", "parallel")),
        name="ssm_state_scan",
    )(s, a16)
    return pl.pallas_call(
        functools.partial(_ssm_out_kernel, tb=tb),
        grid=(nt, nb // tb),
        in_specs=[
            pl.BlockSpec((tok, LANES), lambda c, r: (r, c)),
            pl.BlockSpec((1, tb, 2 * ns), lambda c, r: (c, r, 0)),
            pl.BlockSpec((1, kc, kc), lambda c, r: (c, 0, 0)),
            pl.BlockSpec((1, 2 * ns, kc), lambda c, r: (c, 0, 0)),
            pl.BlockSpec((1, LANES), lambda c, r: (0, c)),
        ],
        out_specs=pl.BlockSpec((tok, LANES), lambda c, r: (r, c)),
        out_shape=jax.ShapeDtypeStruct((n, SSM_WIDTH), F32),
        compiler_params=_params(("parallel", "parallel")),
        name="ssm_chunk_out",
    )(u, x, w, wx, d_skip)


def _merge_kernel(attn_ref, g_ref, ga_ref, gs_ref, wglu_ref, wpa_ref, wps_ref, o_ref, ssm_scr):
    @pl.when(pl.program_id(1) == 0)
    def _():
        g = g_ref[...]
        ssm_scr[...] = (g * _sigmoid(_dot(g.astype(BF16), wglu_ref[...]))).astype(BF16)

    pa = _dot(attn_ref[...], wpa_ref[...])
    ps = _dot(ssm_scr[...], wps_ref[...])
    o_ref[...] = (_sigmoid(ga_ref[...].astype(F32)) * pa + _sigmoid(gs_ref[...].astype(F32)) * ps).astype(o_ref.dtype)


def _merge(attn, g, z, w_glu, w_pa, w_ps):
    n = attn.shape[0]
    tm = min(n, 512)
    tn = 1024
    nj = D_MODEL // tn
    return pl.pallas_call(
        _merge_kernel,
        grid=(n // tm, nj),
        in_specs=[
            pl.BlockSpec((tm, N_HEADS * V_HEAD), lambda i, j: (i, 0)),
            pl.BlockSpec((tm, SSM_WIDTH), lambda i, j: (i, 0)),
            pl.BlockSpec((tm, tn), lambda i, j: (i, 1 + j)),
            pl.BlockSpec((tm, tn), lambda i, j: (i, 1 + nj + j)),
            pl.BlockSpec((SSM_WIDTH, SSM_WIDTH), lambda i, j: (0, 0)),
            pl.BlockSpec((N_HEADS * V_HEAD, tn), lambda i, j: (0, j)),
            pl.BlockSpec((SSM_WIDTH, tn), lambda i, j: (0, j)),
        ],
        out_specs=pl.BlockSpec((tm, tn), lambda i, j: (i, j)),
        out_shape=jax.ShapeDtypeStruct((n, D_MODEL), BF16),
        scratch_shapes=[pltpu.VMEM((tm, SSM_WIDTH), BF16)],
        compiler_params=_params(("parallel", "arbitrary")),
        name="glu_merge",
    )(attn, g, z, z, w_glu, w_pa, w_ps)


def _outproj_kernel(m_ref, x_ref, g1_ref, b1_ref, g2_ref, b2_ref, w_ref, h2_ref, h2t_ref):
    y = _dot(m_ref[...], w_ref[...])
    h = _ln(x_ref[...], g1_ref[...], b1_ref[...])
    h2 = _ln(DEEPNORM_ALPHA * h + y, g2_ref[...], b2_ref[...])
    h2_ref[...] = h2.astype(h2_ref.dtype)
    h2t_ref[...] = h2.T.astype(h2t_ref.dtype)


def _outproj(merged, x2, g1, b1, g2, b2, w_out):
    n = merged.shape[0]
    tm = min(n, 512)
    vec = pl.BlockSpec((1, D_MODEL), lambda i: (0, 0))
    return pl.pallas_call(
        _outproj_kernel,
        grid=(n // tm,),
        in_specs=[
            pl.BlockSpec((tm, D_MODEL), lambda i: (i, 0)),
            pl.BlockSpec((tm, D_MODEL), lambda i: (i, 0)),
            vec, vec, vec, vec,
            pl.BlockSpec((D_MODEL, D_MODEL), lambda i: (0, 0)),
        ],
        out_specs=[
            pl.BlockSpec((tm, D_MODEL), lambda i: (i, 0)),
            pl.BlockSpec((D_MODEL, tm), lambda i: (0, i)),
        ],
        out_shape=[
            jax.ShapeDtypeStruct((n, D_MODEL), BF16),
            jax.ShapeDtypeStruct((D_MODEL, n), BF16),
        ],
        compiler_params=_params(("parallel",)),
        name="outproj_ln",
    )(merged, x2, g1, b1, g2, b2, w_out)


def _peer_wcomb_kernel(keys_ref, wpq_ref, o_ref):
    o_ref[...] = lax.dot_general(keys_ref[0].astype(BF16), wpq_ref[...], (((1,), (1,)), ((), ())),
                                 preferred_element_type=F32).astype(o_ref.dtype)


def _peer_wcomb(sub_keys, w_pq):
    nhc = PEER_HEADS * 2
    keys = sub_keys.reshape(nhc, N_KEYS, PEER_KEY_HALF)
    return pl.pallas_call(
        _peer_wcomb_kernel,
        grid=(nhc,),
        in_specs=[
            pl.BlockSpec((1, N_KEYS, PEER_KEY_HALF), lambda i: (i, 0, 0)),
            pl.BlockSpec((D_MODEL, PEER_KEY_HALF), lambda i: (0, i)),
        ],
        out_specs=pl.BlockSpec((N_KEYS, D_MODEL), lambda i: (i, 0)),
        out_shape=jax.ShapeDtypeStruct((nhc * N_KEYS, D_MODEL), BF16),
        compiler_params=_params(("parallel",)),
        name="peer_wcomb",
    )(keys, w_pq)


PEER_TOK = 1024


def _peer_scores_kernel(w_ref, xt_ref, s3_ref):
    rows = w_ref.shape[0] // 2
    for g in range(2):
        st = _dot(w_ref[g * rows:(g + 1) * rows, :], xt_ref[...])
        for c in range(s3_ref.shape[1]):
            s3_ref[g * rows:(g + 1) * rows, c, :] = st[:, c * LANES:(c + 1) * LANES]


def _peer_scores(wcomb, h2t):
    n = h2t.shape[1]
    tl = min(n, PEER_TOK)
    rows = wcomb.shape[0]
    return pl.pallas_call(
        _peer_scores_kernel,
        grid=(n // tl,),
        in_specs=[
            pl.BlockSpec((rows, D_MODEL), lambda i: (0, 0)),
            pl.BlockSpec((D_MODEL, tl), lambda i: (0, i)),
        ],
        out_specs=pl.BlockSpec((rows, tl // LANES, LANES), lambda i: (0, i, 0)),
        out_shape=jax.ShapeDtypeStruct((rows, n // LANES, LANES), F32),
        compiler_params=_params(("parallel",)),
        name="peer_scores",
    )(wcomb, h2t)


PEER_LIST = PEER_TOPK + 1
PEER_SUB = 4 * N_KEYS


def _insert_sorted(t, v):
    out = []
    for r in range(len(t)):
        out.append(jnp.maximum(t[r], v))
        v = jnp.minimum(t[r], v)
    return out


PEER_ROW = 4


def _count_above(vals, x, strict):
    above = (lambda v: v > x) if strict else (lambda v: v >= x)
    pick = lambda m, hi, lo: jnp.where(m, hi, lo)
    m8 = above(vals[7])
    m4 = above(pick(m8, vals[11], vals[3]))
    m2 = above(pick(m8, pick(m4, vals[13], vals[9]), pick(m4, vals[5], vals[1])))
    hi = pick(m4, pick(m2, vals[14], vals[12]), pick(m2, vals[10], vals[8]))
    lo = pick(m4, pick(m2, vals[6], vals[4]), pick(m2, vals[2], vals[0]))
    m1 = above(pick(m8, hi, lo))
    cnt = (pick(m8, 8.0, 0.0) + pick(m4, 4.0, 0.0)) + (pick(m2, 2.0, 0.0) + pick(m1, 1.0, 0.0))
    return pick(above(vals[15]), 16.0, cnt)


def _peer_thresh_kernel(s_ref, c_ref, e1_ref, q_ref, e2_ref):
    shape = s_ref.shape[1:]
    neg = jnp.full(shape, -jnp.inf, F32)

    def top_list(base):
        def body(n, t):
            return tuple(_insert_sorted(list(t), s_ref[base + n]))
        return lax.fori_loop(0, N_KEYS, body, (neg,) * PEER_LIST, unroll=4)

    a = top_list(0)
    b = top_list(N_KEYS)
    t = [neg] * PEER_LIST
    for i in range(1, PEER_LIST + 1):
        for j in range(1, PEER_LIST // i + 1):
            t = _insert_sorted(t, a[i - 1] + b[j - 1])
    tau = 0.5 * (t[PEER_TOPK - 1] + t[PEER_TOPK])
    top = a[0] + b[0]
    z = jnp.exp(t[0] - top)
    for r in range(1, PEER_TOPK):
        z = z + jnp.exp(t[r] - top)
    cv = b[0] + jnp.log(2.0 * z)
    a16 = a[:PEER_TOPK]

    def write(n, carry):
        s1 = s_ref[n]
        s2 = s_ref[N_KEYS + n]
        c = 1.0 + _count_above(a16, s1, True)
        e1 = jnp.exp(s1 - a16[0])
        for gq in range(shape[0] // PEER_ROW):
            c_ref[n, gq] = c[gq * PEER_ROW:(gq + 1) * PEER_ROW, :]
            e1_ref[n, gq] = e1[gq * PEER_ROW:(gq + 1) * PEER_ROW, :]
        q_ref[n] = _count_above(a16, tau - s2, False)
        e2_ref[n] = jnp.exp(s2 - cv)
        return carry

    lax.fori_loop(0, N_KEYS, write, 0, unroll=2)


def _peer_thresh(s3):
    rows, nl, _ = s3.shape
    n = nl * LANES
    nb = min(PEER_TOK // LANES, nl)
    half = rows // 2
    ng = nb // PEER_ROW
    row_spec = pl.BlockSpec((N_KEYS, ng, PEER_ROW, LANES), lambda i, h: (h, i, 0, 0))
    row_shape = jax.ShapeDtypeStruct((half, nl // PEER_ROW, PEER_ROW, LANES), F32)
    tile_spec = pl.BlockSpec((N_KEYS, nb, LANES), lambda i, h: (h, i, 0))
    tile_shape = jax.ShapeDtypeStruct((half, nl, LANES), F32)
    c, e1, q, e2 = pl.pallas_call(
        _peer_thresh_kernel,
        grid=(nl // nb, PEER_HEADS),
        in_specs=[pl.BlockSpec((2 * N_KEYS, nb, LANES), lambda i, h: (h, i, 0))],
        out_specs=[row_spec, row_spec, tile_spec, tile_spec],
        out_shape=[row_shape, row_shape, tile_shape, tile_shape],
        compiler_params=_params(("parallel", "parallel")),
        name="peer_thresh",
    )(s3)
    shape5 = (PEER_HEADS, N_KEYS, nl // PEER_ROW, PEER_ROW, LANES)
    tiles = lambda v: v.reshape(PEER_HEADS, N_KEYS, n).astype(BF16)
    return c.reshape(shape5), e1.reshape(shape5), tiles(q), tiles(e2)


def _peer_dense_kernel(xt_ref, u_ref, vt_ref, q_ref, e2_ref, c_ref, e1_ref, h2_ref, g_ref, b_ref,
                       o_ref, acc_scr, a_scr, *, sub):
    e = pl.program_id(1)
    eb = u_ref.shape[0]

    @pl.when(e == 0)
    def _():
        acc_scr[...] = jnp.zeros(acc_scr.shape, F32)

    def row(ref, h, il):
        return jnp.concatenate([ref[h, il, 0, c:c + 1, :] for c in range(ref.shape[3])], axis=1).astype(BF16)

    zero = jnp.zeros((), BF16)

    xt = xt_ref[...]
    nsub = eb // sub
    score = lambda s: _dot(u_ref[s * sub:(s + 1) * sub, :], xt)
    st_next = score(0)
    for s in range(nsub):
        st, st_next = st_next, (score(s + 1) if s + 1 < nsub else None)
        for ii in range(sub // N_KEYS):
            il = s * (sub // N_KEYS) + ii
            gate = None
            for h in range(PEER_HEADS):
                term = row(e1_ref, h, il) * jnp.where(q_ref[h] >= row(c_ref, h, il), e2_ref[h], zero)
                gate = term if gate is None else gate + term
            a = _gelu_x2(st[ii * N_KEYS:(ii + 1) * N_KEYS, :]).astype(BF16) * gate
            a_scr[s * sub + ii * N_KEYS:s * sub + (ii + 1) * N_KEYS, :] = a
        acc_scr[...] += _dot(vt_ref[:, s * sub:(s + 1) * sub], a_scr[s * sub:(s + 1) * sub, :])

    @pl.when(e == pl.num_programs(1) - 1)
    def _():
        y = acc_scr[...].T
        o_ref[...] = _ln(DEEPNORM_ALPHA * h2_ref[...].astype(F32) + y, g_ref[...], b_ref[...])


def _peer_dense(h2, h2t, q, e2, c, e1, u_tab, vt_tab, ln_g, ln_b):
    n = h2.shape[0]
    t = min(n, PEER_ROW * LANES)
    eb = 8 * N_KEYS
    sub = PEER_SUB
    vec = pl.BlockSpec((1, D_MODEL), lambda i, e: (0, 0))
    row_spec = pl.BlockSpec((PEER_HEADS, eb // N_KEYS, 1, PEER_ROW, LANES), lambda i, e: (0, e, i, 0, 0))
    return pl.pallas_call(
        functools.partial(_peer_dense_kernel, sub=sub),
        grid=(n // t, N_EXPERTS // eb),
        in_specs=[
            pl.BlockSpec((D_MODEL, t), lambda i, e: (0, i)),
            pl.BlockSpec((eb, D_MODEL), lambda i, e: (e, 0)),
            pl.BlockSpec((D_MODEL, eb), lambda i, e: (0, e)),
            pl.BlockSpec((PEER_HEADS, N_KEYS, t), lambda i, e: (0, 0, i)),
            pl.BlockSpec((PEER_HEADS, N_KEYS, t), lambda i, e: (0, 0, i)),
            row_spec, row_spec,
            pl.BlockSpec((t, D_MODEL), lambda i, e: (i, 0)),
            vec, vec,
        ],
        out_specs=pl.BlockSpec((t, D_MODEL), lambda i, e: (i, 0)),
        out_shape=jax.ShapeDtypeStruct((n, D_MODEL), F32),
        scratch_shapes=[
            pltpu.VMEM((D_MODEL, t), F32),
            pltpu.VMEM((eb, t), BF16),
        ],
        compiler_params=_params(("parallel", "arbitrary")),
        name="peer_dense",
    )(h2t, u_tab, vt_tab, q, e2, c, e1, h2, ln_g, ln_b)


def _rot_half_cols(w):
    half = QK_ROPE // 2
    return jnp.concatenate([-w[..., half:], w[..., :half]], axis=-1)


def kernel(x, ln_in_g, ln_in_b, w_in, q_norm_g, w_uq, kv_norm_g, w_ukv, ssm_lam_re, ssm_lam_im, ssm_log_dt,
           ssm_b_re, ssm_b_im, ssm_c_re, ssm_c_im, ssm_d, w_glu, w_proj_attn, w_proj_ssm, w_out, ln_mix_g, ln_mix_b,
           w_peer_q, peer_sub_keys, peer_u, peer_v, ln_ffn_g, ln_ffn_b):
    batch, seq, _ = x.shape
    n = batch * seq
    x2 = x.reshape(n, D_MODEL)
    row = lambda v: v.reshape(1, -1).astype(F32)
    l = 0

    wi = w_in[l]
    w_main = jnp.concatenate([wi[:, OFF_SSM:OFF_KR], wi[:, OFF_GA:IN_WIDTH]], axis=1).astype(BF16)
    w_krc = wi[:, OFF_KR:OFF_GA]
    w_kr = jnp.concatenate([w_krc, _rot_half_cols(w_krc)], axis=1).astype(BF16)
    wq = w_uq[l].reshape(Q_LORA, N_HEADS, QK_HEAD)
    wq_ext = jnp.concatenate([wq, _rot_half_cols(wq[..., QK_NOPE:])], axis=-1)
    wq_ext = wq_ext.transpose(1, 0, 2).astype(BF16)
    wkv = w_ukv[l].reshape(KV_LORA, N_HEADS, QK_NOPE + V_HEAD).transpose(1, 0, 2).astype(BF16)

    cs = _rope_table(seq)
    u, z, kr = _inproj(x2, row(ln_in_g), row(ln_in_b), w_main, w_kr)

    q, k, v = _qkv(z, kr, cs, row(q_norm_g[l]), row(kv_norm_g[l]), wq_ext, wkv, batch, seq)
    attn = _attention(q, k, v).reshape(n, N_HEADS * V_HEAD)

    w, ws, wx, a16 = _ssm_wgen(ssm_lam_re[l], ssm_lam_im[l], ssm_log_dt[l], ssm_b_re[l], ssm_b_im[l],
                               ssm_c_re[l], ssm_c_im[l])
    g = _ssm_branch(u, batch, seq, w, ws, wx, a16, row(ssm_d[l]))

    merged = _merge(attn, g, z, w_glu[l].astype(BF16), w_proj_attn[l].astype(BF16), w_proj_ssm[l].astype(BF16))
    h2, h2t = _outproj(merged, x2, row(ln_in_g), row(ln_in_b), row(ln_mix_g[l]), row(ln_mix_b[l]),
                       w_out[l].astype(BF16))

    wcomb = _peer_wcomb(peer_sub_keys[l], w_peer_q[l].astype(BF16))
    s3 = _peer_scores(wcomb, h2t)
    c, e1, q, e2 = _peer_thresh(s3)
    out = _peer_dense(h2, h2t, q, e2, c, e1, peer_u[l].astype(BF16), peer_v[l].T.astype(BF16),
                      row(ln_ffn_g[l]), row(ln_ffn_b[l]))
    return out.reshape(batch, seq, D_MODEL)
```

```python
import functools
import math

import jax
import jax.numpy as jnp
from jax import lax
from jax.experimental import pallas as pl
from jax.experimental.pallas import tpu as pltpu

F32 = jnp.float32
BF16 = jnp.bfloat16

D_MODEL = 2048
DEPTH = 1
CHUNK = 64
SSM_WIDTH = 1024
SSM_GROUP = 16
SSM_GROUPS = SSM_WIDTH // SSM_GROUP
SSM_STATE = 64
N_HEADS = 16
Q_LORA = 512
KV_LORA = 512
QK_NOPE = 128
QK_ROPE = 64
V_HEAD = 128
QK_HEAD = QK_NOPE + QK_ROPE
ROPE_THETA = 10000.0
PEER_HEADS = 8
N_KEYS = 128
N_EXPERTS = N_KEYS * N_KEYS
PEER_KEY_HALF = 128
PEER_TOPK = 16
LN_EPS = 1e-5
RMS_EPS = 1e-6
DEEPNORM_ALPHA = (2.0 * DEPTH) ** 0.25
OFF_SSM = 0
OFF_Q = OFF_SSM + SSM_WIDTH
OFF_KV = OFF_Q + Q_LORA
OFF_KR = OFF_KV + KV_LORA
OFF_GA = OFF_KR + QK_ROPE
OFF_GS = OFF_GA + D_MODEL
IN_WIDTH = OFF_GS + D_MODEL

LANES = 128
VMEM_LIMIT_BYTES = 56 * 1024 * 1024

SSM_CHUNK = 16
SSM_TILE_GROUPS = LANES // SSM_GROUP
SSM_TILES = SSM_WIDTH // LANES
SSM_TILE_STATE = SSM_TILE_GROUPS * SSM_STATE
NEG_BIG = -1e30


def _params(semantics):
    return pltpu.CompilerParams(dimension_semantics=semantics, vmem_limit_bytes=VMEM_LIMIT_BYTES)


def _dot(a, b):
    return jnp.dot(a, b, preferred_element_type=F32)


def _ln(x, g, b):
    mu = jnp.mean(x, axis=-1, keepdims=True)
    xc = x - mu
    var = jnp.mean(xc * xc, axis=-1, keepdims=True)
    return xc * lax.rsqrt(var + LN_EPS) * g + b


GELU_C = 0.7978845608028654
GELU_A = 0.044715


def _gelu(x):
    return 0.5 * x * (1.0 + jnp.tanh(GELU_C * (x + GELU_A * (x * x * x))))


def _gelu_x2(x):
    return x * (1.0 + jnp.tanh(x * (GELU_C + (GELU_C * GELU_A) * (x * x))))


def _sigmoid(x):
    return 1.0 / (1.0 + jnp.exp(-x))


def _rope_table_kernel(invf_ref, o_ref):
    ts = o_ref.shape[0]
    pos = (lax.broadcasted_iota(jnp.int32, (ts, LANES), 0) + pl.program_id(0) * ts).astype(F32)
    ang = pos * invf_ref[...]
    lane = lax.broadcasted_iota(jnp.int32, (ts, LANES), 1)
    o_ref[...] = jnp.where(lane < 2 * (QK_ROPE // 2), jnp.cos(ang), jnp.sin(ang))


def _rope_table(seq):
    half = QK_ROPE // 2
    invf = 1.0 / (ROPE_THETA ** (jnp.arange(half, dtype=F32) / half))
    invf4 = jnp.tile(invf, 4)[None, :]
    ts = min(seq, 1024)
    return pl.pallas_call(
        _rope_table_kernel,
        grid=(seq // ts,),
        in_specs=[pl.BlockSpec((1, LANES), lambda i: (0, 0))],
        out_specs=pl.BlockSpec((ts, LANES), lambda i: (i, 0)),
        out_shape=jax.ShapeDtypeStruct((seq, LANES), F32),
        compiler_params=_params(("parallel",)),
        name="rope_table",
    )(invf4)


def _inproj_kernel(x_ref, g_ref, b_ref, w_ref, wkr_ref, u_ref, z_ref, kr_ref, h_scr):
    j = pl.program_id(1)

    @pl.when(j == 0)
    def _():
        h = _ln(x_ref[...], g_ref[...], b_ref[...]).astype(BF16)
        h_scr[...] = h
        u_ref[...] = _dot(h, w_ref[...])
        kr_ref[...] = _dot(h, wkr_ref[...]).astype(kr_ref.dtype)

    @pl.when(j > 0)
    def _():
        z_ref[...] = _dot(h_scr[...], w_ref[...]).astype(z_ref.dtype)


def _inproj(x2, ln_g, ln_b, w_main, w_kr):
    n = x2.shape[0]
    tm = min(n, 1024)
    tn = 1024
    nj = w_main.shape[1] // tn
    return pl.pallas_call(
        _inproj_kernel,
        grid=(n // tm, nj),
        in_specs=[
            pl.BlockSpec((tm, D_MODEL), lambda i, j: (i, 0)),
            pl.BlockSpec((1, D_MODEL), lambda i, j: (0, 0)),
            pl.BlockSpec((1, D_MODEL), lambda i, j: (0, 0)),
            pl.BlockSpec((D_MODEL, tn), lambda i, j: (0, j)),
            pl.BlockSpec((D_MODEL, LANES), lambda i, j: (0, 0)),
        ],
        out_specs=[
            pl.BlockSpec((tm, tn), lambda i, j: (i, 0)),
            pl.BlockSpec((tm, tn), lambda i, j: (i, jnp.maximum(j - 1, 0))),
            pl.BlockSpec((tm, LANES), lambda i, j: (i, 0)),
        ],
        out_shape=[
            jax.ShapeDtypeStruct((n, tn), F32),
            jax.ShapeDtypeStruct((n, (nj - 1) * tn), BF16),
            jax.ShapeDtypeStruct((n, LANES), BF16),
        ],
        scratch_shapes=[pltpu.VMEM((tm, D_MODEL), BF16)],
        compiler_params=_params(("parallel", "arbitrary")),
        name="ln_inproj",
    )(x2, ln_g, ln_b, w_main, w_kr)


def _rope_mix(t2, cs):
    t = t2 * cs
    return t + pltpu.roll(t, QK_ROPE, 1)


def _qkv_kernel(cq_ref, ckv_ref, kr_ref, cs_ref, gq_ref, gkv_ref, wq_ref, wkv_ref,
                q_ref, k_ref, v_ref, cqn_scr, ckvn_scr, kro_scr):
    cs = cs_ref[...]

    @pl.when(pl.program_id(1) == 0)
    def _():
        def rms(c, g):
            c = c.astype(F32)
            return (c * lax.rsqrt(jnp.mean(c * c, axis=-1, keepdims=True) + RMS_EPS) * g).astype(BF16)

        cqn_scr[...] = rms(cq_ref[...], gq_ref[...])
        ckvn_scr[...] = rms(ckv_ref[...], gkv_ref[...])
        kro_scr[...] = _rope_mix(kr_ref[...].astype(F32), cs).astype(BF16)

    scale = math.log2(math.e) / math.sqrt(QK_HEAD)
    hp = wq_ref.shape[0]
    wide = 2 * LANES
    qe2 = _dot(cqn_scr[...], jnp.concatenate([wq_ref[j] for j in range(hp)], axis=1))
    kv2 = _dot(ckvn_scr[...], jnp.concatenate([wkv_ref[j] for j in range(hp)], axis=1))
    for j in range(hp):
        qe = qe2[:, j * wide:(j + 1) * wide]
        qr = _rope_mix(qe[:, QK_NOPE:], cs)
        q_ref[0, j, :, :QK_NOPE] = (qe[:, :QK_NOPE] * scale).astype(BF16)
        q_ref[0, j, :, QK_NOPE:] = (qr[:, :QK_ROPE] * scale).astype(BF16)
        kv = kv2[:, j * wide:(j + 1) * wide]
        k_ref[0, j, :, :QK_NOPE] = kv[:, :QK_NOPE].astype(BF16)
        k_ref[0, j, :, QK_NOPE:] = kro_scr[:, :QK_ROPE]
        v_ref[0, j] = kv[:, QK_NOPE:].astype(BF16)


def _qkv(z, kr, cs, gq, gkv, wq_ext, wkv, batch, seq):
    n = z.shape[0]
    tm = min(seq, 2048)
    nsb = seq // tm
    hp = 2
    return pl.pallas_call(
        _qkv_kernel,
        grid=(n // tm, N_HEADS // hp),
        in_specs=[
            pl.BlockSpec((tm, Q_LORA), lambda i, h: (i, 0)),
            pl.BlockSpec((tm, KV_LORA), lambda i, h: (i, 1)),
            pl.BlockSpec((tm, LANES), lambda i, h: (i, 0)),
            pl.BlockSpec((tm, LANES), lambda i, h: (i % nsb, 0)),
            pl.BlockSpec((1, Q_LORA), lambda i, h: (0, 0)),
            pl.BlockSpec((1, KV_LORA), lambda i, h: (0, 0)),
            pl.BlockSpec((hp, Q_LORA, 2 * LANES), lambda i, h: (h, 0, 0)),
            pl.BlockSpec((hp, KV_LORA, 2 * LANES), lambda i, h: (h, 0, 0)),
        ],
        out_specs=[
            pl.BlockSpec((1, hp, tm, QK_HEAD), lambda i, h: (i // nsb, h, i % nsb, 0)),
            pl.BlockSpec((1, hp, tm, QK_HEAD), lambda i, h: (i // nsb, h, i % nsb, 0)),
            pl.BlockSpec((1, hp, tm, V_HEAD), lambda i, h: (i // nsb, h, i % nsb, 0)),
        ],
        out_shape=[
            jax.ShapeDtypeStruct((batch, N_HEADS, seq, QK_HEAD), BF16),
            jax.ShapeDtypeStruct((batch, N_HEADS, seq, QK_HEAD), BF16),
            jax.ShapeDtypeStruct((batch, N_HEADS, seq, V_HEAD), BF16),
        ],
        scratch_shapes=[
            pltpu.VMEM((tm, Q_LORA), BF16),
            pltpu.VMEM((tm, KV_LORA), BF16),
            pltpu.VMEM((tm, LANES), BF16),
        ],
        compiler_params=_params(("parallel", "arbitrary")),
        name="qkv_rope",
    )(z, z, kr, cs, gq, gkv, wq_ext, wkv)


ATT_ROWS = 512
ATT_CHAINS = 8
ATT_BK_MAIN = 1024


def _attn_kernel(q_ref, k_ref, v_ref, o_ref, m_scr, acc_scr, *, rows, chains, bkm):
    qi = pl.program_id(2)
    m_scr[...] = jnp.full(m_scr.shape, -jnp.inf, F32)
    acc_scr[...] = jnp.zeros(acc_scr.shape, F32)

    def load_kv(j, bk):
        start = pl.multiple_of(j * bk, bk)
        kj = k_ref[0, 0, pl.ds(start, bk), :]
        vj = jnp.concatenate([v_ref[0, 0, pl.ds(start, bk), :], jnp.ones((bk, LANES), BF16)], axis=1)
        return kj, vj

    def block(c, kj, vj, masked):
        q = q_ref[0, 0, c * rows:(c + 1) * rows, :]
        s = lax.dot_general(q, kj, (((1,), (1,)), ((), ())), preferred_element_type=F32)
        if masked:
            rc = lax.broadcasted_iota(jnp.int32, s.shape, 0) // CHUNK
            cc = lax.broadcasted_iota(jnp.int32, s.shape, 1) // CHUNK
            s = jnp.where(cc <= rc, s, NEG_BIG)
        m_prev = m_scr[c]
        m_new = jnp.maximum(m_prev, jnp.max(s, axis=-1, keepdims=True))
        alpha = jnp.exp2(m_prev - m_new)
        p = jnp.exp2(s - jnp.tile(m_new, (1, s.shape[1] // LANES)))
        acc_scr[c] = jnp.tile(alpha, (1, 2)) * acc_scr[c] + _dot(p.astype(BF16), vj)
        m_scr[c] = m_new

    def body(j, carry):
        kj, vj = load_kv(j, bkm)
        for c in range(chains):
            block(c, kj, vj, False)
        return carry

    lax.fori_loop(0, qi * (chains * rows // bkm), body, 0)
    for jj in range(chains):
        kj, vj = load_kv(qi * chains + jj, rows)
        for c in range(jj, chains):
            block(c, kj, vj, c == jj)
    for c in range(chains):
        a = acc_scr[c]
        o_ref[0, c * rows:(c + 1) * rows, :] = (a[:, :V_HEAD] / a[:, V_HEAD:]).astype(o_ref.dtype)


def _attention(q, k, v):
    batch, _, seq, _ = q.shape
    rows = min(seq, ATT_ROWS)
    chains = min(ATT_CHAINS, seq // rows)
    bq = rows * chains
    bkm = min(ATT_BK_MAIN, bq)
    return pl.pallas_call(
        functools.partial(_attn_kernel, rows=rows, chains=chains, bkm=bkm),
        grid=(batch, N_HEADS, seq // bq),
        in_specs=[
            pl.BlockSpec((1, 1, bq, QK_HEAD), lambda b, h, i: (b, h, i, 0)),
            pl.BlockSpec((1, 1, seq, QK_HEAD), lambda b, h, i: (b, h, 0, 0)),
            pl.BlockSpec((1, 1, seq, V_HEAD), lambda b, h, i: (b, h, 0, 0)),
        ],
        out_specs=pl.BlockSpec((1, bq, V_HEAD), lambda b, h, i: (b, i, h)),
        out_shape=jax.ShapeDtypeStruct((batch, seq, N_HEADS * V_HEAD), BF16),
        scratch_shapes=[
            pltpu.VMEM((chains, rows, LANES), F32),
            pltpu.VMEM((chains, rows, 2 * V_HEAD), F32),
        ],
        compiler_params=_params(("parallel", "parallel", "arbitrary")),
        name="flash_attn",
    )(q, k, v)


def _cpow(lr, li, ldt, k):
    dt = jnp.exp(ldt)
    mag = jnp.exp(lr * dt * k)
    ang = li * dt * k
    return mag * jnp.cos(ang), mag * jnp.sin(ang)


def _ssm_wgen_kernel(lr_r, li_r, ldt_r, bre_ref, bim_ref, cre_ref, cim_ref,
                     w_ref, ws_ref, wx_ref, a16_ref):
    hp = lax.Precision.HIGHEST
    ns = SSM_TILE_STATE
    lr, li, ldt = lr_r[0], li_r[0], ldt_r[0]
    a_re, a_im = _cpow(lr, li, ldt, 1.0)
    den = lr * lr + li * li
    nr = a_re - 1.0
    ni = a_im
    cf_re = (nr * lr + ni * li) / den
    cf_im = (ni * lr - nr * li) / den
    bre, bim = bre_ref[0], bim_ref[0]
    cre, cim = cre_ref[0], cim_ref[0]
    w_ref[...] = jnp.zeros(w_ref.shape, w_ref.dtype)
    for k in range(SSM_CHUNK):
        p_re, p_im = _cpow(lr, li, ldt, float(k))
        m_re = cf_re * p_re - cf_im * p_im
        m_im = cf_re * p_im + cf_im * p_re
        bm_re = bre * m_re - bim * m_im
        bm_im = bre * m_im + bim * m_re
        bd = (jnp.dot(bm_re, cre, precision=hp, preferred_element_type=F32)
              - jnp.dot(bm_im, cim, precision=hp, preferred_element_type=F32)).astype(BF16)
        for j in range(SSM_CHUNK - k):
            i = j + k
            w_ref[0, j * LANES:(j + 1) * LANES, i * LANES:(i + 1) * LANES] = bd
        jj = SSM_CHUNK - 1 - k
        ws_ref[0, jj * LANES:(jj + 1) * LANES, :ns] = bm_re.astype(BF16)
        ws_ref[0, jj * LANES:(jj + 1) * LANES, ns:] = bm_im.astype(BF16)
    cre_t, cim_t = cre.T, cim.T
    for i in range(SSM_CHUNK):
        p_re, p_im = _cpow(lr, li, ldt, float(i + 1))
        wx_ref[0, :ns, i * LANES:(i + 1) * LANES] = (p_re * cre_t - p_im * cim_t).T.astype(BF16)
        wx_ref[0, ns:, i * LANES:(i + 1) * LANES] = (-(p_re * cim_t + p_im * cre_t)).T.astype(BF16)
    q_re, q_im = _cpow(lr, li, ldt, float(SSM_CHUNK))
    a16_ref[0, :, :ns] = q_re
    a16_ref[0, :, ns:] = q_im


def _ssm_wgen(lam_re, lam_im, log_dt, b_re, b_im, c_re, c_im):
    nt, tg, ns = SSM_TILES, SSM_TILE_GROUPS, SSM_TILE_STATE
    eye = jnp.eye(tg, dtype=F32)
    bbd = lambda b: jnp.einsum("cgph,gk->cghkp", b.reshape(nt, tg, SSM_STATE, SSM_GROUP), eye).reshape(nt, LANES, ns)
    cbd = lambda c: jnp.einsum("cghp,gk->cgpkh", c.reshape(nt, tg, SSM_GROUP, SSM_STATE), eye).reshape(nt, ns, LANES)
    ldt = jnp.repeat(log_dt, SSM_STATE)
    rows = [a.reshape(nt, 1, ns) for a in (lam_re, lam_im, ldt)]
    kc = SSM_CHUNK * LANES
    row_spec = pl.BlockSpec((1, 1, ns), lambda c: (c, 0, 0))
    return pl.pallas_call(
        _ssm_wgen_kernel,
        grid=(nt,),
        in_specs=[row_spec] * 3 + [
            pl.BlockSpec((1, LANES, ns), lambda c: (c, 0, 0)),
            pl.BlockSpec((1, LANES, ns), lambda c: (c, 0, 0)),
            pl.BlockSpec((1, ns, LANES), lambda c: (c, 0, 0)),
            pl.BlockSpec((1, ns, LANES), lambda c: (c, 0, 0)),
        ],
        out_specs=[
            pl.BlockSpec((1, kc, kc), lambda c: (c, 0, 0)),
            pl.BlockSpec((1, kc, 2 * ns), lambda c: (c, 0, 0)),
            pl.BlockSpec((1, 2 * ns, kc), lambda c: (c, 0, 0)),
            pl.BlockSpec((1, 1, 2 * ns), lambda c: (c, 0, 0)),
        ],
        out_shape=[
            jax.ShapeDtypeStruct((nt, kc, kc), BF16),
            jax.ShapeDtypeStruct((nt, kc, 2 * ns), BF16),
            jax.ShapeDtypeStruct((nt, 2 * ns, kc), BF16),
            jax.ShapeDtypeStruct((nt, 1, 2 * ns), F32),
        ],
        compiler_params=_params(("parallel",)),
        name="ssm_wgen",
    )(*rows, bbd(b_re), bbd(b_im), cbd(c_re), cbd(c_im))


def _chunk_rows(u_ref, tb):
    return [u_ref[pl.ds(j, tb, stride=SSM_CHUNK), :] for j in range(SSM_CHUNK)]


def _ssm_sum_kernel(u_ref, ws_ref, s_ref, *, tb):
    u16 = jnp.concatenate(_chunk_rows(u_ref, tb), axis=1).astype(BF16)
    s_ref[0] = _dot(u16, ws_ref[0])


def _ssm_scan_kernel(s_ref, a16_ref, x_ref):
    ns = SSM_TILE_STATE
    a_re = a16_ref[0, :, :ns]
    a_im = a16_ref[0, :, ns:]
    nrow = s_ref.shape[1]

    def body(r, carry):
        xr, xi = carry
        x_ref[0, pl.ds(r, 1), :ns] = xr
        x_ref[0, pl.ds(r, 1), ns:] = xi
        sr = s_ref[0, pl.ds(r, 1), :ns]
        si = s_ref[0, pl.ds(r, 1), ns:]
        return a_re * xr - a_im * xi + sr, a_re * xi + a_im * xr + si

    zero = jnp.zeros((1, ns), F32)
    lax.fori_loop(0, nrow, body, (zero, zero), unroll=8)


def _ssm_out_kernel(u_ref, x_ref, w_ref, wx_ref, d_ref, g_ref, *, tb):
    us = _chunk_rows(u_ref, tb)
    u16 = jnp.concatenate(us, axis=1).astype(BF16)
    y16 = _dot(u16, w_ref[0]) + _dot(x_ref[0].astype(BF16), wx_ref[0])
    d = d_ref[...]
    for i in range(SSM_CHUNK):
        y = y16[:, i * LANES:(i + 1) * LANES] + d * us[i]
        g_ref[pl.ds(i, tb, stride=SSM_CHUNK), :] = _gelu(y)


def _ssm_branch(u, batch, seq, w, ws, wx, a16, d_skip):
    n = u.shape[0]
    nt, ns = SSM_TILES, SSM_TILE_STATE
    kc = SSM_CHUNK * LANES
    nb = n // SSM_CHUNK
    nbs = seq // SSM_CHUNK
    tb = min(nbs, 256)
    tok = tb * SSM_CHUNK
    s = pl.pallas_call(
        functools.partial(_ssm_sum_kernel, tb=tb),
        grid=(nt, nb // tb),
        in_specs=[
            pl.BlockSpec((tok, LANES), lambda c, r: (r, c)),
            pl.BlockSpec((1, kc, 2 * ns), lambda c, r: (c, 0, 0)),
        ],
        out_specs=pl.BlockSpec((1, tb, 2 * ns), lambda c, r: (c, r, 0)),
        out_shape=jax.ShapeDtypeStruct((nt, nb, 2 * ns), F32),
        compiler_params=_params(("parallel", "parallel")),
        name="ssm_chunk_sum",
    )(u, ws)
    x = pl.pallas_call(
        _ssm_scan_kernel,
        grid=(nt, batch),
        in_specs=[
            pl.BlockSpec((1, nbs, 2 * ns), lambda c, b: (c, b, 0)),
            pl.BlockSpec((1, 1, 2 * ns), lambda c, b: (c, 0, 0)),
        ],
        out_specs=pl.BlockSpec((1, nbs, 2 * ns), lambda c, b: (c, b, 0)),
        out_shape=jax.ShapeDtypeStruct((nt, nb, 2 * ns), F32),
        compiler_params=_params(("parallel", "parallel")),
        name="ssm_state_scan",
    )(s, a16)
    return pl.pallas_call(
        functools.partial(_ssm_out_kernel, tb=tb),
        grid=(nt, nb // tb),
        in_specs=[
            pl.BlockSpec((tok, LANES), lambda c, r: (r, c)),
            pl.BlockSpec((1, tb, 2 * ns), lambda c, r: (c, r, 0)),
            pl.BlockSpec((1, kc, kc), lambda c, r: (c, 0, 0)),
            pl.BlockSpec((1, 2 * ns, kc), lambda c, r: (c, 0, 0)),
            pl.BlockSpec((1, LANES), lambda c, r: (0, c)),
        ],
        out_specs=pl.BlockSpec((tok, LANES), lambda c, r: (r, c)),
        out_shape=jax.ShapeDtypeStruct((n, SSM_WIDTH), F32),
        compiler_params=_params(("parallel", "parallel")),
        name="ssm_chunk_out",
    )(u, x, w, wx, d_skip)


def _merge_kernel(attn_ref, g_ref, ga_ref, gs_ref, wglu_ref, wpa_ref, wps_ref, o_ref, ssm_scr):
    @pl.when(pl.program_id(1) == 0)
    def _():
        g = g_ref[...]
        ssm_scr[...] = (g * _sigmoid(_dot(g.astype(BF16), wglu_ref[...]))).astype(BF16)

    pa = _dot(attn_ref[...], wpa_ref[...])
    ps = _dot(ssm_scr[...], wps_ref[...])
    o_ref[...] = (_sigmoid(ga_ref[...].astype(F32)) * pa + _sigmoid(gs_ref[...].astype(F32)) * ps).astype(o_ref.dtype)


def _merge(attn, g, z, w_glu, w_pa, w_ps):
    n = attn.shape[0]
    tm = min(n, 512)
    tn = 1024
    nj = D_MODEL // tn
    return pl.pallas_call(
        _merge_kernel,
        grid=(n // tm, nj),
        in_specs=[
            pl.BlockSpec((tm, N_HEADS * V_HEAD), lambda i, j: (i, 0)),
            pl.BlockSpec((tm, SSM_WIDTH), lambda i, j: (i, 0)),
            pl.BlockSpec((tm, tn), lambda i, j: (i, 1 + j)),
            pl.BlockSpec((tm, tn), lambda i, j: (i, 1 + nj + j)),
            pl.BlockSpec((SSM_WIDTH, SSM_WIDTH), lambda i, j: (0, 0)),
            pl.BlockSpec((N_HEADS * V_HEAD, tn), lambda i, j: (0, j)),
            pl.BlockSpec((SSM_WIDTH, tn), lambda i, j: (0, j)),
        ],
        out_specs=pl.BlockSpec((tm, tn), lambda i, j: (i, j)),
        out_shape=jax.ShapeDtypeStruct((n, D_MODEL), BF16),
        scratch_shapes=[pltpu.VMEM((tm, SSM_WIDTH), BF16)],
        compiler_params=_params(("parallel", "arbitrary")),
        name="glu_merge",
    )(attn, g, z, z, w_glu, w_pa, w_ps)


def _outproj_kernel(m_ref, x_ref, g1_ref, b1_ref, g2_ref, b2_ref, w_ref, h2_ref, h2t_ref):
    y = _dot(m_ref[...], w_ref[...])
    h = _ln(x_ref[...], g1_ref[...], b1_ref[...])
    h2 = _ln(DEEPNORM_ALPHA * h + y, g2_ref[...], b2_ref[...])
    h2_ref[...] = h2.astype(h2_ref.dtype)
    h2t_ref[...] = h2.T.astype(h2t_ref.dtype)


def _outproj(merged, x2, g1, b1, g2, b2, w_out):
    n = merged.shape[0]
    tm = min(n, 512)
    vec = pl.BlockSpec((1, D_MODEL), lambda i: (0, 0))
    return pl.pallas_call(
        _outproj_kernel,
        grid=(n // tm,),
        in_specs=[
            pl.BlockSpec((tm, D_MODEL), lambda i: (i, 0)),
            pl.BlockSpec((tm, D_MODEL), lambda i: (i, 0)),
            vec, vec, vec, vec,
            pl.BlockSpec((D_MODEL, D_MODEL), lambda i: (0, 0)),
        ],
        out_specs=[
            pl.BlockSpec((tm, D_MODEL), lambda i: (i, 0)),
            pl.BlockSpec((D_MODEL, tm), lambda i: (0, i)),
        ],
        out_shape=[
            jax.ShapeDtypeStruct((n, D_MODEL), BF16),
            jax.ShapeDtypeStruct((D_MODEL, n), BF16),
        ],
        compiler_params=_params(("parallel",)),
        name="outproj_ln",
    )(merged, x2, g1, b1, g2, b2, w_out)


def _peer_wcomb_kernel(keys_ref, wpq_ref, o_ref):
    o_ref[...] = lax.dot_general(keys_ref[0].astype(BF16), wpq_ref[...], (((1,), (1,)), ((), ())),
                                 preferred_element_type=F32).astype(o_ref.dtype)


def _peer_wcomb(sub_keys, w_pq):
    nhc = PEER_HEADS * 2
    keys = sub_keys.reshape(nhc, N_KEYS, PEER_KEY_HALF)
    return pl.pallas_call(
        _peer_wcomb_kernel,
        grid=(nhc,),
        in_specs=[
            pl.BlockSpec((1, N_KEYS, PEER_KEY_HALF), lambda i: (i, 0, 0)),
            pl.BlockSpec((D_MODEL, PEER_KEY_HALF), lambda i: (0, i)),
        ],
        out_specs=pl.BlockSpec((N_KEYS, D_MODEL), lambda i: (i, 0)),
        out_shape=jax.ShapeDtypeStruct((nhc * N_KEYS, D_MODEL), BF16),
        compiler_params=_params(("parallel",)),
        name="peer_wcomb",
    )(keys, w_pq)


PEER_TOK = 1024
PEER_THRESH_TOK = 2048


def _peer_scores_kernel(w_ref, xt_ref, s3_ref):
    rows = w_ref.shape[0] // 2
    for g in range(2):
        st = _dot(w_ref[g * rows:(g + 1) * rows, :], xt_ref[...])
        for c in range(s3_ref.shape[1]):
            s3_ref[g * rows:(g + 1) * rows, c, :] = st[:, c * LANES:(c + 1) * LANES]


def _peer_scores(wcomb, h2t):
    n = h2t.shape[1]
    tl = min(n, PEER_TOK)
    rows = wcomb.shape[0]
    return pl.pallas_call(
        _peer_scores_kernel,
        grid=(n // tl,),
        in_specs=[
            pl.BlockSpec((rows, D_MODEL), lambda i: (0, 0)),
            pl.BlockSpec((D_MODEL, tl), lambda i: (0, i)),
        ],
        out_specs=pl.BlockSpec((rows, tl // LANES, LANES), lambda i: (0, i, 0)),
        out_shape=jax.ShapeDtypeStruct((rows, n // LANES, LANES), F32),
        compiler_params=_params(("parallel",)),
        name="peer_scores",
    )(wcomb, h2t)


PEER_LIST = PEER_TOPK + 1
PEER_SUB = 4 * N_KEYS


def _insert_sorted(t, v):
    out = []
    for r in range(len(t)):
        out.append(jnp.maximum(t[r], v))
        v = jnp.minimum(t[r], v)
    return out


PEER_ROW = 4


def _count_above(vals, x, strict):
    above = (lambda v: v > x) if strict else (lambda v: v >= x)
    pick = lambda m, hi, lo: jnp.where(m, hi, lo)
    m8 = above(vals[7])
    m4 = above(pick(m8, vals[11], vals[3]))
    m2 = above(pick(m8, pick(m4, vals[13], vals[9]), pick(m4, vals[5], vals[1])))
    hi = pick(m4, pick(m2, vals[14], vals[12]), pick(m2, vals[10], vals[8]))
    lo = pick(m4, pick(m2, vals[6], vals[4]), pick(m2, vals[2], vals[0]))
    m1 = above(pick(m8, hi, lo))
    cnt = (pick(m8, 8.0, 0.0) + pick(m4, 4.0, 0.0)) + (pick(m2, 2.0, 0.0) + pick(m1, 1.0, 0.0))
    return pick(above(vals[15]), 16.0, cnt)


def _peer_thresh_kernel(s_ref, c_ref, e1_ref, q_ref, e2_ref):
    shape = s_ref.shape[1:]
    neg = jnp.full(shape, -jnp.inf, F32)

    def top_list(base):
        def body(n, t):
            return tuple(_insert_sorted(list(t), s_ref[base + n]))
        return lax.fori_loop(0, N_KEYS, body, (neg,) * PEER_LIST, unroll=4)

    a = top_list(0)
    b = top_list(N_KEYS)
    t = [neg] * PEER_LIST
    for i in range(1, PEER_LIST + 1):
        for j in range(1, PEER_LIST // i + 1):
            t = _insert_sorted(t, a[i - 1] + b[j - 1])
    tau = 0.5 * (t[PEER_TOPK - 1] + t[PEER_TOPK])
    top = a[0] + b[0]
    z = jnp.exp(t[0] - top)
    for r in range(1, PEER_TOPK):
        z = z + jnp.exp(t[r] - top)
    cv = b[0] + jnp.log(2.0 * z)
    a16 = a[:PEER_TOPK]

    def write(n, carry):
        s1 = s_ref[n]
        s2 = s_ref[N_KEYS + n]
        c = 1.0 + _count_above(a16, s1, True)
        e1 = jnp.exp(s1 - a16[0])
        for gq in range(shape[0] // PEER_ROW):
            c_ref[n, gq] = c[gq * PEER_ROW:(gq + 1) * PEER_ROW, :]
            e1_ref[n, gq] = e1[gq * PEER_ROW:(gq + 1) * PEER_ROW, :]
        q_ref[n] = _count_above(a16, tau - s2, False).astype(q_ref.dtype)
        e2_ref[n] = jnp.exp(s2 - cv).astype(e2_ref.dtype)
        return carry

    lax.fori_loop(0, N_KEYS, write, 0, unroll=2)


def _peer_thresh(s3):
    rows, nl, _ = s3.shape
    n = nl * LANES
    nb = min(PEER_THRESH_TOK // LANES, nl)
    half = rows // 2
    ng = nb // PEER_ROW
    row_spec = pl.BlockSpec((N_KEYS, ng, PEER_ROW, LANES), lambda i, h: (h, i, 0, 0))
    row_shape = jax.ShapeDtypeStruct((half, nl // PEER_ROW, PEER_ROW, LANES), F32)
    tile_spec = pl.BlockSpec((N_KEYS, nb, LANES), lambda i, h: (h, i, 0))
    tile_shape = jax.ShapeDtypeStruct((half, nl, LANES), BF16)
    c, e1, q, e2 = pl.pallas_call(
        _peer_thresh_kernel,
        grid=(nl // nb, PEER_HEADS),
        in_specs=[pl.BlockSpec((2 * N_KEYS, nb, LANES), lambda i, h: (h, i, 0))],
        out_specs=[row_spec, row_spec, tile_spec, tile_spec],
        out_shape=[row_shape, row_shape, tile_shape, tile_shape],
        compiler_params=_params(("parallel", "parallel")),
        name="peer_thresh",
    )(s3)
    shape5 = (PEER_HEADS, N_KEYS, nl // PEER_ROW, PEER_ROW, LANES)
    tiles = lambda v: v.reshape(PEER_HEADS, N_KEYS, n)
    return c.reshape(shape5), e1.reshape(shape5), tiles(q), tiles(e2)


def _peer_dense_kernel(xt_ref, u_ref, vt_ref, q_ref, e2_ref, c_ref, e1_ref, h2_ref, g_ref, b_ref,
                       o_ref, acc_scr, a_scr, *, sub):
    e = pl.program_id(1)
    eb = u_ref.shape[0]

    @pl.when(e == 0)
    def _():
        acc_scr[...] = jnp.zeros(acc_scr.shape, F32)

    def row(ref, h, il):
        return jnp.concatenate([ref[h, il, 0, c:c + 1, :] for c in range(ref.shape[3])], axis=1).astype(BF16)

    zero = jnp.zeros((), BF16)

    xt = xt_ref[...]
    nsub = eb // sub
    score = lambda s: _dot(u_ref[s * sub:(s + 1) * sub, :], xt)
    st_next = score(0)
    for s in range(nsub):
        st, st_next = st_next, (score(s + 1) if s + 1 < nsub else None)
        for ii in range(sub // N_KEYS):
            il = s * (sub // N_KEYS) + ii
            gate = None
            for h in range(PEER_HEADS):
                term = row(e1_ref, h, il) * jnp.where(q_ref[h] >= row(c_ref, h, il), e2_ref[h], zero)
                gate = term if gate is None else gate + term
            a = _gelu_x2(st[ii * N_KEYS:(ii + 1) * N_KEYS, :]).astype(BF16) * gate
            a_scr[s * sub + ii * N_KEYS:s * sub + (ii + 1) * N_KEYS, :] = a
        acc_scr[...] += _dot(vt_ref[:, s * sub:(s + 1) * sub], a_scr[s * sub:(s + 1) * sub, :])

    @pl.when(e == pl.num_programs(1) - 1)
    def _():
        y = acc_scr[...].T
        o_ref[...] = _ln(DEEPNORM_ALPHA * h2_ref[...].astype(F32) + y, g_ref[...], b_ref[...])


def _peer_dense(h2, h2t, q, e2, c, e1, u_tab, vt_tab, ln_g, ln_b):
    n = h2.shape[0]
    t = min(n, PEER_ROW * LANES)
    eb = 8 * N_KEYS
    sub = PEER_SUB
    vec = pl.BlockSpec((1, D_MODEL), lambda i, e: (0, 0))
    row_spec = pl.BlockSpec((PEER_HEADS, eb // N_KEYS, 1, PEER_ROW, LANES), lambda i, e: (0, e, i, 0, 0))
    return pl.pallas_call(
        functools.partial(_peer_dense_kernel, sub=sub),
        grid=(n // t, N_EXPERTS // eb),
        in_specs=[
            pl.BlockSpec((D_MODEL, t), lambda i, e: (0, i)),
            pl.BlockSpec((eb, D_MODEL), lambda i, e: (e, 0)),
            pl.BlockSpec((D_MODEL, eb), lambda i, e: (0, e)),
            pl.BlockSpec((PEER_HEADS, N_KEYS, t), lambda i, e: (0, 0, i)),
            pl.BlockSpec((PEER_HEADS, N_KEYS, t), lambda i, e: (0, 0, i)),
            row_spec, row_spec,
            pl.BlockSpec((t, D_MODEL), lambda i, e: (i, 0)),
            vec, vec,
        ],
        out_specs=pl.BlockSpec((t, D_MODEL), lambda i, e: (i, 0)),
        out_shape=jax.ShapeDtypeStruct((n, D_MODEL), F32),
        scratch_shapes=[
            pltpu.VMEM((D_MODEL, t), F32),
            pltpu.VMEM((eb, t), BF16),
        ],
        compiler_params=_params(("parallel", "arbitrary")),
        name="peer_dense",
    )(h2t, u_tab, vt_tab, q, e2, c, e1, h2, ln_g, ln_b)


def _rot_half_cols(w):
    half = QK_ROPE // 2
    return jnp.concatenate([-w[..., half:], w[..., :half]], axis=-1)


def kernel(x, ln_in_g, ln_in_b, w_in, q_norm_g, w_uq, kv_norm_g, w_ukv, ssm_lam_re, ssm_lam_im, ssm_log_dt,
           ssm_b_re, ssm_b_im, ssm_c_re, ssm_c_im, ssm_d, w_glu, w_proj_attn, w_proj_ssm, w_out, ln_mix_g, ln_mix_b,
           w_peer_q, peer_sub_keys, peer_u, peer_v, ln_ffn_g, ln_ffn_b):
    batch, seq, _ = x.shape
    n = batch * seq
    x2 = x.reshape(n, D_MODEL)
    row = lambda v: v.reshape(1, -1).astype(F32)
    l = 0

    wi = w_in[l]
    w_main = jnp.concatenate([wi[:, OFF_SSM:OFF_KR], wi[:, OFF_GA:IN_WIDTH]], axis=1).astype(BF16)
    w_krc = wi[:, OFF_KR:OFF_GA]
    w_kr = jnp.concatenate([w_krc, _rot_half_cols(w_krc)], axis=1).astype(BF16)
    wq = w_uq[l].reshape(Q_LORA, N_HEADS, QK_HEAD)
    wq_ext = jnp.concatenate([wq, _rot_half_cols(wq[..., QK_NOPE:])], axis=-1)
    wq_ext = wq_ext.transpose(1, 0, 2).astype(BF16)
    wkv = w_ukv[l].reshape(KV_LORA, N_HEADS, QK_NOPE + V_HEAD).transpose(1, 0, 2).astype(BF16)

    cs = _rope_table(seq)
    u, z, kr = _inproj(x2, row(ln_in_g), row(ln_in_b), w_main, w_kr)

    q, k, v = _qkv(z, kr, cs, row(q_norm_g[l]), row(kv_norm_g[l]), wq_ext, wkv, batch, seq)
    attn = _attention(q, k, v).reshape(n, N_HEADS * V_HEAD)

    w, ws, wx, a16 = _ssm_wgen(ssm_lam_re[l], ssm_lam_im[l], ssm_log_dt[l], ssm_b_re[l], ssm_b_im[l],
                               ssm_c_re[l], ssm_c_im[l])
    g = _ssm_branch(u, batch, seq, w, ws, wx, a16, row(ssm_d[l]))

    merged = _merge(attn, g, z, w_glu[l].astype(BF16), w_proj_attn[l].astype(BF16), w_proj_ssm[l].astype(BF16))
    h2, h2t = _outproj(merged, x2, row(ln_in_g), row(ln_in_b), row(ln_mix_g[l]), row(ln_mix_b[l]),
                       w_out[l].astype(BF16))

    wcomb = _peer_wcomb(peer_sub_keys[l], w_peer_q[l].astype(BF16))
    s3 = _peer_scores(wcomb, h2t)
    c, e1, q, e2 = _peer_thresh(s3)
    out = _peer_dense(h2, h2t, q, e2, c, e1, peer_u[l].astype(BF16), peer_v[l].T.astype(BF16),
                      row(ln_ffn_g[l]), row(ln_ffn_b[l]))
    return out.reshape(batch, seq, D_MODEL)
```

```python
import functools
import math

import jax
import jax.numpy as jnp
from jax import lax
from jax.experimental import pallas as pl
from jax.experimental.pallas import tpu as pltpu

F32 = jnp.float32
BF16 = jnp.bfloat16

D_MODEL = 2048
DEPTH = 1
CHUNK = 64
SSM_WIDTH = 1024
SSM_GROUP = 16
SSM_GROUPS = SSM_WIDTH // SSM_GROUP
SSM_STATE = 64
N_HEADS = 16
Q_LORA = 512
KV_LORA = 512
QK_NOPE = 128
QK_ROPE = 64
V_HEAD = 128
QK_HEAD = QK_NOPE + QK_ROPE
ROPE_THETA = 10000.0
PEER_HEADS = 8
N_KEYS = 128
N_EXPERTS = N_KEYS * N_KEYS
PEER_KEY_HALF = 128
PEER_TOPK = 16
LN_EPS = 1e-5
RMS_EPS = 1e-6
DEEPNORM_ALPHA = (2.0 * DEPTH) ** 0.25
OFF_SSM = 0
OFF_Q = OFF_SSM + SSM_WIDTH
OFF_KV = OFF_Q + Q_LORA
OFF_KR = OFF_KV + KV_LORA
OFF_GA = OFF_KR + QK_ROPE
OFF_GS = OFF_GA + D_MODEL
IN_WIDTH = OFF_GS + D_MODEL

LANES = 128
VMEM_LIMIT_BYTES = 56 * 1024 * 1024

SSM_CHUNK = 16
SSM_TILE_GROUPS = LANES // SSM_GROUP
SSM_TILES = SSM_WIDTH // LANES
SSM_TILE_STATE = SSM_TILE_GROUPS * SSM_STATE
NEG_BIG = -1e30


def _params(semantics):
    return pltpu.CompilerParams(dimension_semantics=semantics, vmem_limit_bytes=VMEM_LIMIT_BYTES)


def _dot(a, b):
    return jnp.dot(a, b, preferred_element_type=F32)


def _ln(x, g, b):
    mu = jnp.mean(x, axis=-1, keepdims=True)
    xc = x - mu
    var = jnp.mean(xc * xc, axis=-1, keepdims=True)
    return xc * lax.rsqrt(var + LN_EPS) * g + b


GELU_C = 0.7978845608028654
GELU_A = 0.044715


def _gelu(x):
    return 0.5 * x * (1.0 + jnp.tanh(GELU_C * (x + GELU_A * (x * x * x))))


def _gelu_x2(x):
    return x * (1.0 + jnp.tanh(x * (GELU_C + (GELU_C * GELU_A) * (x * x))))


def _sigmoid(x):
    return 1.0 / (1.0 + jnp.exp(-x))


def _rope_table_kernel(invf_ref, o_ref):
    ts = o_ref.shape[0]
    pos = (lax.broadcasted_iota(jnp.int32, (ts, LANES), 0) + pl.program_id(0) * ts).astype(F32)
    ang = pos * invf_ref[...]
    lane = lax.broadcasted_iota(jnp.int32, (ts, LANES), 1)
    o_ref[...] = jnp.where(lane < 2 * (QK_ROPE // 2), jnp.cos(ang), jnp.sin(ang))


def _rope_table(seq):
    half = QK_ROPE // 2
    invf = 1.0 / (ROPE_THETA ** (jnp.arange(half, dtype=F32) / half))
    invf4 = jnp.tile(invf, 4)[None, :]
    ts = min(seq, 1024)
    return pl.pallas_call(
        _rope_table_kernel,
        grid=(seq // ts,),
        in_specs=[pl.BlockSpec((1, LANES), lambda i: (0, 0))],
        out_specs=pl.BlockSpec((ts, LANES), lambda i: (i, 0)),
        out_shape=jax.ShapeDtypeStruct((seq, LANES), F32),
        compiler_params=_params(("parallel",)),
        name="rope_table",
    )(invf4)


def _inproj_kernel(x_ref, g_ref, b_ref, w_ref, wkr_ref, u_ref, z_ref, kr_ref, h_scr):
    j = pl.program_id(1)

    @pl.when(j == 0)
    def _():
        h = _ln(x_ref[...], g_ref[...], b_ref[...]).astype(BF16)
        h_scr[...] = h
        u_ref[...] = _dot(h, w_ref[...])
        kr_ref[...] = _dot(h, wkr_ref[...]).astype(kr_ref.dtype)

    @pl.when(j > 0)
    def _():
        z_ref[...] = _dot(h_scr[...], w_ref[...]).astype(z_ref.dtype)


def _inproj(x2, ln_g, ln_b, w_main, w_kr):
    n = x2.shape[0]
    tm = min(n, 1024)
    tn = 1024
    nj = w_main.shape[1] // tn
    return pl.pallas_call(
        _inproj_kernel,
        grid=(n // tm, nj),
        in_specs=[
            pl.BlockSpec((tm, D_MODEL), lambda i, j: (i, 0)),
            pl.BlockSpec((1, D_MODEL), lambda i, j: (0, 0)),
            pl.BlockSpec((1, D_MODEL), lambda i, j: (0, 0)),
            pl.BlockSpec((D_MODEL, tn), lambda i, j: (0, j)),
            pl.BlockSpec((D_MODEL, LANES), lambda i, j: (0, 0)),
        ],
        out_specs=[
            pl.BlockSpec((tm, tn), lambda i, j: (i, 0)),
            pl.BlockSpec((tm, tn), lambda i, j: (i, jnp.maximum(j - 1, 0))),
            pl.BlockSpec((tm, LANES), lambda i, j: (i, 0)),
        ],
        out_shape=[
            jax.ShapeDtypeStruct((n, tn), F32),
            jax.ShapeDtypeStruct((n, (nj - 1) * tn), BF16),
            jax.ShapeDtypeStruct((n, LANES), BF16),
        ],
        scratch_shapes=[pltpu.VMEM((tm, D_MODEL), BF16)],
        compiler_params=_params(("parallel", "arbitrary")),
        name="ln_inproj",
    )(x2, ln_g, ln_b, w_main, w_kr)


def _rope_mix(t2, cs):
    t = t2 * cs
    return t + pltpu.roll(t, QK_ROPE, 1)


def _qkv_kernel(cq_ref, ckv_ref, kr_ref, cs_ref, gq_ref, gkv_ref, wq_ref, wkv_ref,
                q_ref, k_ref, v_ref, cqn_scr, ckvn_scr, kro_scr):
    cs = cs_ref[...]

    @pl.when(pl.program_id(1) == 0)
    def _():
        def rms(c, g):
            c = c.astype(F32)
            return (c * lax.rsqrt(jnp.mean(c * c, axis=-1, keepdims=True) + RMS_EPS) * g).astype(BF16)

        cqn_scr[...] = rms(cq_ref[...], gq_ref[...])
        ckvn_scr[...] = rms(ckv_ref[...], gkv_ref[...])
        kro_scr[...] = _rope_mix(kr_ref[...].astype(F32), cs).astype(BF16)

    scale = math.log2(math.e) / math.sqrt(QK_HEAD)
    hp = wq_ref.shape[0]
    wide = 2 * LANES
    qe2 = _dot(cqn_scr[...], jnp.concatenate([wq_ref[j] for j in range(hp)], axis=1))
    kv2 = _dot(ckvn_scr[...], jnp.concatenate([wkv_ref[j] for j in range(hp)], axis=1))
    for j in range(hp):
        qe = qe2[:, j * wide:(j + 1) * wide]
        qr = _rope_mix(qe[:, QK_NOPE:], cs)
        q_ref[0, j, :, :QK_NOPE] = (qe[:, :QK_NOPE] * scale).astype(BF16)
        q_ref[0, j, :, QK_NOPE:] = (qr[:, :QK_ROPE] * scale).astype(BF16)
        kv = kv2[:, j * wide:(j + 1) * wide]
        k_ref[0, j, :, :QK_NOPE] = kv[:, :QK_NOPE].astype(BF16)
        k_ref[0, j, :, QK_NOPE:] = kro_scr[:, :QK_ROPE]
        v_ref[0, j] = kv[:, QK_NOPE:].astype(BF16)


def _qkv(z, kr, cs, gq, gkv, wq_ext, wkv, batch, seq):
    n = z.shape[0]
    tm = min(seq, 2048)
    nsb = seq // tm
    hp = 2
    return pl.pallas_call(
        _qkv_kernel,
        grid=(n // tm, N_HEADS // hp),
        in_specs=[
            pl.BlockSpec((tm, Q_LORA), lambda i, h: (i, 0)),
            pl.BlockSpec((tm, KV_LORA), lambda i, h: (i, 1)),
            pl.BlockSpec((tm, LANES), lambda i, h: (i, 0)),
            pl.BlockSpec((tm, LANES), lambda i, h: (i % nsb, 0)),
            pl.BlockSpec((1, Q_LORA), lambda i, h: (0, 0)),
            pl.BlockSpec((1, KV_LORA), lambda i, h: (0, 0)),
            pl.BlockSpec((hp, Q_LORA, 2 * LANES), lambda i, h: (h, 0, 0)),
            pl.BlockSpec((hp, KV_LORA, 2 * LANES), lambda i, h: (h, 0, 0)),
        ],
        out_specs=[
            pl.BlockSpec((1, hp, tm, QK_HEAD), lambda i, h: (i // nsb, h, i % nsb, 0)),
            pl.BlockSpec((1, hp, tm, QK_HEAD), lambda i, h: (i // nsb, h, i % nsb, 0)),
            pl.BlockSpec((1, hp, tm, V_HEAD), lambda i, h: (i // nsb, h, i % nsb, 0)),
        ],
        out_shape=[
            jax.ShapeDtypeStruct((batch, N_HEADS, seq, QK_HEAD), BF16),
            jax.ShapeDtypeStruct((batch, N_HEADS, seq, QK_HEAD), BF16),
            jax.ShapeDtypeStruct((batch, N_HEADS, seq, V_HEAD), BF16),
        ],
        scratch_shapes=[
            pltpu.VMEM((tm, Q_LORA), BF16),
            pltpu.VMEM((tm, KV_LORA), BF16),
            pltpu.VMEM((tm, LANES), BF16),
        ],
        compiler_params=_params(("parallel", "arbitrary")),
        name="qkv_rope",
    )(z, z, kr, cs, gq, gkv, wq_ext, wkv)


ATT_ROWS = 512
ATT_CHAINS = 8
ATT_BK_MAIN = 1024


def _attn_kernel(q_ref, k_ref, v_ref, o_ref, m_scr, acc_scr, *, rows, chains, bkm):
    qi = pl.program_id(2)
    m_scr[...] = jnp.full(m_scr.shape, -jnp.inf, F32)
    acc_scr[...] = jnp.zeros(acc_scr.shape, F32)

    def load_kv(j, bk):
        start = pl.multiple_of(j * bk, bk)
        kj = k_ref[0, 0, pl.ds(start, bk), :]
        vj = jnp.concatenate([v_ref[0, 0, pl.ds(start, bk), :], jnp.ones((bk, LANES), BF16)], axis=1)
        return kj, vj

    def block(c, kj, vj, masked):
        q = q_ref[0, 0, c * rows:(c + 1) * rows, :]
        s = lax.dot_general(q, kj, (((1,), (1,)), ((), ())), preferred_element_type=F32)
        if masked:
            rc = lax.broadcasted_iota(jnp.int32, s.shape, 0) // CHUNK
            cc = lax.broadcasted_iota(jnp.int32, s.shape, 1) // CHUNK
            s = jnp.where(cc <= rc, s, NEG_BIG)
        m_prev = m_scr[c]
        m_new = jnp.maximum(m_prev, jnp.max(s, axis=-1, keepdims=True))
        alpha = jnp.exp2(m_prev - m_new)
        p = jnp.exp2(s - jnp.tile(m_new, (1, s.shape[1] // LANES)))
        acc_scr[c] = jnp.tile(alpha, (1, 2)) * acc_scr[c] + _dot(p.astype(BF16), vj)
        m_scr[c] = m_new

    def body(j, carry):
        kj, vj = load_kv(j, bkm)
        for c in range(chains):
            block(c, kj, vj, False)
        return carry

    lax.fori_loop(0, qi * (chains * rows // bkm), body, 0)
    for jj in range(chains):
        kj, vj = load_kv(qi * chains + jj, rows)
        for c in range(jj, chains):
            block(c, kj, vj, c == jj)
    for c in range(chains):
        a = acc_scr[c]
        o_ref[0, c * rows:(c + 1) * rows, :] = (a[:, :V_HEAD] / a[:, V_HEAD:]).astype(o_ref.dtype)


def _attention(q, k, v):
    batch, _, seq, _ = q.shape
    rows = min(seq, ATT_ROWS)
    chains = min(ATT_CHAINS, seq // rows)
    bq = rows * chains
    bkm = min(ATT_BK_MAIN, bq)
    return pl.pallas_call(
        functools.partial(_attn_kernel, rows=rows, chains=chains, bkm=bkm),
        grid=(batch, N_HEADS, seq // bq),
        in_specs=[
            pl.BlockSpec((1, 1, bq, QK_HEAD), lambda b, h, i: (b, h, i, 0)),
            pl.BlockSpec((1, 1, seq, QK_HEAD), lambda b, h, i: (b, h, 0, 0)),
            pl.BlockSpec((1, 1, seq, V_HEAD), lambda b, h, i: (b, h, 0, 0)),
        ],
        out_specs=pl.BlockSpec((1, bq, V_HEAD), lambda b, h, i: (b, i, h)),
        out_shape=jax.ShapeDtypeStruct((batch, seq, N_HEADS * V_HEAD), BF16),
        scratch_shapes=[
            pltpu.VMEM((chains, rows, LANES), F32),
            pltpu.VMEM((chains, rows, 2 * V_HEAD), F32),
        ],
        compiler_params=_params(("parallel", "parallel", "arbitrary")),
        name="flash_attn",
    )(q, k, v)


def _cpow(lr, li, ldt, k):
    dt = jnp.exp(ldt)
    mag = jnp.exp(lr * dt * k)
    ang = li * dt * k
    return mag * jnp.cos(ang), mag * jnp.sin(ang)


def _ssm_wgen_kernel(lr_r, li_r, ldt_r, bre_ref, bim_ref, cre_ref, cim_ref,
                     w_ref, ws_ref, wx_ref, a16_ref):
    hp = lax.Precision.HIGHEST
    ns = SSM_TILE_STATE
    lr, li, ldt = lr_r[0], li_r[0], ldt_r[0]
    a_re, a_im = _cpow(lr, li, ldt, 1.0)
    den = lr * lr + li * li
    nr = a_re - 1.0
    ni = a_im
    cf_re = (nr * lr + ni * li) / den
    cf_im = (ni * lr - nr * li) / den
    bre, bim = bre_ref[0], bim_ref[0]
    cre, cim = cre_ref[0], cim_ref[0]
    w_ref[...] = jnp.zeros(w_ref.shape, w_ref.dtype)
    for k in range(SSM_CHUNK):
        p_re, p_im = _cpow(lr, li, ldt, float(k))
        m_re = cf_re * p_re - cf_im * p_im
        m_im = cf_re * p_im + cf_im * p_re
        bm_re = bre * m_re - bim * m_im
        bm_im = bre * m_im + bim * m_re
        bd = (jnp.dot(bm_re, cre, precision=hp, preferred_element_type=F32)
              - jnp.dot(bm_im, cim, precision=hp, preferred_element_type=F32)).astype(BF16)
        for j in range(SSM_CHUNK - k):
            i = j + k
            w_ref[0, j * LANES:(j + 1) * LANES, i * LANES:(i + 1) * LANES] = bd
        jj = SSM_CHUNK - 1 - k
        ws_ref[0, jj * LANES:(jj + 1) * LANES, :ns] = bm_re.astype(BF16)
        ws_ref[0, jj * LANES:(jj + 1) * LANES, ns:] = bm_im.astype(BF16)
    cre_t, cim_t = cre.T, cim.T
    for i in range(SSM_CHUNK):
        p_re, p_im = _cpow(lr, li, ldt, float(i + 1))
        wx_ref[0, :ns, i * LANES:(i + 1) * LANES] = (p_re * cre_t - p_im * cim_t).T.astype(BF16)
        wx_ref[0, ns:, i * LANES:(i + 1) * LANES] = (-(p_re * cim_t + p_im * cre_t)).T.astype(BF16)
    q_re, q_im = _cpow(lr, li, ldt, float(SSM_CHUNK))
    a16_ref[0, :, :ns] = q_re
    a16_ref[0, :, ns:] = q_im


def _ssm_wgen(lam_re, lam_im, log_dt, b_re, b_im, c_re, c_im):
    nt, tg, ns = SSM_TILES, SSM_TILE_GROUPS, SSM_TILE_STATE
    eye = jnp.eye(tg, dtype=F32)
    bbd = lambda b: jnp.einsum("cgph,gk->cghkp", b.reshape(nt, tg, SSM_STATE, SSM_GROUP), eye).reshape(nt, LANES, ns)
    cbd = lambda c: jnp.einsum("cghp,gk->cgpkh", c.reshape(nt, tg, SSM_GROUP, SSM_STATE), eye).reshape(nt, ns, LANES)
    ldt = jnp.repeat(log_dt, SSM_STATE)
    rows = [a.reshape(nt, 1, ns) for a in (lam_re, lam_im, ldt)]
    kc = SSM_CHUNK * LANES
    row_spec = pl.BlockSpec((1, 1, ns), lambda c: (c, 0, 0))
    return pl.pallas_call(
        _ssm_wgen_kernel,
        grid=(nt,),
        in_specs=[row_spec] * 3 + [
            pl.BlockSpec((1, LANES, ns), lambda c: (c, 0, 0)),
            pl.BlockSpec((1, LANES, ns), lambda c: (c, 0, 0)),
            pl.BlockSpec((1, ns, LANES), lambda c: (c, 0, 0)),
            pl.BlockSpec((1, ns, LANES), lambda c: (c, 0, 0)),
        ],
        out_specs=[
            pl.BlockSpec((1, kc, kc), lambda c: (c, 0, 0)),
            pl.BlockSpec((1, kc, 2 * ns), lambda c: (c, 0, 0)),
            pl.BlockSpec((1, 2 * ns, kc), lambda c: (c, 0, 0)),
            pl.BlockSpec((1, 1, 2 * ns), lambda c: (c, 0, 0)),
        ],
        out_shape=[
            jax.ShapeDtypeStruct((nt, kc, kc), BF16),
            jax.ShapeDtypeStruct((nt, kc, 2 * ns), BF16),
            jax.ShapeDtypeStruct((nt, 2 * ns, kc), BF16),
            jax.ShapeDtypeStruct((nt, 1, 2 * ns), F32),
        ],
        compiler_params=_params(("parallel",)),
        name="ssm_wgen",
    )(*rows, bbd(b_re), bbd(b_im), cbd(c_re), cbd(c_im))


def _chunk_rows(u_ref, tb):
    return [u_ref[pl.ds(j, tb, stride=SSM_CHUNK), :] for j in range(SSM_CHUNK)]


def _ssm_sum_kernel(u_ref, ws_ref, s_ref, *, tb):
    u16 = jnp.concatenate(_chunk_rows(u_ref, tb), axis=1).astype(BF16)
    s_ref[0] = _dot(u16, ws_ref[0])


def _ssm_scan_kernel(s_ref, a16_ref, x_ref):
    ns = SSM_TILE_STATE
    a_re = a16_ref[0, :, :ns]
    a_im = a16_ref[0, :, ns:]
    nrow = s_ref.shape[1]

    def body(r, carry):
        xr, xi = carry
        x_ref[0, pl.ds(r, 1), :ns] = xr
        x_ref[0, pl.ds(r, 1), ns:] = xi
        sr = s_ref[0, pl.ds(r, 1), :ns]
        si = s_ref[0, pl.ds(r, 1), ns:]
        return a_re * xr - a_im * xi + sr, a_re * xi + a_im * xr + si

    zero = jnp.zeros((1, ns), F32)
    lax.fori_loop(0, nrow, body, (zero, zero), unroll=8)


def _ssm_out_kernel(u_ref, x_ref, w_ref, wx_ref, d_ref, g_ref, *, tb):
    us = _chunk_rows(u_ref, tb)
    u16 = jnp.concatenate(us, axis=1).astype(BF16)
    y16 = _dot(u16, w_ref[0]) + _dot(x_ref[0].astype(BF16), wx_ref[0])
    d = d_ref[...]
    for i in range(SSM_CHUNK):
        y = y16[:, i * LANES:(i + 1) * LANES] + d * us[i]
        g_ref[pl.ds(i, tb, stride=SSM_CHUNK), :] = _gelu(y)


def _ssm_branch(u, batch, seq, w, ws, wx, a16, d_skip):
    n = u.shape[0]
    nt, ns = SSM_TILES, SSM_TILE_STATE
    kc = SSM_CHUNK * LANES
    nb = n // SSM_CHUNK
    nbs = seq // SSM_CHUNK
    tb = min(nbs, 256)
    tok = tb * SSM_CHUNK
    s = pl.pallas_call(
        functools.partial(_ssm_sum_kernel, tb=tb),
        grid=(nt, nb // tb),
        in_specs=[
            pl.BlockSpec((tok, LANES), lambda c, r: (r, c)),
            pl.BlockSpec((1, kc, 2 * ns), lambda c, r: (c, 0, 0)),
        ],
        out_specs=pl.BlockSpec((1, tb, 2 * ns), lambda c, r: (c, r, 0)),
        out_shape=jax.ShapeDtypeStruct((nt, nb, 2 * ns), F32),
        compiler_params=_params(("parallel", "parallel")),
        name="ssm_chunk_sum",
    )(u, ws)
    x = pl.pallas_call(
        _ssm_scan_kernel,
        grid=(nt, batch),
        in_specs=[
            pl.BlockSpec((1, nbs, 2 * ns), lambda c, b: (c, b, 0)),
            pl.BlockSpec((1, 1, 2 * ns), lambda c, b: (c, 0, 0)),
        ],
        out_specs=pl.BlockSpec((1, nbs, 2 * ns), lambda c, b: (c, b, 0)),
        out_shape=jax.ShapeDtypeStruct((nt, nb, 2 * ns), F32),
        compiler_params=_params(("parallel", "parallel")),
        name="ssm_state_scan",
    )(s, a16)
    return pl.pallas_call(
        functools.partial(_ssm_out_kernel, tb=tb),
        grid=(nt, nb // tb),
        in_specs=[
            pl.BlockSpec((tok, LANES), lambda c, r: (r, c)),
            pl.BlockSpec((1, tb, 2 * ns), lambda c, r: (c, r, 0)),
            pl.BlockSpec((1, kc, kc), lambda c, r: (c, 0, 0)),
            pl.BlockSpec((1, 2 * ns, kc), lambda c, r: (c, 0, 0)),
            pl.BlockSpec((1, LANES), lambda c, r: (0, c)),
        ],
        out_specs=pl.BlockSpec((tok, LANES), lambda c, r: (r, c)),
        out_shape=jax.ShapeDtypeStruct((n, SSM_WIDTH), F32),
        compiler_params=_params(("parallel", "parallel")),
        name="ssm_chunk_out",
    )(u, x, w, wx, d_skip)


def _merge_kernel(attn_ref, g_ref, ga_ref, gs_ref, wglu_ref, wpa_ref, wps_ref, o_ref, ssm_scr):
    @pl.when(pl.program_id(1) == 0)
    def _():
        g = g_ref[...]
        ssm_scr[...] = (g * _sigmoid(_dot(g.astype(BF16), wglu_ref[...]))).astype(BF16)

    pa = _dot(attn_ref[...], wpa_ref[...])
    ps = _dot(ssm_scr[...], wps_ref[...])
    o_ref[...] = (_sigmoid(ga_ref[...].astype(F32)) * pa + _sigmoid(gs_ref[...].astype(F32)) * ps).astype(o_ref.dtype)


def _merge(attn, g, z, w_glu, w_pa, w_ps):
    n = attn.shape[0]
    tm = min(n, 1024)
    tn = 1024
    nj = D_MODEL // tn
    return pl.pallas_call(
        _merge_kernel,
        grid=(n // tm, nj),
        in_specs=[
            pl.BlockSpec((tm, N_HEADS * V_HEAD), lambda i, j: (i, 0)),
            pl.BlockSpec((tm, SSM_WIDTH), lambda i, j: (i, 0)),
            pl.BlockSpec((tm, tn), lambda i, j: (i, 1 + j)),
            pl.BlockSpec((tm, tn), lambda i, j: (i, 1 + nj + j)),
            pl.BlockSpec((SSM_WIDTH, SSM_WIDTH), lambda i, j: (0, 0)),
            pl.BlockSpec((N_HEADS * V_HEAD, tn), lambda i, j: (0, j)),
            pl.BlockSpec((SSM_WIDTH, tn), lambda i, j: (0, j)),
        ],
        out_specs=pl.BlockSpec((tm, tn), lambda i, j: (i, j)),
        out_shape=jax.ShapeDtypeStruct((n, D_MODEL), BF16),
        scratch_shapes=[pltpu.VMEM((tm, SSM_WIDTH), BF16)],
        compiler_params=_params(("parallel", "arbitrary")),
        name="glu_merge",
    )(attn, g, z, z, w_glu, w_pa, w_ps)


def _outproj_kernel(m_ref, x_ref, g1_ref, b1_ref, g2_ref, b2_ref, w_ref, h2_ref, h2t_ref):
    y = _dot(m_ref[...], w_ref[...])
    h = _ln(x_ref[...], g1_ref[...], b1_ref[...])
    h2 = _ln(DEEPNORM_ALPHA * h + y, g2_ref[...], b2_ref[...])
    h2_ref[...] = h2.astype(h2_ref.dtype)
    h2t_ref[...] = h2.T.astype(h2t_ref.dtype)


def _outproj(merged, x2, g1, b1, g2, b2, w_out):
    n = merged.shape[0]
    tm = min(n, 512)
    vec = pl.BlockSpec((1, D_MODEL), lambda i: (0, 0))
    return pl.pallas_call(
        _outproj_kernel,
        grid=(n // tm,),
        in_specs=[
            pl.BlockSpec((tm, D_MODEL), lambda i: (i, 0)),
            pl.BlockSpec((tm, D_MODEL), lambda i: (i, 0)),
            vec, vec, vec, vec,
            pl.BlockSpec((D_MODEL, D_MODEL), lambda i: (0, 0)),
        ],
        out_specs=[
            pl.BlockSpec((tm, D_MODEL), lambda i: (i, 0)),
            pl.BlockSpec((D_MODEL, tm), lambda i: (0, i)),
        ],
        out_shape=[
            jax.ShapeDtypeStruct((n, D_MODEL), BF16),
            jax.ShapeDtypeStruct((D_MODEL, n), BF16),
        ],
        compiler_params=_params(("parallel",)),
        name="outproj_ln",
    )(merged, x2, g1, b1, g2, b2, w_out)


def _peer_wcomb_kernel(keys_ref, wpq_ref, o_ref):
    o_ref[...] = lax.dot_general(keys_ref[0].astype(BF16), wpq_ref[...], (((1,), (1,)), ((), ())),
                                 preferred_element_type=F32).astype(o_ref.dtype)


def _peer_wcomb(sub_keys, w_pq):
    nhc = PEER_HEADS * 2
    keys = sub_keys.reshape(nhc, N_KEYS, PEER_KEY_HALF)
    return pl.pallas_call(
        _peer_wcomb_kernel,
        grid=(nhc,),
        in_specs=[
            pl.BlockSpec((1, N_KEYS, PEER_KEY_HALF), lambda i: (i, 0, 0)),
            pl.BlockSpec((D_MODEL, PEER_KEY_HALF), lambda i: (0, i)),
        ],
        out_specs=pl.BlockSpec((N_KEYS, D_MODEL), lambda i: (i, 0)),
        out_shape=jax.ShapeDtypeStruct((nhc * N_KEYS, D_MODEL), BF16),
        compiler_params=_params(("parallel",)),
        name="peer_wcomb",
    )(keys, w_pq)


PEER_TOK = 1024
PEER_THRESH_TOK = 2048


def _peer_scores_kernel(w_ref, xt_ref, s3_ref):
    rows = w_ref.shape[0] // 2
    for g in range(2):
        st = _dot(w_ref[g * rows:(g + 1) * rows, :], xt_ref[...])
        for c in range(s3_ref.shape[1]):
            s3_ref[g * rows:(g + 1) * rows, c, :] = st[:, c * LANES:(c + 1) * LANES]


def _peer_scores(wcomb, h2t):
    n = h2t.shape[1]
    tl = min(n, PEER_TOK)
    rows = wcomb.shape[0]
    return pl.pallas_call(
        _peer_scores_kernel,
        grid=(n // tl,),
        in_specs=[
            pl.BlockSpec((rows, D_MODEL), lambda i: (0, 0)),
            pl.BlockSpec((D_MODEL, tl), lambda i: (0, i)),
        ],
        out_specs=pl.BlockSpec((rows, tl // LANES, LANES), lambda i: (0, i, 0)),
        out_shape=jax.ShapeDtypeStruct((rows, n // LANES, LANES), F32),
        compiler_params=_params(("parallel",)),
        name="peer_scores",
    )(wcomb, h2t)


PEER_LIST = PEER_TOPK + 1
PEER_SUB = 4 * N_KEYS


def _insert_sorted(t, v):
    out = []
    for r in range(len(t)):
        out.append(jnp.maximum(t[r], v))
        v = jnp.minimum(t[r], v)
    return out


PEER_ROW = 4


def _count_above(vals, x, strict):
    above = (lambda v: v > x) if strict else (lambda v: v >= x)
    pick = lambda m, hi, lo: jnp.where(m, hi, lo)
    m8 = above(vals[7])
    m4 = above(pick(m8, vals[11], vals[3]))
    m2 = above(pick(m8, pick(m4, vals[13], vals[9]), pick(m4, vals[5], vals[1])))
    hi = pick(m4, pick(m2, vals[14], vals[12]), pick(m2, vals[10], vals[8]))
    lo = pick(m4, pick(m2, vals[6], vals[4]), pick(m2, vals[2], vals[0]))
    m1 = above(pick(m8, hi, lo))
    cnt = (pick(m8, 8.0, 0.0) + pick(m4, 4.0, 0.0)) + (pick(m2, 2.0, 0.0) + pick(m1, 1.0, 0.0))
    return pick(above(vals[15]), 16.0, cnt)


def _peer_thresh_kernel(s_ref, c_ref, e1_ref, q_ref, e2_ref, q_scr, e2_scr):
    shape = s_ref.shape[1:]
    neg = jnp.full(shape, -jnp.inf, F32)

    def top_list(base):
        def body(n, t):
            return tuple(_insert_sorted(list(t), s_ref[base + n]))
        return lax.fori_loop(0, N_KEYS, body, (neg,) * PEER_LIST, unroll=4)

    a = top_list(0)
    b = top_list(N_KEYS)
    t = [neg] * PEER_LIST
    for i in range(1, PEER_LIST + 1):
        for j in range(1, PEER_LIST // i + 1):
            t = _insert_sorted(t, a[i - 1] + b[j - 1])
    tau = 0.5 * (t[PEER_TOPK - 1] + t[PEER_TOPK])
    top = a[0] + b[0]
    z = jnp.exp(t[0] - top)
    for r in range(1, PEER_TOPK):
        z = z + jnp.exp(t[r] - top)
    cv = b[0] + jnp.log(2.0 * z)
    a16 = a[:PEER_TOPK]

    def write(n, carry):
        s1 = s_ref[n]
        s2 = s_ref[N_KEYS + n]
        c = 1.0 + _count_above(a16, s1, True)
        e1 = jnp.exp(s1 - a16[0])
        for gq in range(shape[0] // PEER_ROW):
            c_ref[n, gq] = c[gq * PEER_ROW:(gq + 1) * PEER_ROW, :]
            e1_ref[n, gq] = e1[gq * PEER_ROW:(gq + 1) * PEER_ROW, :]
        row0 = pl.multiple_of(n * nsub, nsub)
        q_scr[pl.ds(row0, nsub), :] = _count_above(a16, tau - s2, False)
        e2_scr[pl.ds(row0, nsub), :] = jnp.exp(s2 - cv)
        return carry

    nsub = shape[0]
    lax.fori_loop(0, N_KEYS, write, 0, unroll=2)
    for c in range(nsub):
        q_ref[:, c * LANES:(c + 1) * LANES] = q_scr[pl.ds(c, N_KEYS, stride=nsub), :].astype(q_ref.dtype)
        e2_ref[:, c * LANES:(c + 1) * LANES] = e2_scr[pl.ds(c, N_KEYS, stride=nsub), :].astype(e2_ref.dtype)


def _peer_thresh(s3):
    rows, nl, _ = s3.shape
    n = nl * LANES
    nb = min(PEER_THRESH_TOK // LANES, nl)
    half = rows // 2
    ng = nb // PEER_ROW
    row_spec = pl.BlockSpec((N_KEYS, ng, PEER_ROW, LANES), lambda i, h: (h, i, 0, 0))
    row_shape = jax.ShapeDtypeStruct((half, nl // PEER_ROW, PEER_ROW, LANES), F32)
    tile_spec = pl.BlockSpec((N_KEYS, nb * LANES), lambda i, h: (h, i))
    tile_shape = jax.ShapeDtypeStruct((half, n), BF16)
    c, e1, q, e2 = pl.pallas_call(
        _peer_thresh_kernel,
        grid=(nl // nb, PEER_HEADS),
        in_specs=[pl.BlockSpec((2 * N_KEYS, nb, LANES), lambda i, h: (h, i, 0))],
        out_specs=[row_spec, row_spec, tile_spec, tile_spec],
        out_shape=[row_shape, row_shape, tile_shape, tile_shape],
        scratch_shapes=[pltpu.VMEM((N_KEYS * nb, LANES), F32), pltpu.VMEM((N_KEYS * nb, LANES), F32)],
        compiler_params=_params(("parallel", "parallel")),
        name="peer_thresh",
    )(s3)
    shape5 = (PEER_HEADS, N_KEYS, nl // PEER_ROW, PEER_ROW, LANES)
    tiles = lambda v: v.reshape(PEER_HEADS, N_KEYS, n)
    return c.reshape(shape5), e1.reshape(shape5), tiles(q), tiles(e2)


def _peer_dense_kernel(xt_ref, u_ref, vt_ref, q_ref, e2_ref, c_ref, e1_ref, h2_ref, g_ref, b_ref,
                       o_ref, acc_scr, a_scr, *, sub):
    e = pl.program_id(1)
    eb = u_ref.shape[0]

    @pl.when(e == 0)
    def _():
        acc_scr[...] = jnp.zeros(acc_scr.shape, F32)

    def row(ref, h, il):
        return jnp.concatenate([ref[h, il, 0, c:c + 1, :] for c in range(ref.shape[3])], axis=1).astype(BF16)

    zero = jnp.zeros((), BF16)

    xt = xt_ref[...]
    nsub = eb // sub
    score = lambda s: _dot(u_ref[s * sub:(s + 1) * sub, :], xt)
    st_next = score(0)
    for s in range(nsub):
        st, st_next = st_next, (score(s + 1) if s + 1 < nsub else None)
        for ii in range(sub // N_KEYS):
            il = s * (sub // N_KEYS) + ii
            gate = None
            for h in range(PEER_HEADS):
                term = row(e1_ref, h, il) * jnp.where(q_ref[h] >= row(c_ref, h, il), e2_ref[h], zero)
                gate = term if gate is None else gate + term
            a = _gelu_x2(st[ii * N_KEYS:(ii + 1) * N_KEYS, :]).astype(BF16) * gate
            a_scr[s * sub + ii * N_KEYS:s * sub + (ii + 1) * N_KEYS, :] = a
        acc_scr[...] += _dot(vt_ref[:, s * sub:(s + 1) * sub], a_scr[s * sub:(s + 1) * sub, :])

    @pl.when(e == pl.num_programs(1) - 1)
    def _():
        y = acc_scr[...].T
        o_ref[...] = _ln(DEEPNORM_ALPHA * h2_ref[...].astype(F32) + y, g_ref[...], b_ref[...])


def _peer_dense(h2, h2t, q, e2, c, e1, u_tab, vt_tab, ln_g, ln_b):
    n = h2.shape[0]
    t = min(n, PEER_ROW * LANES)
    eb = 8 * N_KEYS
    sub = PEER_SUB
    vec = pl.BlockSpec((1, D_MODEL), lambda i, e: (0, 0))
    row_spec = pl.BlockSpec((PEER_HEADS, eb // N_KEYS, 1, PEER_ROW, LANES), lambda i, e: (0, e, i, 0, 0))
    return pl.pallas_call(
        functools.partial(_peer_dense_kernel, sub=sub),
        grid=(n // t, N_EXPERTS // eb),
        in_specs=[
            pl.BlockSpec((D_MODEL, t), lambda i, e: (0, i)),
            pl.BlockSpec((eb, D_MODEL), lambda i, e: (e, 0)),
            pl.BlockSpec((D_MODEL, eb), lambda i, e: (0, e)),
            pl.BlockSpec((PEER_HEADS, N_KEYS, t), lambda i, e: (0, 0, i)),
            pl.BlockSpec((PEER_HEADS, N_KEYS, t), lambda i, e: (0, 0, i)),
            row_spec, row_spec,
            pl.BlockSpec((t, D_MODEL), lambda i, e: (i, 0)),
            vec, vec,
        ],
        out_specs=pl.BlockSpec((t, D_MODEL), lambda i, e: (i, 0)),
        out_shape=jax.ShapeDtypeStruct((n, D_MODEL), F32),
        scratch_shapes=[
            pltpu.VMEM((D_MODEL, t), F32),
            pltpu.VMEM((eb, t), BF16),
        ],
        compiler_params=_params(("parallel", "arbitrary")),
        name="peer_dense",
    )(h2t, u_tab, vt_tab, q, e2, c, e1, h2, ln_g, ln_b)


def _rot_half_cols(w):
    half = QK_ROPE // 2
    return jnp.concatenate([-w[..., half:], w[..., :half]], axis=-1)


def kernel(x, ln_in_g, ln_in_b, w_in, q_norm_g, w_uq, kv_norm_g, w_ukv, ssm_lam_re, ssm_lam_im, ssm_log_dt,
           ssm_b_re, ssm_b_im, ssm_c_re, ssm_c_im, ssm_d, w_glu, w_proj_attn, w_proj_ssm, w_out, ln_mix_g, ln_mix_b,
           w_peer_q, peer_sub_keys, peer_u, peer_v, ln_ffn_g, ln_ffn_b):
    batch, seq, _ = x.shape
    n = batch * seq
    x2 = x.reshape(n, D_MODEL)
    row = lambda v: v.reshape(1, -1).astype(F32)
    l = 0

    wi = w_in[l]
    w_main = jnp.concatenate([wi[:, OFF_SSM:OFF_KR], wi[:, OFF_GA:IN_WIDTH]], axis=1).astype(BF16)
    w_krc = wi[:, OFF_KR:OFF_GA]
    w_kr = jnp.concatenate([w_krc, _rot_half_cols(w_krc)], axis=1).astype(BF16)
    wq = w_uq[l].reshape(Q_LORA, N_HEADS, QK_HEAD)
    wq_ext = jnp.concatenate([wq, _rot_half_cols(wq[..., QK_NOPE:])], axis=-1)
    wq_ext = wq_ext.transpose(1, 0, 2).astype(BF16)
    wkv = w_ukv[l].reshape(KV_LORA, N_HEADS, QK_NOPE + V_HEAD).transpose(1, 0, 2).astype(BF16)

    cs = _rope_table(seq)
    u, z, kr = _inproj(x2, row(ln_in_g), row(ln_in_b), w_main, w_kr)

    q, k, v = _qkv(z, kr, cs, row(q_norm_g[l]), row(kv_norm_g[l]), wq_ext, wkv, batch, seq)
    attn = _attention(q, k, v).reshape(n, N_HEADS * V_HEAD)

    w, ws, wx, a16 = _ssm_wgen(ssm_lam_re[l], ssm_lam_im[l], ssm_log_dt[l], ssm_b_re[l], ssm_b_im[l],
                               ssm_c_re[l], ssm_c_im[l])
    g = _ssm_branch(u, batch, seq, w, ws, wx, a16, row(ssm_d[l]))

    merged = _merge(attn, g, z, w_glu[l].astype(BF16), w_proj_attn[l].astype(BF16), w_proj_ssm[l].astype(BF16))
    h2, h2t = _outproj(merged, x2, row(ln_in_g), row(ln_in_b), row(ln_mix_g[l]), row(ln_mix_b[l]),
                       w_out[l].astype(BF16))

    wcomb = _peer_wcomb(peer_sub_keys[l], w_peer_q[l].astype(BF16))
    s3 = _peer_scores(wcomb, h2t)
    c, e1, q, e2 = _peer_thresh(s3)
    out = _peer_dense(h2, h2t, q, e2, c, e1, peer_u[l].astype(BF16), peer_v[l].T.astype(BF16),
                      row(ln_ffn_g[l]), row(ln_ffn_b[l]))
    return out.reshape(batch, seq, D_MODEL)
```

```python
import functools
import math

import jax
import jax.numpy as jnp
from jax import lax
from jax.experimental import pallas as pl
from jax.experimental.pallas import tpu as pltpu

F32 = jnp.float32
BF16 = jnp.bfloat16

D_MODEL = 2048
DEPTH = 1
CHUNK = 64
SSM_WIDTH = 1024
SSM_GROUP = 16
SSM_GROUPS = SSM_WIDTH // SSM_GROUP
SSM_STATE = 64
N_HEADS = 16
Q_LORA = 512
KV_LORA = 512
QK_NOPE = 128
QK_ROPE = 64
V_HEAD = 128
QK_HEAD = QK_NOPE + QK_ROPE
ROPE_THETA = 10000.0
PEER_HEADS = 8
N_KEYS = 128
N_EXPERTS = N_KEYS * N_KEYS
PEER_KEY_HALF = 128
PEER_TOPK = 16
LN_EPS = 1e-5
RMS_EPS = 1e-6
DEEPNORM_ALPHA = (2.0 * DEPTH) ** 0.25
OFF_SSM = 0
OFF_Q = OFF_SSM + SSM_WIDTH
OFF_KV = OFF_Q + Q_LORA
OFF_KR = OFF_KV + KV_LORA
OFF_GA = OFF_KR + QK_ROPE
OFF_GS = OFF_GA + D_MODEL
IN_WIDTH = OFF_GS + D_MODEL

LANES = 128
VMEM_LIMIT_BYTES = 56 * 1024 * 1024

SSM_CHUNK = 16
SSM_TILE_GROUPS = LANES // SSM_GROUP
SSM_TILES = SSM_WIDTH // LANES
SSM_TILE_STATE = SSM_TILE_GROUPS * SSM_STATE
NEG_BIG = -1e30


def _params(semantics):
    return pltpu.CompilerParams(dimension_semantics=semantics, vmem_limit_bytes=VMEM_LIMIT_BYTES)


def _dot(a, b):
    return jnp.dot(a, b, preferred_element_type=F32)


def _ln(x, g, b):
    mu = jnp.mean(x, axis=-1, keepdims=True)
    xc = x - mu
    var = jnp.mean(xc * xc, axis=-1, keepdims=True)
    return xc * lax.rsqrt(var + LN_EPS) * g + b


GELU_C = 0.7978845608028654
GELU_A = 0.044715


def _gelu(x):
    return 0.5 * x * (1.0 + jnp.tanh(GELU_C * (x + GELU_A * (x * x * x))))


def _gelu_x2(x):
    return x * (1.0 + jnp.tanh(x * (GELU_C + (GELU_C * GELU_A) * (x * x))))


def _sigmoid(x):
    return 1.0 / (1.0 + jnp.exp(-x))


def _rope_table_kernel(invf_ref, o_ref):
    ts = o_ref.shape[0]
    pos = (lax.broadcasted_iota(jnp.int32, (ts, LANES), 0) + pl.program_id(0) * ts).astype(F32)
    ang = pos * invf_ref[...]
    lane = lax.broadcasted_iota(jnp.int32, (ts, LANES), 1)
    o_ref[...] = jnp.where(lane < 2 * (QK_ROPE // 2), jnp.cos(ang), jnp.sin(ang))


def _rope_table(seq):
    half = QK_ROPE // 2
    invf = 1.0 / (ROPE_THETA ** (jnp.arange(half, dtype=F32) / half))
    invf4 = jnp.tile(invf, 4)[None, :]
    ts = min(seq, 1024)
    return pl.pallas_call(
        _rope_table_kernel,
        grid=(seq // ts,),
        in_specs=[pl.BlockSpec((1, LANES), lambda i: (0, 0))],
        out_specs=pl.BlockSpec((ts, LANES), lambda i: (i, 0)),
        out_shape=jax.ShapeDtypeStruct((seq, LANES), F32),
        compiler_params=_params(("parallel",)),
        name="rope_table",
    )(invf4)


def _inproj_kernel(x_ref, g_ref, b_ref, w_ref, wkr_ref, u_ref, z_ref, kr_ref, h_scr):
    j = pl.program_id(1)

    @pl.when(j == 0)
    def _():
        h = _ln(x_ref[...], g_ref[...], b_ref[...]).astype(BF16)
        h_scr[...] = h
        u_ref[...] = _dot(h, w_ref[...])
        kr_ref[...] = _dot(h, wkr_ref[...]).astype(kr_ref.dtype)

    @pl.when(j > 0)
    def _():
        z_ref[...] = _dot(h_scr[...], w_ref[...]).astype(z_ref.dtype)


def _inproj(x2, ln_g, ln_b, w_main, w_kr):
    n = x2.shape[0]
    tm = min(n, 1024)
    tn = 1024
    nj = w_main.shape[1] // tn
    return pl.pallas_call(
        _inproj_kernel,
        grid=(n // tm, nj),
        in_specs=[
            pl.BlockSpec((tm, D_MODEL), lambda i, j: (i, 0)),
            pl.BlockSpec((1, D_MODEL), lambda i, j: (0, 0)),
            pl.BlockSpec((1, D_MODEL), lambda i, j: (0, 0)),
            pl.BlockSpec((D_MODEL, tn), lambda i, j: (0, j)),
            pl.BlockSpec((D_MODEL, LANES), lambda i, j: (0, 0)),
        ],
        out_specs=[
            pl.BlockSpec((tm, tn), lambda i, j: (i, 0)),
            pl.BlockSpec((tm, tn), lambda i, j: (i, jnp.maximum(j - 1, 0))),
            pl.BlockSpec((tm, LANES), lambda i, j: (i, 0)),
        ],
        out_shape=[
            jax.ShapeDtypeStruct((n, tn), F32),
            jax.ShapeDtypeStruct((n, (nj - 1) * tn), BF16),
            jax.ShapeDtypeStruct((n, LANES), BF16),
        ],
        scratch_shapes=[pltpu.VMEM((tm, D_MODEL), BF16)],
        compiler_params=_params(("parallel", "arbitrary")),
        name="ln_inproj",
    )(x2, ln_g, ln_b, w_main, w_kr)


def _rope_mix(t2, cs):
    t = t2 * cs
    return t + pltpu.roll(t, QK_ROPE, 1)


def _qkv_kernel(cq_ref, ckv_ref, kr_ref, cs_ref, gq_ref, gkv_ref, wq_ref, wkv_ref,
                q_ref, k_ref, v_ref, cqn_scr, ckvn_scr, kro_scr):
    cs = cs_ref[...]

    @pl.when(pl.program_id(1) == 0)
    def _():
        def rms(c, g):
            c = c.astype(F32)
            return (c * lax.rsqrt(jnp.mean(c * c, axis=-1, keepdims=True) + RMS_EPS) * g).astype(BF16)

        cqn_scr[...] = rms(cq_ref[...], gq_ref[...])
        ckvn_scr[...] = rms(ckv_ref[...], gkv_ref[...])
        kro_scr[...] = _rope_mix(kr_ref[...].astype(F32), cs).astype(BF16)

    scale = math.log2(math.e) / math.sqrt(QK_HEAD)
    hp = wq_ref.shape[0]
    wide = 2 * LANES
    qe2 = _dot(cqn_scr[...], jnp.concatenate([wq_ref[j] for j in range(hp)], axis=1))
    kv2 = _dot(ckvn_scr[...], jnp.concatenate([wkv_ref[j] for j in range(hp)], axis=1))
    for j in range(hp):
        qe = qe2[:, j * wide:(j + 1) * wide]
        qr = _rope_mix(qe[:, QK_NOPE:], cs)
        q_ref[0, j, :, :QK_NOPE] = (qe[:, :QK_NOPE] * scale).astype(BF16)
        q_ref[0, j, :, QK_NOPE:] = (qr[:, :QK_ROPE] * scale).astype(BF16)
        kv = kv2[:, j * wide:(j + 1) * wide]
        k_ref[0, j, :, :QK_NOPE] = kv[:, :QK_NOPE].astype(BF16)
        k_ref[0, j, :, QK_NOPE:] = kro_scr[:, :QK_ROPE]
        v_ref[0, j] = kv[:, QK_NOPE:].astype(BF16)


def _qkv(z, kr, cs, gq, gkv, wq_ext, wkv, batch, seq):
    n = z.shape[0]
    tm = min(seq, 2048)
    nsb = seq // tm
    hp = 2
    return pl.pallas_call(
        _qkv_kernel,
        grid=(n // tm, N_HEADS // hp),
        in_specs=[
            pl.BlockSpec((tm, Q_LORA), lambda i, h: (i, 0)),
            pl.BlockSpec((tm, KV_LORA), lambda i, h: (i, 1)),
            pl.BlockSpec((tm, LANES), lambda i, h: (i, 0)),
            pl.BlockSpec((tm, LANES), lambda i, h: (i % nsb, 0)),
            pl.BlockSpec((1, Q_LORA), lambda i, h: (0, 0)),
            pl.BlockSpec((1, KV_LORA), lambda i, h: (0, 0)),
            pl.BlockSpec((hp, Q_LORA, 2 * LANES), lambda i, h: (h, 0, 0)),
            pl.BlockSpec((hp, KV_LORA, 2 * LANES), lambda i, h: (h, 0, 0)),
        ],
        out_specs=[
            pl.BlockSpec((1, hp, tm, QK_HEAD), lambda i, h: (i // nsb, h, i % nsb, 0)),
            pl.BlockSpec((1, hp, tm, QK_HEAD), lambda i, h: (i // nsb, h, i % nsb, 0)),
            pl.BlockSpec((1, hp, tm, V_HEAD), lambda i, h: (i // nsb, h, i % nsb, 0)),
        ],
        out_shape=[
            jax.ShapeDtypeStruct((batch, N_HEADS, seq, QK_HEAD), BF16),
            jax.ShapeDtypeStruct((batch, N_HEADS, seq, QK_HEAD), BF16),
            jax.ShapeDtypeStruct((batch, N_HEADS, seq, V_HEAD), BF16),
        ],
        scratch_shapes=[
            pltpu.VMEM((tm, Q_LORA), BF16),
            pltpu.VMEM((tm, KV_LORA), BF16),
            pltpu.VMEM((tm, LANES), BF16),
        ],
        compiler_params=_params(("parallel", "arbitrary")),
        name="qkv_rope",
    )(z, z, kr, cs, gq, gkv, wq_ext, wkv)


ATT_ROWS = 512
ATT_CHAINS = 8
ATT_BK_MAIN = 1024


def _attn_kernel(q_ref, k_ref, v_ref, o_ref, m_scr, acc_scr, *, rows, chains, bkm):
    qi = pl.program_id(2)
    m_scr[...] = jnp.full(m_scr.shape, -jnp.inf, F32)
    acc_scr[...] = jnp.zeros(acc_scr.shape, F32)

    def load_kv(j, bk):
        start = pl.multiple_of(j * bk, bk)
        kj = k_ref[0, 0, pl.ds(start, bk), :]
        vj = jnp.concatenate([v_ref[0, 0, pl.ds(start, bk), :], jnp.ones((bk, LANES), BF16)], axis=1)
        return kj, vj

    def block(c, kj, vj, masked):
        q = q_ref[0, 0, c * rows:(c + 1) * rows, :]
        s = lax.dot_general(q, kj, (((1,), (1,)), ((), ())), preferred_element_type=F32)
        if masked:
            rc = lax.broadcasted_iota(jnp.int32, s.shape, 0) // CHUNK
            cc = lax.broadcasted_iota(jnp.int32, s.shape, 1) // CHUNK
            s = jnp.where(cc <= rc, s, NEG_BIG)
        m_prev = m_scr[c]
        m_new = jnp.maximum(m_prev, jnp.max(s, axis=-1, keepdims=True))
        alpha = jnp.exp2(m_prev - m_new)
        p = jnp.exp2(s - jnp.tile(m_new, (1, s.shape[1] // LANES)))
        acc_scr[c] = jnp.tile(alpha, (1, 2)) * acc_scr[c] + _dot(p.astype(BF16), vj)
        m_scr[c] = m_new

    per_step = chains * rows // bkm
    group = 2 if per_step % 2 == 0 else 1

    def body(j, carry):
        for jj in range(group):
            kj, vj = load_kv(j * group + jj, bkm)
            for c in range(chains):
                block(c, kj, vj, False)
        return carry

    lax.fori_loop(0, qi * (per_step // group), body, 0)
    for jj in range(chains):
        kj, vj = load_kv(qi * chains + jj, rows)
        for c in range(jj, chains):
            block(c, kj, vj, c == jj)
    for c in range(chains):
        a = acc_scr[c]
        o_ref[0, c * rows:(c + 1) * rows, :] = (a[:, :V_HEAD] / a[:, V_HEAD:]).astype(o_ref.dtype)


def _attention(q, k, v):
    batch, _, seq, _ = q.shape
    rows = min(seq, ATT_ROWS)
    chains = min(ATT_CHAINS, seq // rows)
    bq = rows * chains
    bkm = min(ATT_BK_MAIN, bq)
    return pl.pallas_call(
        functools.partial(_attn_kernel, rows=rows, chains=chains, bkm=bkm),
        grid=(batch, N_HEADS, seq // bq),
        in_specs=[
            pl.BlockSpec((1, 1, bq, QK_HEAD), lambda b, h, i: (b, h, i, 0)),
            pl.BlockSpec((1, 1, seq, QK_HEAD), lambda b, h, i: (b, h, 0, 0)),
            pl.BlockSpec((1, 1, seq, V_HEAD), lambda b, h, i: (b, h, 0, 0)),
        ],
        out_specs=pl.BlockSpec((1, bq, V_HEAD), lambda b, h, i: (b, i, h)),
        out_shape=jax.ShapeDtypeStruct((batch, seq, N_HEADS * V_HEAD), BF16),
        scratch_shapes=[
            pltpu.VMEM((chains, rows, LANES), F32),
            pltpu.VMEM((chains, rows, 2 * V_HEAD), F32),
        ],
        compiler_params=_params(("parallel", "parallel", "arbitrary")),
        name="flash_attn",
    )(q, k, v)


def _cpow(lr, li, ldt, k):
    dt = jnp.exp(ldt)
    mag = jnp.exp(lr * dt * k)
    ang = li * dt * k
    return mag * jnp.cos(ang), mag * jnp.sin(ang)


def _ssm_wgen_kernel(lr_r, li_r, ldt_r, bre_ref, bim_ref, cre_ref, cim_ref,
                     w_ref, ws_ref, wx_ref, a16_ref):
    hp = lax.Precision.HIGHEST
    ns = SSM_TILE_STATE
    lr, li, ldt = lr_r[0], li_r[0], ldt_r[0]
    a_re, a_im = _cpow(lr, li, ldt, 1.0)
    den = lr * lr + li * li
    nr = a_re - 1.0
    ni = a_im
    cf_re = (nr * lr + ni * li) / den
    cf_im = (ni * lr - nr * li) / den
    bre, bim = bre_ref[0], bim_ref[0]
    cre, cim = cre_ref[0], cim_ref[0]
    w_ref[...] = jnp.zeros(w_ref.shape, w_ref.dtype)
    for k in range(SSM_CHUNK):
        p_re, p_im = _cpow(lr, li, ldt, float(k))
        m_re = cf_re * p_re - cf_im * p_im
        m_im = cf_re * p_im + cf_im * p_re
        bm_re = bre * m_re - bim * m_im
        bm_im = bre * m_im + bim * m_re
        bd = (jnp.dot(bm_re, cre, precision=hp, preferred_element_type=F32)
              - jnp.dot(bm_im, cim, precision=hp, preferred_element_type=F32)).astype(BF16)
        for j in range(SSM_CHUNK - k):
            i = j + k
            w_ref[0, j * LANES:(j + 1) * LANES, i * LANES:(i + 1) * LANES] = bd
        jj = SSM_CHUNK - 1 - k
        ws_ref[0, jj * LANES:(jj + 1) * LANES, :ns] = bm_re.astype(BF16)
        ws_ref[0, jj * LANES:(jj + 1) * LANES, ns:] = bm_im.astype(BF16)
    cre_t, cim_t = cre.T, cim.T
    for i in range(SSM_CHUNK):
        p_re, p_im = _cpow(lr, li, ldt, float(i + 1))
        wx_ref[0, :ns, i * LANES:(i + 1) * LANES] = (p_re * cre_t - p_im * cim_t).T.astype(BF16)
        wx_ref[0, ns:, i * LANES:(i + 1) * LANES] = (-(p_re * cim_t + p_im * cre_t)).T.astype(BF16)
    q_re, q_im = _cpow(lr, li, ldt, float(SSM_CHUNK))
    a16_ref[0, :, :ns] = q_re
    a16_ref[0, :, ns:] = q_im


def _ssm_wgen(lam_re, lam_im, log_dt, b_re, b_im, c_re, c_im):
    nt, tg, ns = SSM_TILES, SSM_TILE_GROUPS, SSM_TILE_STATE
    eye = jnp.eye(tg, dtype=F32)
    bbd = lambda b: jnp.einsum("cgph,gk->cghkp", b.reshape(nt, tg, SSM_STATE, SSM_GROUP), eye).reshape(nt, LANES, ns)
    cbd = lambda c: jnp.einsum("cghp,gk->cgpkh", c.reshape(nt, tg, SSM_GROUP, SSM_STATE), eye).reshape(nt, ns, LANES)
    ldt = jnp.repeat(log_dt, SSM_STATE)
    rows = [a.reshape(nt, 1, ns) for a in (lam_re, lam_im, ldt)]
    kc = SSM_CHUNK * LANES
    row_spec = pl.BlockSpec((1, 1, ns), lambda c: (c, 0, 0))
    return pl.pallas_call(
        _ssm_wgen_kernel,
        grid=(nt,),
        in_specs=[row_spec] * 3 + [
            pl.BlockSpec((1, LANES, ns), lambda c: (c, 0, 0)),
            pl.BlockSpec((1, LANES, ns), lambda c: (c, 0, 0)),
            pl.BlockSpec((1, ns, LANES), lambda c: (c, 0, 0)),
            pl.BlockSpec((1, ns, LANES), lambda c: (c, 0, 0)),
        ],
        out_specs=[
            pl.BlockSpec((1, kc, kc), lambda c: (c, 0, 0)),
            pl.BlockSpec((1, kc, 2 * ns), lambda c: (c, 0, 0)),
            pl.BlockSpec((1, 2 * ns, kc), lambda c: (c, 0, 0)),
            pl.BlockSpec((1, 1, 2 * ns), lambda c: (c, 0, 0)),
        ],
        out_shape=[
            jax.ShapeDtypeStruct((nt, kc, kc), BF16),
            jax.ShapeDtypeStruct((nt, kc, 2 * ns), BF16),
            jax.ShapeDtypeStruct((nt, 2 * ns, kc), BF16),
            jax.ShapeDtypeStruct((nt, 1, 2 * ns), F32),
        ],
        compiler_params=_params(("parallel",)),
        name="ssm_wgen",
    )(*rows, bbd(b_re), bbd(b_im), cbd(c_re), cbd(c_im))


def _chunk_rows(u_ref, tb):
    return [u_ref[pl.ds(j, tb, stride=SSM_CHUNK), :] for j in range(SSM_CHUNK)]


def _ssm_sum_kernel(u_ref, ws_ref, s_ref, *, tb):
    u16 = jnp.concatenate(_chunk_rows(u_ref, tb), axis=1).astype(BF16)
    s_ref[0] = _dot(u16, ws_ref[0])


def _ssm_scan_kernel(s_ref, a16_ref, x_ref):
    ns = SSM_TILE_STATE
    a_re = a16_ref[0, :, :ns]
    a_im = a16_ref[0, :, ns:]
    nrow = s_ref.shape[1]

    def body(r, carry):
        xr, xi = carry
        x_ref[0, pl.ds(r, 1), :ns] = xr
        x_ref[0, pl.ds(r, 1), ns:] = xi
        sr = s_ref[0, pl.ds(r, 1), :ns]
        si = s_ref[0, pl.ds(r, 1), ns:]
        return a_re * xr - a_im * xi + sr, a_re * xi + a_im * xr + si

    zero = jnp.zeros((1, ns), F32)
    lax.fori_loop(0, nrow, body, (zero, zero), unroll=8)


def _ssm_out_kernel(u_ref, x_ref, w_ref, wx_ref, d_ref, g_ref, *, tb):
    us = _chunk_rows(u_ref, tb)
    u16 = jnp.concatenate(us, axis=1).astype(BF16)
    y16 = _dot(u16, w_ref[0]) + _dot(x_ref[0].astype(BF16), wx_ref[0])
    d = d_ref[...]
    for i in range(SSM_CHUNK):
        y = y16[:, i * LANES:(i + 1) * LANES] + d * us[i]
        g_ref[pl.ds(i, tb, stride=SSM_CHUNK), :] = _gelu(y)


def _ssm_branch(u, batch, seq, w, ws, wx, a16, d_skip):
    n = u.shape[0]
    nt, ns = SSM_TILES, SSM_TILE_STATE
    kc = SSM_CHUNK * LANES
    nb = n // SSM_CHUNK
    nbs = seq // SSM_CHUNK
    tb = min(nbs, 512)
    tok = tb * SSM_CHUNK
    s = pl.pallas_call(
        functools.partial(_ssm_sum_kernel, tb=tb),
        grid=(nt, nb // tb),
        in_specs=[
            pl.BlockSpec((tok, LANES), lambda c, r: (r, c)),
            pl.BlockSpec((1, kc, 2 * ns), lambda c, r: (c, 0, 0)),
        ],
        out_specs=pl.BlockSpec((1, tb, 2 * ns), lambda c, r: (c, r, 0)),
        out_shape=jax.ShapeDtypeStruct((nt, nb, 2 * ns), F32),
        compiler_params=_params(("parallel", "parallel")),
        name="ssm_chunk_sum",
    )(u, ws)
    x = pl.pallas_call(
        _ssm_scan_kernel,
        grid=(nt, batch),
        in_specs=[
            pl.BlockSpec((1, nbs, 2 * ns), lambda c, b: (c, b, 0)),
            pl.BlockSpec((1, 1, 2 * ns), lambda c, b: (c, 0, 0)),
        ],
        out_specs=pl.BlockSpec((1, nbs, 2 * ns), lambda c, b: (c, b, 0)),
        out_shape=jax.ShapeDtypeStruct((nt, nb, 2 * ns), F32),
        compiler_params=_params(("parallel", "parallel")),
        name="ssm_state_scan",
    )(s, a16)
    return pl.pallas_call(
        functools.partial(_ssm_out_kernel, tb=tb),
        grid=(nt, nb // tb),
        in_specs=[
            pl.BlockSpec((tok, LANES), lambda c, r: (r, c)),
            pl.BlockSpec((1, tb, 2 * ns), lambda c, r: (c, r, 0)),
            pl.BlockSpec((1, kc, kc), lambda c, r: (c, 0, 0)),
            pl.BlockSpec((1, 2 * ns, kc), lambda c, r: (c, 0, 0)),
            pl.BlockSpec((1, LANES), lambda c, r: (0, c)),
        ],
        out_specs=pl.BlockSpec((tok, LANES), lambda c, r: (r, c)),
        out_shape=jax.ShapeDtypeStruct((n, SSM_WIDTH), F32),
        compiler_params=_params(("parallel", "parallel")),
        name="ssm_chunk_out",
    )(u, x, w, wx, d_skip)


def _merge_kernel(attn_ref, g_ref, ga_ref, gs_ref, wglu_ref, wpa_ref, wps_ref, o_ref, ssm_scr):
    @pl.when(pl.program_id(1) == 0)
    def _():
        g = g_ref[...]
        ssm_scr[...] = (g * _sigmoid(_dot(g.astype(BF16), wglu_ref[...]))).astype(BF16)

    pa = _dot(attn_ref[...], wpa_ref[...])
    ps = _dot(ssm_scr[...], wps_ref[...])
    o_ref[...] = (_sigmoid(ga_ref[...].astype(F32)) * pa + _sigmoid(gs_ref[...].astype(F32)) * ps).astype(o_ref.dtype)


def _merge(attn, g, z, w_glu, w_pa, w_ps):
    n = attn.shape[0]
    tm = min(n, 1024)
    tn = 1024
    nj = D_MODEL // tn
    return pl.pallas_call(
        _merge_kernel,
        grid=(n // tm, nj),
        in_specs=[
            pl.BlockSpec((tm, N_HEADS * V_HEAD), lambda i, j: (i, 0)),
            pl.BlockSpec((tm, SSM_WIDTH), lambda i, j: (i, 0)),
            pl.BlockSpec((tm, tn), lambda i, j: (i, 1 + j)),
            pl.BlockSpec((tm, tn), lambda i, j: (i, 1 + nj + j)),
            pl.BlockSpec((SSM_WIDTH, SSM_WIDTH), lambda i, j: (0, 0)),
            pl.BlockSpec((N_HEADS * V_HEAD, tn), lambda i, j: (0, j)),
            pl.BlockSpec((SSM_WIDTH, tn), lambda i, j: (0, j)),
        ],
        out_specs=pl.BlockSpec((tm, tn), lambda i, j: (i, j)),
        out_shape=jax.ShapeDtypeStruct((n, D_MODEL), BF16),
        scratch_shapes=[pltpu.VMEM((tm, SSM_WIDTH), BF16)],
        compiler_params=_params(("parallel", "arbitrary")),
        name="glu_merge",
    )(attn, g, z, z, w_glu, w_pa, w_ps)


def _outproj_kernel(m_ref, x_ref, g1_ref, b1_ref, g2_ref, b2_ref, w_ref, h2_ref, h2t_ref):
    y = _dot(m_ref[...], w_ref[...])
    h = _ln(x_ref[...], g1_ref[...], b1_ref[...])
    h2 = _ln(DEEPNORM_ALPHA * h + y, g2_ref[...], b2_ref[...])
    h2_ref[...] = h2.astype(h2_ref.dtype)
    h2t_ref[...] = h2.T.astype(h2t_ref.dtype)


def _outproj(merged, x2, g1, b1, g2, b2, w_out):
    n = merged.shape[0]
    tm = min(n, 512)
    vec = pl.BlockSpec((1, D_MODEL), lambda i: (0, 0))
    return pl.pallas_call(
        _outproj_kernel,
        grid=(n // tm,),
        in_specs=[
            pl.BlockSpec((tm, D_MODEL), lambda i: (i, 0)),
            pl.BlockSpec((tm, D_MODEL), lambda i: (i, 0)),
            vec, vec, vec, vec,
            pl.BlockSpec((D_MODEL, D_MODEL), lambda i: (0, 0)),
        ],
        out_specs=[
            pl.BlockSpec((tm, D_MODEL), lambda i: (i, 0)),
            pl.BlockSpec((D_MODEL, tm), lambda i: (0, i)),
        ],
        out_shape=[
            jax.ShapeDtypeStruct((n, D_MODEL), BF16),
            jax.ShapeDtypeStruct((D_MODEL, n), BF16),
        ],
        compiler_params=_params(("parallel",)),
        name="outproj_ln",
    )(merged, x2, g1, b1, g2, b2, w_out)


def _peer_wcomb_kernel(keys_ref, wpq_ref, o_ref):
    o_ref[...] = lax.dot_general(keys_ref[0].astype(BF16), wpq_ref[...], (((1,), (1,)), ((), ())),
                                 preferred_element_type=F32).astype(o_ref.dtype)


def _peer_wcomb(sub_keys, w_pq):
    nhc = PEER_HEADS * 2
    keys = sub_keys.reshape(nhc, N_KEYS, PEER_KEY_HALF)
    return pl.pallas_call(
        _peer_wcomb_kernel,
        grid=(nhc,),
        in_specs=[
            pl.BlockSpec((1, N_KEYS, PEER_KEY_HALF), lambda i: (i, 0, 0)),
            pl.BlockSpec((D_MODEL, PEER_KEY_HALF), lambda i: (0, i)),
        ],
        out_specs=pl.BlockSpec((N_KEYS, D_MODEL), lambda i: (i, 0)),
        out_shape=jax.ShapeDtypeStruct((nhc * N_KEYS, D_MODEL), BF16),
        compiler_params=_params(("parallel",)),
        name="peer_wcomb",
    )(keys, w_pq)


PEER_TOK = 1024
PEER_THRESH_TOK = 2048


def _peer_scores_kernel(w_ref, xt_ref, s3_ref):
    rows = w_ref.shape[0] // 2
    for g in range(2):
        st = _dot(w_ref[g * rows:(g + 1) * rows, :], xt_ref[...])
        for c in range(s3_ref.shape[1]):
            s3_ref[g * rows:(g + 1) * rows, c, :] = st[:, c * LANES:(c + 1) * LANES]


def _peer_scores(wcomb, h2t):
    n = h2t.shape[1]
    tl = min(n, PEER_TOK)
    rows = wcomb.shape[0]
    return pl.pallas_call(
        _peer_scores_kernel,
        grid=(n // tl,),
        in_specs=[
            pl.BlockSpec((rows, D_MODEL), lambda i: (0, 0)),
            pl.BlockSpec((D_MODEL, tl), lambda i: (0, i)),
        ],
        out_specs=pl.BlockSpec((rows, tl // LANES, LANES), lambda i: (0, i, 0)),
        out_shape=jax.ShapeDtypeStruct((rows, n // LANES, LANES), F32),
        compiler_params=_params(("parallel",)),
        name="peer_scores",
    )(wcomb, h2t)


PEER_LIST = PEER_TOPK + 1
PEER_SUB = 4 * N_KEYS


def _insert_sorted(t, v):
    out = []
    for r in range(len(t)):
        out.append(jnp.maximum(t[r], v))
        v = jnp.minimum(t[r], v)
    return out


PEER_ROW = 4


def _count_above(vals, x, strict):
    above = (lambda v: v > x) if strict else (lambda v: v >= x)
    pick = lambda m, hi, lo: jnp.where(m, hi, lo)
    m8 = above(vals[7])
    m4 = above(pick(m8, vals[11], vals[3]))
    m2 = above(pick(m8, pick(m4, vals[13], vals[9]), pick(m4, vals[5], vals[1])))
    hi = pick(m4, pick(m2, vals[14], vals[12]), pick(m2, vals[10], vals[8]))
    lo = pick(m4, pick(m2, vals[6], vals[4]), pick(m2, vals[2], vals[0]))
    m1 = above(pick(m8, hi, lo))
    cnt = (pick(m8, 8.0, 0.0) + pick(m4, 4.0, 0.0)) + (pick(m2, 2.0, 0.0) + pick(m1, 1.0, 0.0))
    return pick(above(vals[15]), 16.0, cnt)


def _peer_thresh_kernel(s_ref, c_ref, e1_ref, q_ref, e2_ref, q_scr, e2_scr):
    shape = s_ref.shape[1:]
    neg = jnp.full(shape, -jnp.inf, F32)

    def top_list(base):
        def body(n, t):
            return tuple(_insert_sorted(list(t), s_ref[base + n]))
        return lax.fori_loop(0, N_KEYS, body, (neg,) * PEER_LIST, unroll=4)

    a = top_list(0)
    b = top_list(N_KEYS)
    t = [neg] * PEER_LIST
    for i in range(1, PEER_LIST + 1):
        for j in range(1, PEER_LIST // i + 1):
            t = _insert_sorted(t, a[i - 1] + b[j - 1])
    tau = 0.5 * (t[PEER_TOPK - 1] + t[PEER_TOPK])
    top = a[0] + b[0]
    z = jnp.exp(t[0] - top)
    for r in range(1, PEER_TOPK):
        z = z + jnp.exp(t[r] - top)
    cv = b[0] + jnp.log(2.0 * z)
    a16 = a[:PEER_TOPK]

    def write(n, carry):
        s1 = s_ref[n]
        s2 = s_ref[N_KEYS + n]
        c = 1.0 + _count_above(a16, s1, True)
        e1 = jnp.exp(s1 - a16[0])
        for gq in range(shape[0] // PEER_ROW):
            c_ref[n, gq] = c[gq * PEER_ROW:(gq + 1) * PEER_ROW, :]
            e1_ref[n, gq] = e1[gq * PEER_ROW:(gq + 1) * PEER_ROW, :]
        row0 = pl.multiple_of(n * nsub, nsub)
        q_scr[pl.ds(row0, nsub), :] = _count_above(a16, tau - s2, False)
        e2_scr[pl.ds(row0, nsub), :] = jnp.exp(s2 - cv)
        return carry

    nsub = shape[0]
    lax.fori_loop(0, N_KEYS, write, 0, unroll=2)
    for c in range(nsub):
        q_ref[:, c * LANES:(c + 1) * LANES] = q_scr[pl.ds(c, N_KEYS, stride=nsub), :].astype(q_ref.dtype)
        e2_ref[:, c * LANES:(c + 1) * LANES] = e2_scr[pl.ds(c, N_KEYS, stride=nsub), :].astype(e2_ref.dtype)


def _peer_thresh(s3):
    rows, nl, _ = s3.shape
    n = nl * LANES
    nb = min(PEER_THRESH_TOK // LANES, nl)
    half = rows // 2
    ng = nb // PEER_ROW
    row_spec = pl.BlockSpec((N_KEYS, ng, PEER_ROW, LANES), lambda i, h: (h, i, 0, 0))
    row_shape = jax.ShapeDtypeStruct((half, nl // PEER_ROW, PEER_ROW, LANES), F32)
    tile_spec = pl.BlockSpec((N_KEYS, nb * LANES), lambda i, h: (h, i))
    tile_shape = jax.ShapeDtypeStruct((half, n), BF16)
    c, e1, q, e2 = pl.pallas_call(
        _peer_thresh_kernel,
        grid=(nl // nb, PEER_HEADS),
        in_specs=[pl.BlockSpec((2 * N_KEYS, nb, LANES), lambda i, h: (h, i, 0))],
        out_specs=[row_spec, row_spec, tile_spec, tile_spec],
        out_shape=[row_shape, row_shape, tile_shape, tile_shape],
        scratch_shapes=[pltpu.VMEM((N_KEYS * nb, LANES), F32), pltpu.VMEM((N_KEYS * nb, LANES), F32)],
        compiler_params=_params(("parallel", "parallel")),
        name="peer_thresh",
    )(s3)
    shape5 = (PEER_HEADS, N_KEYS, nl // PEER_ROW, PEER_ROW, LANES)
    tiles = lambda v: v.reshape(PEER_HEADS, N_KEYS, n)
    return c.reshape(shape5), e1.reshape(shape5), tiles(q), tiles(e2)


def _peer_dense_kernel(xt_ref, u_ref, vt_ref, q_ref, e2_ref, c_ref, e1_ref, h2_ref, g_ref, b_ref,
                       o_ref, acc_scr, a_scr, *, sub):
    e = pl.program_id(1)
    eb = u_ref.shape[0]

    @pl.when(e == 0)
    def _():
        acc_scr[...] = jnp.zeros(acc_scr.shape, F32)

    def row(ref, h, il):
        return jnp.concatenate([ref[h, il, 0, c:c + 1, :] for c in range(ref.shape[3])], axis=1).astype(BF16)

    zero = jnp.zeros((), BF16)

    xt = xt_ref[...]
    nsub = eb // sub
    score = lambda s: _dot(u_ref[s * sub:(s + 1) * sub, :], xt)
    st_next = score(0)
    for s in range(nsub):
        st, st_next = st_next, (score(s + 1) if s + 1 < nsub else None)
        for ii in range(sub // N_KEYS):
            il = s * (sub // N_KEYS) + ii
            gate = None
            for h in range(PEER_HEADS):
                term = row(e1_ref, h, il) * jnp.where(q_ref[h] >= row(c_ref, h, il), e2_ref[h], zero)
                gate = term if gate is None else gate + term
            a = _gelu_x2(st[ii * N_KEYS:(ii + 1) * N_KEYS, :]).astype(BF16) * gate
            a_scr[s * sub + ii * N_KEYS:s * sub + (ii + 1) * N_KEYS, :] = a
        acc_scr[...] += _dot(vt_ref[:, s * sub:(s + 1) * sub], a_scr[s * sub:(s + 1) * sub, :])

    @pl.when(e == pl.num_programs(1) - 1)
    def _():
        y = acc_scr[...].T
        o_ref[...] = _ln(DEEPNORM_ALPHA * h2_ref[...].astype(F32) + y, g_ref[...], b_ref[...])


def _peer_dense(h2, h2t, q, e2, c, e1, u_tab, vt_tab, ln_g, ln_b):
    n = h2.shape[0]
    t = min(n, PEER_ROW * LANES)
    eb = 8 * N_KEYS
    sub = PEER_SUB
    vec = pl.BlockSpec((1, D_MODEL), lambda i, e: (0, 0))
    row_spec = pl.BlockSpec((PEER_HEADS, eb // N_KEYS, 1, PEER_ROW, LANES), lambda i, e: (0, e, i, 0, 0))
    return pl.pallas_call(
        functools.partial(_peer_dense_kernel, sub=sub),
        grid=(n // t, N_EXPERTS // eb),
        in_specs=[
            pl.BlockSpec((D_MODEL, t), lambda i, e: (0, i)),
            pl.BlockSpec((eb, D_MODEL), lambda i, e: (e, 0)),
            pl.BlockSpec((D_MODEL, eb), lambda i, e: (0, e)),
            pl.BlockSpec((PEER_HEADS, N_KEYS, t), lambda i, e: (0, 0, i)),
            pl.BlockSpec((PEER_HEADS, N_KEYS, t), lambda i, e: (0, 0, i)),
            row_spec, row_spec,
            pl.BlockSpec((t, D_MODEL), lambda i, e: (i, 0)),
            vec, vec,
        ],
        out_specs=pl.BlockSpec((t, D_MODEL), lambda i, e: (i, 0)),
        out_shape=jax.ShapeDtypeStruct((n, D_MODEL), F32),
        scratch_shapes=[
            pltpu.VMEM((D_MODEL, t), F32),
            pltpu.VMEM((eb, t), BF16),
        ],
        compiler_params=_params(("parallel", "arbitrary")),
        name="peer_dense",
    )(h2t, u_tab, vt_tab, q, e2, c, e1, h2, ln_g, ln_b)


def _rot_half_cols(w):
    half = QK_ROPE // 2
    return jnp.concatenate([-w[..., half:], w[..., :half]], axis=-1)


def kernel(x, ln_in_g, ln_in_b, w_in, q_norm_g, w_uq, kv_norm_g, w_ukv, ssm_lam_re, ssm_lam_im, ssm_log_dt,
           ssm_b_re, ssm_b_im, ssm_c_re, ssm_c_im, ssm_d, w_glu, w_proj_attn, w_proj_ssm, w_out, ln_mix_g, ln_mix_b,
           w_peer_q, peer_sub_keys, peer_u, peer_v, ln_ffn_g, ln_ffn_b):
    batch, seq, _ = x.shape
    n = batch * seq
    x2 = x.reshape(n, D_MODEL)
    row = lambda v: v.reshape(1, -1).astype(F32)
    l = 0

    wi = w_in[l]
    w_main = jnp.concatenate([wi[:, OFF_SSM:OFF_KR], wi[:, OFF_GA:IN_WIDTH]], axis=1).astype(BF16)
    w_krc = wi[:, OFF_KR:OFF_GA]
    w_kr = jnp.concatenate([w_krc, _rot_half_cols(w_krc)], axis=1).astype(BF16)
    wq = w_uq[l].reshape(Q_LORA, N_HEADS, QK_HEAD)
    wq_ext = jnp.concatenate([wq, _rot_half_cols(wq[..., QK_NOPE:])], axis=-1)
    wq_ext = wq_ext.transpose(1, 0, 2).astype(BF16)
    wkv = w_ukv[l].reshape(KV_LORA, N_HEADS, QK_NOPE + V_HEAD).transpose(1, 0, 2).astype(BF16)

    cs = _rope_table(seq)
    u, z, kr = _inproj(x2, row(ln_in_g), row(ln_in_b), w_main, w_kr)

    q, k, v = _qkv(z, kr, cs, row(q_norm_g[l]), row(kv_norm_g[l]), wq_ext, wkv, batch, seq)
    attn = _attention(q, k, v).reshape(n, N_HEADS * V_HEAD)

    w, ws, wx, a16 = _ssm_wgen(ssm_lam_re[l], ssm_lam_im[l], ssm_log_dt[l], ssm_b_re[l], ssm_b_im[l],
                               ssm_c_re[l], ssm_c_im[l])
    g = _ssm_branch(u, batch, seq, w, ws, wx, a16, row(ssm_d[l]))

    merged = _merge(attn, g, z, w_glu[l].astype(BF16), w_proj_attn[l].astype(BF16), w_proj_ssm[l].astype(BF16))
    h2, h2t = _outproj(merged, x2, row(ln_in_g), row(ln_in_b), row(ln_mix_g[l]), row(ln_mix_b[l]),
                       w_out[l].astype(BF16))

    wcomb = _peer_wcomb(peer_sub_keys[l], w_peer_q[l].astype(BF16))
    s3 = _peer_scores(wcomb, h2t)
    c, e1, q, e2 = _peer_thresh(s3)
    out = _peer_dense(h2, h2t, q, e2, c, e1, peer_u[l].astype(BF16), peer_v[l].T.astype(BF16),
                      row(ln_ffn_g[l]), row(ln_ffn_b[l]))
    return out.reshape(batch, seq, D_MODEL)
```

```python
import functools
import math

import jax
import jax.numpy as jnp
from jax import lax
from jax.experimental import pallas as pl
from jax.experimental.pallas import tpu as pltpu

F32 = jnp.float32
BF16 = jnp.bfloat16

D_MODEL = 2048
DEPTH = 1
CHUNK = 64
SSM_WIDTH = 1024
SSM_GROUP = 16
SSM_GROUPS = SSM_WIDTH // SSM_GROUP
SSM_STATE = 64
N_HEADS = 16
Q_LORA = 512
KV_LORA = 512
QK_NOPE = 128
QK_ROPE = 64
V_HEAD = 128
QK_HEAD = QK_NOPE + QK_ROPE
ROPE_THETA = 10000.0
PEER_HEADS = 8
N_KEYS = 128
N_EXPERTS = N_KEYS * N_KEYS
PEER_KEY_HALF = 128
PEER_TOPK = 16
LN_EPS = 1e-5
RMS_EPS = 1e-6
DEEPNORM_ALPHA = (2.0 * DEPTH) ** 0.25
OFF_SSM = 0
OFF_Q = OFF_SSM + SSM_WIDTH
OFF_KV = OFF_Q + Q_LORA
OFF_KR = OFF_KV + KV_LORA
OFF_GA = OFF_KR + QK_ROPE
OFF_GS = OFF_GA + D_MODEL
IN_WIDTH = OFF_GS + D_MODEL

LANES = 128
VMEM_LIMIT_BYTES = 56 * 1024 * 1024

SSM_CHUNK = 16
SSM_TILE_GROUPS = LANES // SSM_GROUP
SSM_TILES = SSM_WIDTH // LANES
SSM_TILE_STATE = SSM_TILE_GROUPS * SSM_STATE
NEG_BIG = -1e30


def _params(semantics):
    return pltpu.CompilerParams(dimension_semantics=semantics, vmem_limit_bytes=VMEM_LIMIT_BYTES)


def _dot(a, b):
    return jnp.dot(a, b, preferred_element_type=F32)


def _ln(x, g, b):
    mu = jnp.mean(x, axis=-1, keepdims=True)
    xc = x - mu
    var = jnp.mean(xc * xc, axis=-1, keepdims=True)
    return xc * lax.rsqrt(var + LN_EPS) * g + b


GELU_C = 0.7978845608028654
GELU_A = 0.044715


def _gelu(x):
    return 0.5 * x * (1.0 + jnp.tanh(GELU_C * (x + GELU_A * (x * x * x))))


def _gelu_x2(x):
    return x * (1.0 + jnp.tanh(x * (GELU_C + (GELU_C * GELU_A) * (x * x))))


def _sigmoid(x):
    return 1.0 / (1.0 + jnp.exp(-x))


def _rope_table_kernel(invf_ref, o_ref):
    ts = o_ref.shape[0]
    pos = (lax.broadcasted_iota(jnp.int32, (ts, LANES), 0) + pl.program_id(0) * ts).astype(F32)
    ang = pos * invf_ref[...]
    lane = lax.broadcasted_iota(jnp.int32, (ts, LANES), 1)
    o_ref[...] = jnp.where(lane < 2 * (QK_ROPE // 2), jnp.cos(ang), jnp.sin(ang))


def _rope_table(seq):
    half = QK_ROPE // 2
    invf = 1.0 / (ROPE_THETA ** (jnp.arange(half, dtype=F32) / half))
    invf4 = jnp.tile(invf, 4)[None, :]
    ts = min(seq, 1024)
    return pl.pallas_call(
        _rope_table_kernel,
        grid=(seq // ts,),
        in_specs=[pl.BlockSpec((1, LANES), lambda i: (0, 0))],
        out_specs=pl.BlockSpec((ts, LANES), lambda i: (i, 0)),
        out_shape=jax.ShapeDtypeStruct((seq, LANES), F32),
        compiler_params=_params(("parallel",)),
        name="rope_table",
    )(invf4)


def _inproj_kernel(x_ref, g_ref, b_ref, w_ref, wkr_ref, u_ref, z_ref, kr_ref, h_scr):
    j = pl.program_id(1)

    @pl.when(j == 0)
    def _():
        h = _ln(x_ref[...], g_ref[...], b_ref[...]).astype(BF16)
        h_scr[...] = h
        u_ref[...] = _dot(h, w_ref[...])
        kr_ref[...] = _dot(h, wkr_ref[...]).astype(kr_ref.dtype)

    @pl.when(j > 0)
    def _():
        z_ref[...] = _dot(h_scr[...], w_ref[...]).astype(z_ref.dtype)


def _inproj(x2, ln_g, ln_b, w_main, w_kr):
    n = x2.shape[0]
    tm = min(n, 1024)
    tn = 1024
    nj = w_main.shape[1] // tn
    return pl.pallas_call(
        _inproj_kernel,
        grid=(n // tm, nj),
        in_specs=[
            pl.BlockSpec((tm, D_MODEL), lambda i, j: (i, 0)),
            pl.BlockSpec((1, D_MODEL), lambda i, j: (0, 0)),
            pl.BlockSpec((1, D_MODEL), lambda i, j: (0, 0)),
            pl.BlockSpec((D_MODEL, tn), lambda i, j: (0, j)),
            pl.BlockSpec((D_MODEL, LANES), lambda i, j: (0, 0)),
        ],
        out_specs=[
            pl.BlockSpec((tm, tn), lambda i, j: (i, 0)),
            pl.BlockSpec((tm, tn), lambda i, j: (i, jnp.maximum(j - 1, 0))),
            pl.BlockSpec((tm, LANES), lambda i, j: (i, 0)),
        ],
        out_shape=[
            jax.ShapeDtypeStruct((n, tn), F32),
            jax.ShapeDtypeStruct((n, (nj - 1) * tn), BF16),
            jax.ShapeDtypeStruct((n, LANES), BF16),
        ],
        scratch_shapes=[pltpu.VMEM((tm, D_MODEL), BF16)],
        compiler_params=_params(("parallel", "arbitrary")),
        name="ln_inproj",
    )(x2, ln_g, ln_b, w_main, w_kr)


def _rope_mix(t2, cs):
    t = t2 * cs
    return t + pltpu.roll(t, QK_ROPE, 1)


def _qkv_kernel(cq_ref, ckv_ref, kr_ref, cs_ref, gq_ref, gkv_ref, wq_ref, wkv_ref,
                q_ref, k_ref, v_ref, cqn_scr, ckvn_scr, kro_scr):
    cs = cs_ref[...]

    @pl.when(pl.program_id(1) == 0)
    def _():
        def rms(c, g):
            c = c.astype(F32)
            return (c * lax.rsqrt(jnp.mean(c * c, axis=-1, keepdims=True) + RMS_EPS) * g).astype(BF16)

        cqn_scr[...] = rms(cq_ref[...], gq_ref[...])
        ckvn_scr[...] = rms(ckv_ref[...], gkv_ref[...])
        kro_scr[...] = _rope_mix(kr_ref[...].astype(F32), cs).astype(BF16)

    scale = math.log2(math.e) / math.sqrt(QK_HEAD)
    hp = wq_ref.shape[0]
    wide = 2 * LANES
    qe2 = _dot(cqn_scr[...], jnp.concatenate([wq_ref[j] for j in range(hp)], axis=1))
    kv2 = _dot(ckvn_scr[...], jnp.concatenate([wkv_ref[j] for j in range(hp)], axis=1))
    for j in range(hp):
        qe = qe2[:, j * wide:(j + 1) * wide]
        qr = _rope_mix(qe[:, QK_NOPE:], cs)
        q_ref[0, j, :, :QK_NOPE] = (qe[:, :QK_NOPE] * scale).astype(BF16)
        q_ref[0, j, :, QK_NOPE:] = (qr[:, :QK_ROPE] * scale).astype(BF16)
        kv = kv2[:, j * wide:(j + 1) * wide]
        k_ref[0, j, :, :QK_NOPE] = kv[:, :QK_NOPE].astype(BF16)
        k_ref[0, j, :, QK_NOPE:] = kro_scr[:, :QK_ROPE]
        v_ref[0, j] = kv[:, QK_NOPE:].astype(BF16)


def _qkv(z, kr, cs, gq, gkv, wq_ext, wkv, batch, seq):
    n = z.shape[0]
    tm = min(seq, 2048)
    nsb = seq // tm
    hp = 2
    return pl.pallas_call(
        _qkv_kernel,
        grid=(n // tm, N_HEADS // hp),
        in_specs=[
            pl.BlockSpec((tm, Q_LORA), lambda i, h: (i, 0)),
            pl.BlockSpec((tm, KV_LORA), lambda i, h: (i, 1)),
            pl.BlockSpec((tm, LANES), lambda i, h: (i, 0)),
            pl.BlockSpec((tm, LANES), lambda i, h: (i % nsb, 0)),
            pl.BlockSpec((1, Q_LORA), lambda i, h: (0, 0)),
            pl.BlockSpec((1, KV_LORA), lambda i, h: (0, 0)),
            pl.BlockSpec((hp, Q_LORA, 2 * LANES), lambda i, h: (h, 0, 0)),
            pl.BlockSpec((hp, KV_LORA, 2 * LANES), lambda i, h: (h, 0, 0)),
        ],
        out_specs=[
            pl.BlockSpec((1, hp, tm, QK_HEAD), lambda i, h: (i // nsb, h, i % nsb, 0)),
            pl.BlockSpec((1, hp, tm, QK_HEAD), lambda i, h: (i // nsb, h, i % nsb, 0)),
            pl.BlockSpec((1, hp, tm, V_HEAD), lambda i, h: (i // nsb, h, i % nsb, 0)),
        ],
        out_shape=[
            jax.ShapeDtypeStruct((batch, N_HEADS, seq, QK_HEAD), BF16),
            jax.ShapeDtypeStruct((batch, N_HEADS, seq, QK_HEAD), BF16),
            jax.ShapeDtypeStruct((batch, N_HEADS, seq, V_HEAD), BF16),
        ],
        scratch_shapes=[
            pltpu.VMEM((tm, Q_LORA), BF16),
            pltpu.VMEM((tm, KV_LORA), BF16),
            pltpu.VMEM((tm, LANES), BF16),
        ],
        compiler_params=_params(("parallel", "arbitrary")),
        name="qkv_rope",
    )(z, z, kr, cs, gq, gkv, wq_ext, wkv)


ATT_ROWS = 512
ATT_CHAINS = 8
ATT_BK_MAIN = 1024


def _attn_kernel(q_ref, k_ref, v_ref, o_ref, m_scr, acc_scr, *, rows, chains, bkm):
    qi = pl.program_id(2)
    m_scr[...] = jnp.full(m_scr.shape, -jnp.inf, F32)
    acc_scr[...] = jnp.zeros(acc_scr.shape, F32)

    def load_kv(j, bk):
        start = pl.multiple_of(j * bk, bk)
        kj = k_ref[0, 0, pl.ds(start, bk), :]
        vj = jnp.concatenate([v_ref[0, 0, pl.ds(start, bk), :], jnp.ones((bk, LANES), BF16)], axis=1)
        return kj, vj

    def block(c, kj, vj, masked):
        q = q_ref[0, 0, c * rows:(c + 1) * rows, :]
        s = lax.dot_general(q, kj, (((1,), (1,)), ((), ())), preferred_element_type=F32)
        if masked:
            rc = lax.broadcasted_iota(jnp.int32, s.shape, 0) // CHUNK
            cc = lax.broadcasted_iota(jnp.int32, s.shape, 1) // CHUNK
            s = jnp.where(cc <= rc, s, NEG_BIG)
        m_prev = m_scr[c]
        m_new = jnp.maximum(m_prev, jnp.max(s, axis=-1, keepdims=True))
        alpha = jnp.exp2(m_prev - m_new)
        p = jnp.exp2(s - jnp.tile(m_new, (1, s.shape[1] // LANES)))
        acc_scr[c] = jnp.tile(alpha, (1, 2)) * acc_scr[c] + _dot(p.astype(BF16), vj)
        m_scr[c] = m_new

    per_step = chains * rows // bkm
    group = 2 if per_step % 2 == 0 else 1

    def body(j, carry):
        for jj in range(group):
            kj, vj = load_kv(j * group + jj, bkm)
            for c in range(chains):
                block(c, kj, vj, False)
        return carry

    lax.fori_loop(0, qi * (per_step // group), body, 0)
    for jj in range(chains):
        kj, vj = load_kv(qi * chains + jj, rows)
        for c in range(jj, chains):
            block(c, kj, vj, c == jj)
    for c in range(chains):
        a = acc_scr[c]
        o_ref[0, c * rows:(c + 1) * rows, :] = (a[:, :V_HEAD] / a[:, V_HEAD:]).astype(o_ref.dtype)


def _attention(q, k, v):
    batch, _, seq, _ = q.shape
    rows = min(seq, ATT_ROWS)
    chains = min(ATT_CHAINS, seq // rows)
    bq = rows * chains
    bkm = min(ATT_BK_MAIN, bq)
    return pl.pallas_call(
        functools.partial(_attn_kernel, rows=rows, chains=chains, bkm=bkm),
        grid=(batch, N_HEADS, seq // bq),
        in_specs=[
            pl.BlockSpec((1, 1, bq, QK_HEAD), lambda b, h, i: (b, h, i, 0)),
            pl.BlockSpec((1, 1, seq, QK_HEAD), lambda b, h, i: (b, h, 0, 0)),
            pl.BlockSpec((1, 1, seq, V_HEAD), lambda b, h, i: (b, h, 0, 0)),
        ],
        out_specs=pl.BlockSpec((1, bq, V_HEAD), lambda b, h, i: (b, i, h)),
        out_shape=jax.ShapeDtypeStruct((batch, seq, N_HEADS * V_HEAD), BF16),
        scratch_shapes=[
            pltpu.VMEM((chains, rows, LANES), F32),
            pltpu.VMEM((chains, rows, 2 * V_HEAD), F32),
        ],
        compiler_params=_params(("parallel", "parallel", "arbitrary")),
        name="flash_attn",
    )(q, k, v)


def _cpow(lr, li, ldt, k):
    dt = jnp.exp(ldt)
    mag = jnp.exp(lr * dt * k)
    ang = li * dt * k
    return mag * jnp.cos(ang), mag * jnp.sin(ang)


def _ssm_wgen_kernel(lr_r, li_r, ldt_r, bre_ref, bim_ref, cre_ref, cim_ref,
                     w_ref, ws_ref, wx_ref, a16_ref):
    hp = lax.Precision.HIGHEST
    ns = SSM_TILE_STATE
    lr, li, ldt = lr_r[0], li_r[0], ldt_r[0]
    a_re, a_im = _cpow(lr, li, ldt, 1.0)
    den = lr * lr + li * li
    nr = a_re - 1.0
    ni = a_im
    cf_re = (nr * lr + ni * li) / den
    cf_im = (ni * lr - nr * li) / den
    bre, bim = bre_ref[0], bim_ref[0]
    cre, cim = cre_ref[0], cim_ref[0]
    w_ref[...] = jnp.zeros(w_ref.shape, w_ref.dtype)
    for k in range(SSM_CHUNK):
        p_re, p_im = _cpow(lr, li, ldt, float(k))
        m_re = cf_re * p_re - cf_im * p_im
        m_im = cf_re * p_im + cf_im * p_re
        bm_re = bre * m_re - bim * m_im
        bm_im = bre * m_im + bim * m_re
        bd = (jnp.dot(bm_re, cre, precision=hp, preferred_element_type=F32)
              - jnp.dot(bm_im, cim, precision=hp, preferred_element_type=F32)).astype(BF16)
        for j in range(SSM_CHUNK - k):
            i = j + k
            w_ref[0, j * LANES:(j + 1) * LANES, i * LANES:(i + 1) * LANES] = bd
        jj = SSM_CHUNK - 1 - k
        ws_ref[0, jj * LANES:(jj + 1) * LANES, :ns] = bm_re.astype(BF16)
        ws_ref[0, jj * LANES:(jj + 1) * LANES, ns:] = bm_im.astype(BF16)
    cre_t, cim_t = cre.T, cim.T
    for i in range(SSM_CHUNK):
        p_re, p_im = _cpow(lr, li, ldt, float(i + 1))
        wx_ref[0, :ns, i * LANES:(i + 1) * LANES] = (p_re * cre_t - p_im * cim_t).T.astype(BF16)
        wx_ref[0, ns:, i * LANES:(i + 1) * LANES] = (-(p_re * cim_t + p_im * cre_t)).T.astype(BF16)
    q_re, q_im = _cpow(lr, li, ldt, float(SSM_CHUNK))
    a16_ref[0, :, :ns] = q_re
    a16_ref[0, :, ns:] = q_im


def _ssm_wgen(lam_re, lam_im, log_dt, b_re, b_im, c_re, c_im):
    nt, tg, ns = SSM_TILES, SSM_TILE_GROUPS, SSM_TILE_STATE
    eye = jnp.eye(tg, dtype=F32)
    bbd = lambda b: jnp.einsum("cgph,gk->cghkp", b.reshape(nt, tg, SSM_STATE, SSM_GROUP), eye).reshape(nt, LANES, ns)
    cbd = lambda c: jnp.einsum("cghp,gk->cgpkh", c.reshape(nt, tg, SSM_GROUP, SSM_STATE), eye).reshape(nt, ns, LANES)
    ldt = jnp.repeat(log_dt, SSM_STATE)
    rows = [a.reshape(nt, 1, ns) for a in (lam_re, lam_im, ldt)]
    kc = SSM_CHUNK * LANES
    row_spec = pl.BlockSpec((1, 1, ns), lambda c: (c, 0, 0))
    return pl.pallas_call(
        _ssm_wgen_kernel,
        grid=(nt,),
        in_specs=[row_spec] * 3 + [
            pl.BlockSpec((1, LANES, ns), lambda c: (c, 0, 0)),
            pl.BlockSpec((1, LANES, ns), lambda c: (c, 0, 0)),
            pl.BlockSpec((1, ns, LANES), lambda c: (c, 0, 0)),
            pl.BlockSpec((1, ns, LANES), lambda c: (c, 0, 0)),
        ],
        out_specs=[
            pl.BlockSpec((1, kc, kc), lambda c: (c, 0, 0)),
            pl.BlockSpec((1, kc, 2 * ns), lambda c: (c, 0, 0)),
            pl.BlockSpec((1, 2 * ns, kc), lambda c: (c, 0, 0)),
            pl.BlockSpec((1, 1, 2 * ns), lambda c: (c, 0, 0)),
        ],
        out_shape=[
            jax.ShapeDtypeStruct((nt, kc, kc), BF16),
            jax.ShapeDtypeStruct((nt, kc, 2 * ns), BF16),
            jax.ShapeDtypeStruct((nt, 2 * ns, kc), BF16),
            jax.ShapeDtypeStruct((nt, 1, 2 * ns), F32),
        ],
        compiler_params=_params(("parallel",)),
        name="ssm_wgen",
    )(*rows, bbd(b_re), bbd(b_im), cbd(c_re), cbd(c_im))


def _chunk_rows(u_ref, tb):
    return [u_ref[pl.ds(j, tb, stride=SSM_CHUNK), :] for j in range(SSM_CHUNK)]


def _ssm_sum_kernel(u_ref, ws_ref, s_ref, *, tb):
    u16 = jnp.concatenate(_chunk_rows(u_ref, tb), axis=1).astype(BF16)
    s_ref[0] = _dot(u16, ws_ref[0])


def _ssm_scan_kernel(s_ref, a16_ref, x_ref):
    ns = SSM_TILE_STATE
    a_re = a16_ref[0, :, :ns]
    a_im = a16_ref[0, :, ns:]
    nrow = s_ref.shape[1]

    def body(r, carry):
        xr, xi = carry
        x_ref[0, pl.ds(r, 1), :ns] = xr
        x_ref[0, pl.ds(r, 1), ns:] = xi
        sr = s_ref[0, pl.ds(r, 1), :ns]
        si = s_ref[0, pl.ds(r, 1), ns:]
        return a_re * xr - a_im * xi + sr, a_re * xi + a_im * xr + si

    zero = jnp.zeros((1, ns), F32)
    lax.fori_loop(0, nrow, body, (zero, zero), unroll=8)


def _ssm_out_kernel(u_ref, x_ref, w_ref, wx_ref, d_ref, g_ref, *, tb):
    us = _chunk_rows(u_ref, tb)
    u16 = jnp.concatenate(us, axis=1).astype(BF16)
    y16 = _dot(u16, w_ref[0]) + _dot(x_ref[0].astype(BF16), wx_ref[0])
    d = d_ref[...]
    for i in range(SSM_CHUNK):
        y = y16[:, i * LANES:(i + 1) * LANES] + d * us[i]
        g_ref[pl.ds(i, tb, stride=SSM_CHUNK), :] = _gelu(y)


def _ssm_branch(u, batch, seq, w, ws, wx, a16, d_skip):
    n = u.shape[0]
    nt, ns = SSM_TILES, SSM_TILE_STATE
    kc = SSM_CHUNK * LANES
    nb = n // SSM_CHUNK
    nbs = seq // SSM_CHUNK
    tb = min(nbs, 512)
    tok = tb * SSM_CHUNK
    s = pl.pallas_call(
        functools.partial(_ssm_sum_kernel, tb=tb),
        grid=(nt, nb // tb),
        in_specs=[
            pl.BlockSpec((tok, LANES), lambda c, r: (r, c)),
            pl.BlockSpec((1, kc, 2 * ns), lambda c, r: (c, 0, 0)),
        ],
        out_specs=pl.BlockSpec((1, tb, 2 * ns), lambda c, r: (c, r, 0)),
        out_shape=jax.ShapeDtypeStruct((nt, nb, 2 * ns), F32),
        compiler_params=_params(("parallel", "parallel")),
        name="ssm_chunk_sum",
    )(u, ws)
    x = pl.pallas_call(
        _ssm_scan_kernel,
        grid=(nt, batch),
        in_specs=[
            pl.BlockSpec((1, nbs, 2 * ns), lambda c, b: (c, b, 0)),
            pl.BlockSpec((1, 1, 2 * ns), lambda c, b: (c, 0, 0)),
        ],
        out_specs=pl.BlockSpec((1, nbs, 2 * ns), lambda c, b: (c, b, 0)),
        out_shape=jax.ShapeDtypeStruct((nt, nb, 2 * ns), F32),
        compiler_params=_params(("parallel", "parallel")),
        name="ssm_state_scan",
    )(s, a16)
    return pl.pallas_call(
        functools.partial(_ssm_out_kernel, tb=tb),
        grid=(nt, nb // tb),
        in_specs=[
            pl.BlockSpec((tok, LANES), lambda c, r: (r, c)),
            pl.BlockSpec((1, tb, 2 * ns), lambda c, r: (c, r, 0)),
            pl.BlockSpec((1, kc, kc), lambda c, r: (c, 0, 0)),
            pl.BlockSpec((1, 2 * ns, kc), lambda c, r: (c, 0, 0)),
            pl.BlockSpec((1, LANES), lambda c, r: (0, c)),
        ],
        out_specs=pl.BlockSpec((tok, LANES), lambda c, r: (r, c)),
        out_shape=jax.ShapeDtypeStruct((n, SSM_WIDTH), F32),
        compiler_params=_params(("parallel", "parallel")),
        name="ssm_chunk_out",
    )(u, x, w, wx, d_skip)


def _merge_kernel(attn_ref, g_ref, ga_ref, gs_ref, wglu_ref, wpa_ref, wps_ref, o_ref, ssm_scr):
    @pl.when(pl.program_id(1) == 0)
    def _():
        g = g_ref[...]
        ssm_scr[...] = (g * _sigmoid(_dot(g.astype(BF16), wglu_ref[...]))).astype(BF16)

    pa = _dot(attn_ref[...], wpa_ref[...])
    ps = _dot(ssm_scr[...], wps_ref[...])
    o_ref[...] = (_sigmoid(ga_ref[...].astype(F32)) * pa + _sigmoid(gs_ref[...].astype(F32)) * ps).astype(o_ref.dtype)


def _merge(attn, g, z, w_glu, w_pa, w_ps):
    n = attn.shape[0]
    tm = min(n, 1024)
    tn = 1024
    nj = D_MODEL // tn
    return pl.pallas_call(
        _merge_kernel,
        grid=(n // tm, nj),
        in_specs=[
            pl.BlockSpec((tm, N_HEADS * V_HEAD), lambda i, j: (i, 0)),
            pl.BlockSpec((tm, SSM_WIDTH), lambda i, j: (i, 0)),
            pl.BlockSpec((tm, tn), lambda i, j: (i, 1 + j)),
            pl.BlockSpec((tm, tn), lambda i, j: (i, 1 + nj + j)),
            pl.BlockSpec((SSM_WIDTH, SSM_WIDTH), lambda i, j: (0, 0)),
            pl.BlockSpec((N_HEADS * V_HEAD, tn), lambda i, j: (0, j)),
            pl.BlockSpec((SSM_WIDTH, tn), lambda i, j: (0, j)),
        ],
        out_specs=pl.BlockSpec((tm, tn), lambda i, j: (i, j)),
        out_shape=jax.ShapeDtypeStruct((n, D_MODEL), BF16),
        scratch_shapes=[pltpu.VMEM((tm, SSM_WIDTH), BF16)],
        compiler_params=_params(("parallel", "arbitrary")),
        name="glu_merge",
    )(attn, g, z, z, w_glu, w_pa, w_ps)


def _outproj_kernel(m_ref, x_ref, g1_ref, b1_ref, g2_ref, b2_ref, w_ref, h2_ref, h2t_ref):
    y = _dot(m_ref[...], w_ref[...])
    h = _ln(x_ref[...], g1_ref[...], b1_ref[...])
    h2 = _ln(DEEPNORM_ALPHA * h + y, g2_ref[...], b2_ref[...])
    h2_ref[...] = h2.astype(h2_ref.dtype)
    h2t_ref[...] = h2.T.astype(h2t_ref.dtype)


def _outproj(merged, x2, g1, b1, g2, b2, w_out):
    n = merged.shape[0]
    tm = min(n, 512)
    vec = pl.BlockSpec((1, D_MODEL), lambda i: (0, 0))
    return pl.pallas_call(
        _outproj_kernel,
        grid=(n // tm,),
        in_specs=[
            pl.BlockSpec((tm, D_MODEL), lambda i: (i, 0)),
            pl.BlockSpec((tm, D_MODEL), lambda i: (i, 0)),
            vec, vec, vec, vec,
            pl.BlockSpec((D_MODEL, D_MODEL), lambda i: (0, 0)),
        ],
        out_specs=[
            pl.BlockSpec((tm, D_MODEL), lambda i: (i, 0)),
            pl.BlockSpec((D_MODEL, tm), lambda i: (0, i)),
        ],
        out_shape=[
            jax.ShapeDtypeStruct((n, D_MODEL), BF16),
            jax.ShapeDtypeStruct((D_MODEL, n), BF16),
        ],
        compiler_params=_params(("parallel",)),
        name="outproj_ln",
    )(merged, x2, g1, b1, g2, b2, w_out)


def _peer_wcomb_kernel(keys_ref, wpq_ref, o_ref):
    o_ref[...] = lax.dot_general(keys_ref[0].astype(BF16), wpq_ref[...], (((1,), (1,)), ((), ())),
                                 preferred_element_type=F32).astype(o_ref.dtype)


def _peer_wcomb(sub_keys, w_pq):
    nhc = PEER_HEADS * 2
    keys = sub_keys.reshape(nhc, N_KEYS, PEER_KEY_HALF)
    return pl.pallas_call(
        _peer_wcomb_kernel,
        grid=(nhc,),
        in_specs=[
            pl.BlockSpec((1, N_KEYS, PEER_KEY_HALF), lambda i: (i, 0, 0)),
            pl.BlockSpec((D_MODEL, PEER_KEY_HALF), lambda i: (0, i)),
        ],
        out_specs=pl.BlockSpec((N_KEYS, D_MODEL), lambda i: (i, 0)),
        out_shape=jax.ShapeDtypeStruct((nhc * N_KEYS, D_MODEL), BF16),
        compiler_params=_params(("parallel",)),
        name="peer_wcomb",
    )(keys, w_pq)


PEER_TOK = 1024
PEER_THRESH_TOK = 2048


def _peer_scores_kernel(w_ref, xt_ref, s3_ref):
    rows = w_ref.shape[0] // 2
    for g in range(2):
        st = _dot(w_ref[g * rows:(g + 1) * rows, :], xt_ref[...])
        for c in range(s3_ref.shape[1]):
            s3_ref[g * rows:(g + 1) * rows, c, :] = st[:, c * LANES:(c + 1) * LANES]


def _peer_scores(wcomb, h2t):
    n = h2t.shape[1]
    tl = min(n, PEER_TOK)
    rows = wcomb.shape[0]
    return pl.pallas_call(
        _peer_scores_kernel,
        grid=(n // tl,),
        in_specs=[
            pl.BlockSpec((rows, D_MODEL), lambda i: (0, 0)),
            pl.BlockSpec((D_MODEL, tl), lambda i: (0, i)),
        ],
        out_specs=pl.BlockSpec((rows, tl // LANES, LANES), lambda i: (0, i, 0)),
        out_shape=jax.ShapeDtypeStruct((rows, n // LANES, LANES), F32),
        compiler_params=_params(("parallel",)),
        name="peer_scores",
    )(wcomb, h2t)


PEER_LIST = PEER_TOPK + 1
PEER_SUB = 4 * N_KEYS


def _insert_sorted(t, v):
    out = []
    for r in range(len(t)):
        out.append(jnp.maximum(t[r], v))
        v = jnp.minimum(t[r], v)
    return out


PEER_ROW = 4


def _count_above(vals, x, strict):
    above = (lambda v: v > x) if strict else (lambda v: v >= x)
    pick = lambda m, hi, lo: jnp.where(m, hi, lo)
    m8 = above(vals[7])
    m4 = above(pick(m8, vals[11], vals[3]))
    m2 = above(pick(m8, pick(m4, vals[13], vals[9]), pick(m4, vals[5], vals[1])))
    hi = pick(m4, pick(m2, vals[14], vals[12]), pick(m2, vals[10], vals[8]))
    lo = pick(m4, pick(m2, vals[6], vals[4]), pick(m2, vals[2], vals[0]))
    m1 = above(pick(m8, hi, lo))
    cnt = (pick(m8, 8.0, 0.0) + pick(m4, 4.0, 0.0)) + (pick(m2, 2.0, 0.0) + pick(m1, 1.0, 0.0))
    return pick(above(vals[15]), 16.0, cnt)


def _peer_thresh_kernel(s_ref, c_ref, e1_ref, q_ref, e2_ref, q_scr, e2_scr):
    shape = s_ref.shape[1:]
    neg = jnp.full(shape, -jnp.inf, F32)

    def top_list(base):
        def body(n, t):
            return tuple(_insert_sorted(list(t), s_ref[base + n]))
        return lax.fori_loop(0, N_KEYS, body, (neg,) * PEER_LIST, unroll=4)

    a = top_list(0)
    b = top_list(N_KEYS)
    t = [neg] * PEER_LIST
    for i in range(1, PEER_LIST + 1):
        for j in range(1, PEER_LIST // i + 1):
            t = _insert_sorted(t, a[i - 1] + b[j - 1])
    tau = 0.5 * (t[PEER_TOPK - 1] + t[PEER_TOPK])
    top = a[0] + b[0]
    z = jnp.exp(t[0] - top)
    for r in range(1, PEER_TOPK):
        z = z + jnp.exp(t[r] - top)
    cv = b[0] + jnp.log(2.0 * z)
    a16 = a[:PEER_TOPK]

    def write(n, carry):
        s1 = s_ref[n]
        s2 = s_ref[N_KEYS + n]
        c = 1.0 + _count_above(a16, s1, True)
        e1 = jnp.exp(s1 - a16[0])
        for gq in range(shape[0] // PEER_ROW):
            c_ref[n, gq] = c[gq * PEER_ROW:(gq + 1) * PEER_ROW, :]
            e1_ref[n, gq] = e1[gq * PEER_ROW:(gq + 1) * PEER_ROW, :]
        row0 = pl.multiple_of(n * nsub, nsub)
        q_scr[pl.ds(row0, nsub), :] = _count_above(a16, tau - s2, False)
        e2_scr[pl.ds(row0, nsub), :] = jnp.exp(s2 - cv)
        return carry

    nsub = shape[0]
    lax.fori_loop(0, N_KEYS, write, 0, unroll=2)
    for c in range(nsub):
        q_ref[:, c * LANES:(c + 1) * LANES] = q_scr[pl.ds(c, N_KEYS, stride=nsub), :].astype(q_ref.dtype)
        e2_ref[:, c * LANES:(c + 1) * LANES] = e2_scr[pl.ds(c, N_KEYS, stride=nsub), :].astype(e2_ref.dtype)


def _peer_thresh(s3):
    rows, nl, _ = s3.shape
    n = nl * LANES
    nb = min(PEER_THRESH_TOK // LANES, nl)
    half = rows // 2
    ng = nb // PEER_ROW
    row_spec = pl.BlockSpec((N_KEYS, ng, PEER_ROW, LANES), lambda i, h: (h, i, 0, 0))
    row_shape = jax.ShapeDtypeStruct((half, nl // PEER_ROW, PEER_ROW, LANES), F32)
    tile_spec = pl.BlockSpec((N_KEYS, nb * LANES), lambda i, h: (h, i))
    tile_shape = jax.ShapeDtypeStruct((half, n), BF16)
    c, e1, q, e2 = pl.pallas_call(
        _peer_thresh_kernel,
        grid=(nl // nb, PEER_HEADS),
        in_specs=[pl.BlockSpec((2 * N_KEYS, nb, LANES), lambda i, h: (h, i, 0))],
        out_specs=[row_spec, row_spec, tile_spec, tile_spec],
        out_shape=[row_shape, row_shape, tile_shape, tile_shape],
        scratch_shapes=[pltpu.VMEM((N_KEYS * nb, LANES), F32), pltpu.VMEM((N_KEYS * nb, LANES), F32)],
        compiler_params=_params(("parallel", "parallel")),
        name="peer_thresh",
    )(s3)
    shape5 = (PEER_HEADS, N_KEYS, nl // PEER_ROW, PEER_ROW, LANES)
    tiles = lambda v: v.reshape(PEER_HEADS, N_KEYS, n)
    return c.reshape(shape5), e1.reshape(shape5), tiles(q), tiles(e2)


def _peer_dense_kernel(xt_ref, u_ref, vt_ref, q_ref, e2_ref, c_ref, e1_ref, h2_ref, g_ref, b_ref,
                       o_ref, acc_scr, a_scr, *, sub):
    e = pl.program_id(1)
    eb = u_ref.shape[0]

    @pl.when(e == 0)
    def _():
        zrows = 256

        def zero_rows(r, carry):
            acc_scr[pl.ds(pl.multiple_of(r * zrows, zrows), zrows), :] = jnp.zeros((zrows, acc_scr.shape[1]), F32)
            return carry

        lax.fori_loop(0, acc_scr.shape[0] // zrows, zero_rows, 0)

    def row(ref, h, il):
        return jnp.concatenate([ref[h, il, 0, c:c + 1, :] for c in range(ref.shape[3])], axis=1).astype(BF16)

    zero = jnp.zeros((), BF16)

    xt = xt_ref[...]
    nsub = eb // sub
    score = lambda s: _dot(u_ref[s * sub:(s + 1) * sub, :], xt)
    st_next = score(0)
    for s in range(nsub):
        st, st_next = st_next, (score(s + 1) if s + 1 < nsub else None)
        for ii in range(sub // N_KEYS):
            il = s * (sub // N_KEYS) + ii
            gate = None
            for h in range(PEER_HEADS):
                term = row(e1_ref, h, il) * jnp.where(q_ref[h] >= row(c_ref, h, il), e2_ref[h], zero)
                gate = term if gate is None else gate + term
            a = _gelu_x2(st[ii * N_KEYS:(ii + 1) * N_KEYS, :]).astype(BF16) * gate
            a_scr[s * sub + ii * N_KEYS:s * sub + (ii + 1) * N_KEYS, :] = a
        acc_scr[...] += _dot(vt_ref[:, s * sub:(s + 1) * sub], a_scr[s * sub:(s + 1) * sub, :])

    @pl.when(e == pl.num_programs(1) - 1)
    def _():
        y = acc_scr[...].T
        o_ref[...] = _ln(DEEPNORM_ALPHA * h2_ref[...].astype(F32) + y, g_ref[...], b_ref[...])


def _peer_dense(h2, h2t, q, e2, c, e1, u_tab, vt_tab, ln_g, ln_b):
    n = h2.shape[0]
    t = min(n, PEER_ROW * LANES)
    eb = 8 * N_KEYS
    sub = PEER_SUB
    vec = pl.BlockSpec((1, D_MODEL), lambda i, e: (0, 0))
    row_spec = pl.BlockSpec((PEER_HEADS, eb // N_KEYS, 1, PEER_ROW, LANES), lambda i, e: (0, e, i, 0, 0))
    return pl.pallas_call(
        functools.partial(_peer_dense_kernel, sub=sub),
        grid=(n // t, N_EXPERTS // eb),
        in_specs=[
            pl.BlockSpec((D_MODEL, t), lambda i, e: (0, i)),
            pl.BlockSpec((eb, D_MODEL), lambda i, e: (e, 0)),
            pl.BlockSpec((D_MODEL, eb), lambda i, e: (0, e)),
            pl.BlockSpec((PEER_HEADS, N_KEYS, t), lambda i, e: (0, 0, i)),
            pl.BlockSpec((PEER_HEADS, N_KEYS, t), lambda i, e: (0, 0, i)),
            row_spec, row_spec,
            pl.BlockSpec((t, D_MODEL), lambda i, e: (i, 0)),
            vec, vec,
        ],
        out_specs=pl.BlockSpec((t, D_MODEL), lambda i, e: (i, 0)),
        out_shape=jax.ShapeDtypeStruct((n, D_MODEL), F32),
        scratch_shapes=[
            pltpu.VMEM((D_MODEL, t), F32),
            pltpu.VMEM((eb, t), BF16),
        ],
        compiler_params=_params(("parallel", "arbitrary")),
        name="peer_dense",
    )(h2t, u_tab, vt_tab, q, e2, c, e1, h2, ln_g, ln_b)


def _rot_half_cols(w):
    half = QK_ROPE // 2
    return jnp.concatenate([-w[..., half:], w[..., :half]], axis=-1)


def kernel(x, ln_in_g, ln_in_b, w_in, q_norm_g, w_uq, kv_norm_g, w_ukv, ssm_lam_re, ssm_lam_im, ssm_log_dt,
           ssm_b_re, ssm_b_im, ssm_c_re, ssm_c_im, ssm_d, w_glu, w_proj_attn, w_proj_ssm, w_out, ln_mix_g, ln_mix_b,
           w_peer_q, peer_sub_keys, peer_u, peer_v, ln_ffn_g, ln_ffn_b):
    batch, seq, _ = x.shape
    n = batch * seq
    x2 = x.reshape(n, D_MODEL)
    row = lambda v: v.reshape(1, -1).astype(F32)
    l = 0

    wi = w_in[l]
    w_main = jnp.concatenate([wi[:, OFF_SSM:OFF_KR], wi[:, OFF_GA:IN_WIDTH]], axis=1).astype(BF16)
    w_krc = wi[:, OFF_KR:OFF_GA]
    w_kr = jnp.concatenate([w_krc, _rot_half_cols(w_krc)], axis=1).astype(BF16)
    wq = w_uq[l].reshape(Q_LORA, N_HEADS, QK_HEAD)
    wq_ext = jnp.concatenate([wq, _rot_half_cols(wq[..., QK_NOPE:])], axis=-1)
    wq_ext = wq_ext.transpose(1, 0, 2).astype(BF16)
    wkv = w_ukv[l].reshape(KV_LORA, N_HEADS, QK_NOPE + V_HEAD).transpose(1, 0, 2).astype(BF16)

    cs = _rope_table(seq)
    u, z, kr = _inproj(x2, row(ln_in_g), row(ln_in_b), w_main, w_kr)

    q, k, v = _qkv(z, kr, cs, row(q_norm_g[l]), row(kv_norm_g[l]), wq_ext, wkv, batch, seq)
    attn = _attention(q, k, v).reshape(n, N_HEADS * V_HEAD)

    w, ws, wx, a16 = _ssm_wgen(ssm_lam_re[l], ssm_lam_im[l], ssm_log_dt[l], ssm_b_re[l], ssm_b_im[l],
                               ssm_c_re[l], ssm_c_im[l])
    g = _ssm_branch(u, batch, seq, w, ws, wx, a16, row(ssm_d[l]))

    merged = _merge(attn, g, z, w_glu[l].astype(BF16), w_proj_attn[l].astype(BF16), w_proj_ssm[l].astype(BF16))
    h2, h2t = _outproj(merged, x2, row(ln_in_g), row(ln_in_b), row(ln_mix_g[l]), row(ln_mix_b[l]),
                       w_out[l].astype(BF16))

    wcomb = _peer_wcomb(peer_sub_keys[l], w_peer_q[l].astype(BF16))
    s3 = _peer_scores(wcomb, h2t)
    c, e1, q, e2 = _peer_thresh(s3)
    out = _peer_dense(h2, h2t, q, e2, c, e1, peer_u[l].astype(BF16), peer_v[l].T.astype(BF16),
                      row(ln_ffn_g[l]), row(ln_ffn_b[l]))
    return out.reshape(batch, seq, D_MODEL)
```

```python
import functools
import math

import jax
import jax.numpy as jnp
from jax import lax
from jax.experimental import pallas as pl
from jax.experimental.pallas import tpu as pltpu

F32 = jnp.float32
BF16 = jnp.bfloat16

D_MODEL = 2048
DEPTH = 1
CHUNK = 64
SSM_WIDTH = 1024
SSM_GROUP = 16
SSM_GROUPS = SSM_WIDTH // SSM_GROUP
SSM_STATE = 64
N_HEADS = 16
Q_LORA = 512
KV_LORA = 512
QK_NOPE = 128
QK_ROPE = 64
V_HEAD = 128
QK_HEAD = QK_NOPE + QK_ROPE
ROPE_THETA = 10000.0
PEER_HEADS = 8
N_KEYS = 128
N_EXPERTS = N_KEYS * N_KEYS
PEER_KEY_HALF = 128
PEER_TOPK = 16
LN_EPS = 1e-5
RMS_EPS = 1e-6
DEEPNORM_ALPHA = (2.0 * DEPTH) ** 0.25
OFF_SSM = 0
OFF_Q = OFF_SSM + SSM_WIDTH
OFF_KV = OFF_Q + Q_LORA
OFF_KR = OFF_KV + KV_LORA
OFF_GA = OFF_KR + QK_ROPE
OFF_GS = OFF_GA + D_MODEL
IN_WIDTH = OFF_GS + D_MODEL

LANES = 128
VMEM_LIMIT_BYTES = 56 * 1024 * 1024

SSM_CHUNK = 16
SSM_TILE_GROUPS = LANES // SSM_GROUP
SSM_TILES = SSM_WIDTH // LANES
SSM_TILE_STATE = SSM_TILE_GROUPS * SSM_STATE
NEG_BIG = -1e30


def _params(semantics):
    return pltpu.CompilerParams(dimension_semantics=semantics, vmem_limit_bytes=VMEM_LIMIT_BYTES)


def _dot(a, b):
    return jnp.dot(a, b, preferred_element_type=F32)


def _ln(x, g, b):
    mu = jnp.mean(x, axis=-1, keepdims=True)
    xc = x - mu
    var = jnp.mean(xc * xc, axis=-1, keepdims=True)
    return xc * lax.rsqrt(var + LN_EPS) * g + b


GELU_C = 0.7978845608028654
GELU_A = 0.044715


def _gelu(x):
    return 0.5 * x * (1.0 + jnp.tanh(GELU_C * (x + GELU_A * (x * x * x))))


def _gelu_x2(x):
    return x * (1.0 + jnp.tanh(x * (GELU_C + (GELU_C * GELU_A) * (x * x))))


def _sigmoid(x):
    return 1.0 / (1.0 + jnp.exp(-x))


def _rope_table_kernel(invf_ref, o_ref):
    ts = o_ref.shape[0]
    pos = (lax.broadcasted_iota(jnp.int32, (ts, LANES), 0) + pl.program_id(0) * ts).astype(F32)
    ang = pos * invf_ref[...]
    lane = lax.broadcasted_iota(jnp.int32, (ts, LANES), 1)
    o_ref[...] = jnp.where(lane < 2 * (QK_ROPE // 2), jnp.cos(ang), jnp.sin(ang))


def _rope_table(seq):
    half = QK_ROPE // 2
    invf = 1.0 / (ROPE_THETA ** (jnp.arange(half, dtype=F32) / half))
    invf4 = jnp.tile(invf, 4)[None, :]
    ts = min(seq, 1024)
    return pl.pallas_call(
        _rope_table_kernel,
        grid=(seq // ts,),
        in_specs=[pl.BlockSpec((1, LANES), lambda i: (0, 0))],
        out_specs=pl.BlockSpec((ts, LANES), lambda i: (i, 0)),
        out_shape=jax.ShapeDtypeStruct((seq, LANES), F32),
        compiler_params=_params(("parallel",)),
        name="rope_table",
    )(invf4)


def _inproj_kernel(x_ref, g_ref, b_ref, w_ref, wkr_ref, u_ref, z_ref, kr_ref, h_scr):
    j = pl.program_id(1)

    @pl.when(j == 0)
    def _():
        h = _ln(x_ref[...], g_ref[...], b_ref[...]).astype(BF16)
        h_scr[...] = h
        u_ref[...] = _dot(h, w_ref[...])
        kr_ref[...] = _dot(h, wkr_ref[...]).astype(kr_ref.dtype)

    @pl.when(j > 0)
    def _():
        z_ref[...] = _dot(h_scr[...], w_ref[...]).astype(z_ref.dtype)


def _inproj(x2, ln_g, ln_b, w_main, w_kr):
    n = x2.shape[0]
    tm = min(n, 1024)
    tn = 1024
    nj = w_main.shape[1] // tn
    return pl.pallas_call(
        _inproj_kernel,
        grid=(n // tm, nj),
        in_specs=[
            pl.BlockSpec((tm, D_MODEL), lambda i, j: (i, 0)),
            pl.BlockSpec((1, D_MODEL), lambda i, j: (0, 0)),
            pl.BlockSpec((1, D_MODEL), lambda i, j: (0, 0)),
            pl.BlockSpec((D_MODEL, tn), lambda i, j: (0, j)),
            pl.BlockSpec((D_MODEL, LANES), lambda i, j: (0, 0)),
        ],
        out_specs=[
            pl.BlockSpec((tm, tn), lambda i, j: (i, 0)),
            pl.BlockSpec((tm, tn), lambda i, j: (i, jnp.maximum(j - 1, 0))),
            pl.BlockSpec((tm, LANES), lambda i, j: (i, 0)),
        ],
        out_shape=[
            jax.ShapeDtypeStruct((n, tn), F32),
            jax.ShapeDtypeStruct((n, (nj - 1) * tn), BF16),
            jax.ShapeDtypeStruct((n, LANES), BF16),
        ],
        scratch_shapes=[pltpu.VMEM((tm, D_MODEL), BF16)],
        compiler_params=_params(("parallel", "arbitrary")),
        name="ln_inproj",
    )(x2, ln_g, ln_b, w_main, w_kr)


def _rope_mix(t2, cs):
    t = t2 * cs
    return t + pltpu.roll(t, QK_ROPE, 1)


def _qkv_kernel(cq_ref, ckv_ref, kr_ref, cs_ref, gq_ref, gkv_ref, wq_ref, wkv_ref,
                q_ref, k_ref, v_ref, cqn_scr, ckvn_scr, kro_scr):
    cs = cs_ref[...]

    @pl.when(pl.program_id(1) == 0)
    def _():
        def rms(c, g):
            c = c.astype(F32)
            return (c * lax.rsqrt(jnp.mean(c * c, axis=-1, keepdims=True) + RMS_EPS) * g).astype(BF16)

        cqn_scr[...] = rms(cq_ref[...], gq_ref[...])
        ckvn_scr[...] = rms(ckv_ref[...], gkv_ref[...])
        kro_scr[...] = _rope_mix(kr_ref[...].astype(F32), cs).astype(BF16)

    scale = math.log2(math.e) / math.sqrt(QK_HEAD)
    hp = wq_ref.shape[0]
    wide = 2 * LANES
    qe2 = _dot(cqn_scr[...], jnp.concatenate([wq_ref[j] for j in range(hp)], axis=1))
    kv2 = _dot(ckvn_scr[...], jnp.concatenate([wkv_ref[j] for j in range(hp)], axis=1))
    for j in range(hp):
        qe = qe2[:, j * wide:(j + 1) * wide]
        qr = _rope_mix(qe[:, QK_NOPE:], cs)
        q_ref[0, j, :, :QK_NOPE] = (qe[:, :QK_NOPE] * scale).astype(BF16)
        q_ref[0, j, :, QK_NOPE:] = (qr[:, :QK_ROPE] * scale).astype(BF16)
        kv = kv2[:, j * wide:(j + 1) * wide]
        k_ref[0, j, :, :QK_NOPE] = kv[:, :QK_NOPE].astype(BF16)
        k_ref[0, j, :, QK_NOPE:] = kro_scr[:, :QK_ROPE]
        v_ref[0, j] = kv[:, QK_NOPE:].astype(BF16)


def _qkv(z, kr, cs, gq, gkv, wq_ext, wkv, batch, seq):
    n = z.shape[0]
    tm = min(seq, 2048)
    nsb = seq // tm
    hp = 2
    return pl.pallas_call(
        _qkv_kernel,
        grid=(n // tm, N_HEADS // hp),
        in_specs=[
            pl.BlockSpec((tm, Q_LORA), lambda i, h: (i, 0)),
            pl.BlockSpec((tm, KV_LORA), lambda i, h: (i, 1)),
            pl.BlockSpec((tm, LANES), lambda i, h: (i, 0)),
            pl.BlockSpec((tm, LANES), lambda i, h: (i % nsb, 0)),
            pl.BlockSpec((1, Q_LORA), lambda i, h: (0, 0)),
            pl.BlockSpec((1, KV_LORA), lambda i, h: (0, 0)),
            pl.BlockSpec((hp, Q_LORA, 2 * LANES), lambda i, h: (h, 0, 0)),
            pl.BlockSpec((hp, KV_LORA, 2 * LANES), lambda i, h: (h, 0, 0)),
        ],
        out_specs=[
            pl.BlockSpec((1, hp, tm, QK_HEAD), lambda i, h: (i // nsb, h, i % nsb, 0)),
            pl.BlockSpec((1, hp, tm, QK_HEAD), lambda i, h: (i // nsb, h, i % nsb, 0)),
            pl.BlockSpec((1, hp, tm, V_HEAD), lambda i, h: (i // nsb, h, i % nsb, 0)),
        ],
        out_shape=[
            jax.ShapeDtypeStruct((batch, N_HEADS, seq, QK_HEAD), BF16),
            jax.ShapeDtypeStruct((batch, N_HEADS, seq, QK_HEAD), BF16),
            jax.ShapeDtypeStruct((batch, N_HEADS, seq, V_HEAD), BF16),
        ],
        scratch_shapes=[
            pltpu.VMEM((tm, Q_LORA), BF16),
            pltpu.VMEM((tm, KV_LORA), BF16),
            pltpu.VMEM((tm, LANES), BF16),
        ],
        compiler_params=_params(("parallel", "arbitrary")),
        name="qkv_rope",
    )(z, z, kr, cs, gq, gkv, wq_ext, wkv)


ATT_ROWS = 512
ATT_CHAINS = 8
ATT_BK_MAIN = 1024


def _attn_kernel(q_ref, k_ref, v_ref, o_ref, m_scr, acc_scr, *, rows, chains, bkm):
    qi = pl.program_id(2)
    m_scr[...] = jnp.full(m_scr.shape, -jnp.inf, F32)
    acc_scr[...] = jnp.zeros(acc_scr.shape, F32)

    def load_kv(j, bk):
        start = pl.multiple_of(j * bk, bk)
        kj = k_ref[0, 0, pl.ds(start, bk), :]
        vj = jnp.concatenate([v_ref[0, 0, pl.ds(start, bk), :], jnp.ones((bk, LANES), BF16)], axis=1)
        return kj, vj

    def block(c, kj, vj, masked):
        q = q_ref[0, 0, c * rows:(c + 1) * rows, :]
        s = lax.dot_general(q, kj, (((1,), (1,)), ((), ())), preferred_element_type=F32)
        if masked:
            rc = lax.broadcasted_iota(jnp.int32, s.shape, 0) // CHUNK
            cc = lax.broadcasted_iota(jnp.int32, s.shape, 1) // CHUNK
            s = jnp.where(cc <= rc, s, NEG_BIG)
        m_prev = m_scr[c]
        m_new = jnp.maximum(m_prev, jnp.max(s, axis=-1, keepdims=True))
        alpha = jnp.exp2(m_prev - m_new)
        p = jnp.exp2(s - jnp.tile(m_new, (1, s.shape[1] // LANES)))
        acc_scr[c] = jnp.tile(alpha, (1, 2)) * acc_scr[c] + _dot(p.astype(BF16), vj)
        m_scr[c] = m_new

    per_step = chains * rows // bkm
    group = 2 if per_step % 2 == 0 else 1

    def body(j, carry):
        for jj in range(group):
            kj, vj = load_kv(j * group + jj, bkm)
            for c in range(chains):
                block(c, kj, vj, False)
        return carry

    lax.fori_loop(0, qi * (per_step // group), body, 0)
    for jj in range(chains):
        kj, vj = load_kv(qi * chains + jj, rows)
        for c in range(jj, chains):
            block(c, kj, vj, c == jj)
    for c in range(chains):
        a = acc_scr[c]
        o_ref[0, c * rows:(c + 1) * rows, :] = (a[:, :V_HEAD] / a[:, V_HEAD:]).astype(o_ref.dtype)


def _attention(q, k, v):
    batch, _, seq, _ = q.shape
    rows = min(seq, ATT_ROWS)
    chains = min(ATT_CHAINS, seq // rows)
    bq = rows * chains
    bkm = min(ATT_BK_MAIN, bq)
    return pl.pallas_call(
        functools.partial(_attn_kernel, rows=rows, chains=chains, bkm=bkm),
        grid=(batch, N_HEADS, seq // bq),
        in_specs=[
            pl.BlockSpec((1, 1, bq, QK_HEAD), lambda b, h, i: (b, h, i, 0)),
            pl.BlockSpec((1, 1, seq, QK_HEAD), lambda b, h, i: (b, h, 0, 0)),
            pl.BlockSpec((1, 1, seq, V_HEAD), lambda b, h, i: (b, h, 0, 0)),
        ],
        out_specs=pl.BlockSpec((1, bq, V_HEAD), lambda b, h, i: (b, i, h)),
        out_shape=jax.ShapeDtypeStruct((batch, seq, N_HEADS * V_HEAD), BF16),
        scratch_shapes=[
            pltpu.VMEM((chains, rows, LANES), F32),
            pltpu.VMEM((chains, rows, 2 * V_HEAD), F32),
        ],
        compiler_params=_params(("parallel", "parallel", "arbitrary")),
        name="flash_attn",
    )(q, k, v)


def _cpow(lr, li, ldt, k):
    dt = jnp.exp(ldt)
    mag = jnp.exp(lr * dt * k)
    ang = li * dt * k
    return mag * jnp.cos(ang), mag * jnp.sin(ang)


def _ssm_wgen_kernel(lr_r, li_r, ldt_r, bre_ref, bim_ref, cre_ref, cim_ref,
                     w_ref, ws_ref, wx_ref, a16_ref):
    hp = lax.Precision.HIGHEST
    ns = SSM_TILE_STATE
    lr, li, ldt = lr_r[0], li_r[0], ldt_r[0]
    a_re, a_im = _cpow(lr, li, ldt, 1.0)
    den = lr * lr + li * li
    nr = a_re - 1.0
    ni = a_im
    cf_re = (nr * lr + ni * li) / den
    cf_im = (ni * lr - nr * li) / den
    bre, bim = bre_ref[0], bim_ref[0]
    cre, cim = cre_ref[0], cim_ref[0]
    w_ref[...] = jnp.zeros(w_ref.shape, w_ref.dtype)
    for k in range(SSM_CHUNK):
        p_re, p_im = _cpow(lr, li, ldt, float(k))
        m_re = cf_re * p_re - cf_im * p_im
        m_im = cf_re * p_im + cf_im * p_re
        bm_re = bre * m_re - bim * m_im
        bm_im = bre * m_im + bim * m_re
        bd = (jnp.dot(bm_re, cre, precision=hp, preferred_element_type=F32)
              - jnp.dot(bm_im, cim, precision=hp, preferred_element_type=F32)).astype(BF16)
        for j in range(SSM_CHUNK - k):
            i = j + k
            w_ref[0, j * LANES:(j + 1) * LANES, i * LANES:(i + 1) * LANES] = bd
        jj = SSM_CHUNK - 1 - k
        ws_ref[0, jj * LANES:(jj + 1) * LANES, :ns] = bm_re.astype(BF16)
        ws_ref[0, jj * LANES:(jj + 1) * LANES, ns:] = bm_im.astype(BF16)
    cre_t, cim_t = cre.T, cim.T
    for i in range(SSM_CHUNK):
        p_re, p_im = _cpow(lr, li, ldt, float(i + 1))
        wx_ref[0, :ns, i * LANES:(i + 1) * LANES] = (p_re * cre_t - p_im * cim_t).T.astype(BF16)
        wx_ref[0, ns:, i * LANES:(i + 1) * LANES] = (-(p_re * cim_t + p_im * cre_t)).T.astype(BF16)
    q_re, q_im = _cpow(lr, li, ldt, float(SSM_CHUNK))
    a16_ref[0, :, :ns] = q_re
    a16_ref[0, :, ns:] = q_im


def _ssm_wgen(lam_re, lam_im, log_dt, b_re, b_im, c_re, c_im):
    nt, tg, ns = SSM_TILES, SSM_TILE_GROUPS, SSM_TILE_STATE
    eye = jnp.eye(tg, dtype=F32)
    bbd = lambda b: jnp.einsum("cgph,gk->cghkp", b.reshape(nt, tg, SSM_STATE, SSM_GROUP), eye).reshape(nt, LANES, ns)
    cbd = lambda c: jnp.einsum("cghp,gk->cgpkh", c.reshape(nt, tg, SSM_GROUP, SSM_STATE), eye).reshape(nt, ns, LANES)
    ldt = jnp.repeat(log_dt, SSM_STATE)
    rows = [a.reshape(nt, 1, ns) for a in (lam_re, lam_im, ldt)]
    kc = SSM_CHUNK * LANES
    row_spec = pl.BlockSpec((1, 1, ns), lambda c: (c, 0, 0))
    return pl.pallas_call(
        _ssm_wgen_kernel,
        grid=(nt,),
        in_specs=[row_spec] * 3 + [
            pl.BlockSpec((1, LANES, ns), lambda c: (c, 0, 0)),
            pl.BlockSpec((1, LANES, ns), lambda c: (c, 0, 0)),
            pl.BlockSpec((1, ns, LANES), lambda c: (c, 0, 0)),
            pl.BlockSpec((1, ns, LANES), lambda c: (c, 0, 0)),
        ],
        out_specs=[
            pl.BlockSpec((1, kc, kc), lambda c: (c, 0, 0)),
            pl.BlockSpec((1, kc, 2 * ns), lambda c: (c, 0, 0)),
            pl.BlockSpec((1, 2 * ns, kc), lambda c: (c, 0, 0)),
            pl.BlockSpec((1, 1, 2 * ns), lambda c: (c, 0, 0)),
        ],
        out_shape=[
            jax.ShapeDtypeStruct((nt, kc, kc), BF16),
            jax.ShapeDtypeStruct((nt, kc, 2 * ns), BF16),
            jax.ShapeDtypeStruct((nt, 2 * ns, kc), BF16),
            jax.ShapeDtypeStruct((nt, 1, 2 * ns), F32),
        ],
        compiler_params=_params(("parallel",)),
        name="ssm_wgen",
    )(*rows, bbd(b_re), bbd(b_im), cbd(c_re), cbd(c_im))


def _chunk_rows(u_ref, tb):
    return [u_ref[pl.ds(j, tb, stride=SSM_CHUNK), :] for j in range(SSM_CHUNK)]


def _ssm_sum_kernel(u_ref, ws_ref, s_ref, *, tb):
    u16 = jnp.concatenate(_chunk_rows(u_ref, tb), axis=1).astype(BF16)
    s_ref[0] = _dot(u16, ws_ref[0])


def _ssm_scan_kernel(s_ref, a16_ref, x_ref):
    ns = SSM_TILE_STATE
    a_re = a16_ref[0, :, :ns]
    a_im = a16_ref[0, :, ns:]
    nrow = s_ref.shape[1]

    def body(r, carry):
        xr, xi = carry
        x_ref[0, pl.ds(r, 1), :ns] = xr
        x_ref[0, pl.ds(r, 1), ns:] = xi
        sr = s_ref[0, pl.ds(r, 1), :ns]
        si = s_ref[0, pl.ds(r, 1), ns:]
        return a_re * xr - a_im * xi + sr, a_re * xi + a_im * xr + si

    zero = jnp.zeros((1, ns), F32)
    lax.fori_loop(0, nrow, body, (zero, zero), unroll=8)


def _ssm_out_kernel(u_ref, x_ref, w_ref, wx_ref, d_ref, g_ref, *, tb):
    us = _chunk_rows(u_ref, tb)
    u16 = jnp.concatenate(us, axis=1).astype(BF16)
    y16 = _dot(u16, w_ref[0]) + _dot(x_ref[0].astype(BF16), wx_ref[0])
    d = d_ref[...]
    for i in range(SSM_CHUNK):
        y = y16[:, i * LANES:(i + 1) * LANES] + d * us[i]
        g_ref[pl.ds(i, tb, stride=SSM_CHUNK), :] = _gelu(y)


def _ssm_branch(u, batch, seq, w, ws, wx, a16, d_skip):
    n = u.shape[0]
    nt, ns = SSM_TILES, SSM_TILE_STATE
    kc = SSM_CHUNK * LANES
    nb = n // SSM_CHUNK
    nbs = seq // SSM_CHUNK
    tb = min(nbs, 512)
    tok = tb * SSM_CHUNK
    s = pl.pallas_call(
        functools.partial(_ssm_sum_kernel, tb=tb),
        grid=(nt, nb // tb),
        in_specs=[
            pl.BlockSpec((tok, LANES), lambda c, r: (r, c)),
            pl.BlockSpec((1, kc, 2 * ns), lambda c, r: (c, 0, 0)),
        ],
        out_specs=pl.BlockSpec((1, tb, 2 * ns), lambda c, r: (c, r, 0)),
        out_shape=jax.ShapeDtypeStruct((nt, nb, 2 * ns), F32),
        compiler_params=_params(("parallel", "parallel")),
        name="ssm_chunk_sum",
    )(u, ws)
    x = pl.pallas_call(
        _ssm_scan_kernel,
        grid=(nt, batch),
        in_specs=[
            pl.BlockSpec((1, nbs, 2 * ns), lambda c, b: (c, b, 0)),
            pl.BlockSpec((1, 1, 2 * ns), lambda c, b: (c, 0, 0)),
        ],
        out_specs=pl.BlockSpec((1, nbs, 2 * ns), lambda c, b: (c, b, 0)),
        out_shape=jax.ShapeDtypeStruct((nt, nb, 2 * ns), F32),
        compiler_params=_params(("parallel", "parallel")),
        name="ssm_state_scan",
    )(s, a16)
    return pl.pallas_call(
        functools.partial(_ssm_out_kernel, tb=tb),
        grid=(nt, nb // tb),
        in_specs=[
            pl.BlockSpec((tok, LANES), lambda c, r: (r, c)),
            pl.BlockSpec((1, tb, 2 * ns), lambda c, r: (c, r, 0)),
            pl.BlockSpec((1, kc, kc), lambda c, r: (c, 0, 0)),
            pl.BlockSpec((1, 2 * ns, kc), lambda c, r: (c, 0, 0)),
            pl.BlockSpec((1, LANES), lambda c, r: (0, c)),
        ],
        out_specs=pl.BlockSpec((tok, LANES), lambda c, r: (r, c)),
        out_shape=jax.ShapeDtypeStruct((n, SSM_WIDTH), F32),
        compiler_params=_params(("parallel", "parallel")),
        name="ssm_chunk_out",
    )(u, x, w, wx, d_skip)


def _merge_kernel(attn_ref, g_ref, ga_ref, gs_ref, wglu_ref, wpa_ref, wps_ref, o_ref, ssm_scr):
    @pl.when(pl.program_id(1) == 0)
    def _():
        g = g_ref[...]
        ssm_scr[...] = (g * _sigmoid(_dot(g.astype(BF16), wglu_ref[...]))).astype(BF16)

    pa = _dot(attn_ref[...], wpa_ref[...])
    ps = _dot(ssm_scr[...], wps_ref[...])
    o_ref[...] = (_sigmoid(ga_ref[...].astype(F32)) * pa + _sigmoid(gs_ref[...].astype(F32)) * ps).astype(o_ref.dtype)


def _merge(attn, g, z, w_glu, w_pa, w_ps):
    n = attn.shape[0]
    tm = min(n, 1024)
    tn = 1024
    nj = D_MODEL // tn
    return pl.pallas_call(
        _merge_kernel,
        grid=(n // tm, nj),
        in_specs=[
            pl.BlockSpec((tm, N_HEADS * V_HEAD), lambda i, j: (i, 0)),
            pl.BlockSpec((tm, SSM_WIDTH), lambda i, j: (i, 0)),
            pl.BlockSpec((tm, tn), lambda i, j: (i, 1 + j)),
            pl.BlockSpec((tm, tn), lambda i, j: (i, 1 + nj + j)),
            pl.BlockSpec((SSM_WIDTH, SSM_WIDTH), lambda i, j: (0, 0)),
            pl.BlockSpec((N_HEADS * V_HEAD, tn), lambda i, j: (0, j)),
            pl.BlockSpec((SSM_WIDTH, tn), lambda i, j: (0, j)),
        ],
        out_specs=pl.BlockSpec((tm, tn), lambda i, j: (i, j)),
        out_shape=jax.ShapeDtypeStruct((n, D_MODEL), BF16),
        scratch_shapes=[pltpu.VMEM((tm, SSM_WIDTH), BF16)],
        compiler_params=_params(("parallel", "arbitrary")),
        name="glu_merge",
    )(attn, g, z, z, w_glu, w_pa, w_ps)


def _outproj_kernel(m_ref, x_ref, g1_ref, b1_ref, g2_ref, b2_ref, w_ref, h2_ref, h2t_ref):
    y = _dot(m_ref[...], w_ref[...])
    h = _ln(x_ref[...], g1_ref[...], b1_ref[...])
    h2 = _ln(DEEPNORM_ALPHA * h + y, g2_ref[...], b2_ref[...])
    h2_ref[...] = h2.astype(h2_ref.dtype)
    h2t_ref[...] = h2.T.astype(h2t_ref.dtype)


def _outproj(merged, x2, g1, b1, g2, b2, w_out):
    n = merged.shape[0]
    tm = min(n, 512)
    vec = pl.BlockSpec((1, D_MODEL), lambda i: (0, 0))
    return pl.pallas_call(
        _outproj_kernel,
        grid=(n // tm,),
        in_specs=[
            pl.BlockSpec((tm, D_MODEL), lambda i: (i, 0)),
            pl.BlockSpec((tm, D_MODEL), lambda i: (i, 0)),
            vec, vec, vec, vec,
            pl.BlockSpec((D_MODEL, D_MODEL), lambda i: (0, 0)),
        ],
        out_specs=[
            pl.BlockSpec((tm, D_MODEL), lambda i: (i, 0)),
            pl.BlockSpec((D_MODEL, tm), lambda i: (0, i)),
        ],
        out_shape=[
            jax.ShapeDtypeStruct((n, D_MODEL), BF16),
            jax.ShapeDtypeStruct((D_MODEL, n), BF16),
        ],
        compiler_params=_params(("parallel",)),
        name="outproj_ln",
    )(merged, x2, g1, b1, g2, b2, w_out)


def _peer_wcomb_kernel(keys_ref, wpq_ref, o_ref):
    o_ref[...] = lax.dot_general(keys_ref[0].astype(BF16), wpq_ref[...], (((1,), (1,)), ((), ())),
                                 preferred_element_type=F32).astype(o_ref.dtype)


def _peer_wcomb(sub_keys, w_pq):
    nhc = PEER_HEADS * 2
    keys = sub_keys.reshape(nhc, N_KEYS, PEER_KEY_HALF)
    return pl.pallas_call(
        _peer_wcomb_kernel,
        grid=(nhc,),
        in_specs=[
            pl.BlockSpec((1, N_KEYS, PEER_KEY_HALF), lambda i: (i, 0, 0)),
            pl.BlockSpec((D_MODEL, PEER_KEY_HALF), lambda i: (0, i)),
        ],
        out_specs=pl.BlockSpec((N_KEYS, D_MODEL), lambda i: (i, 0)),
        out_shape=jax.ShapeDtypeStruct((nhc * N_KEYS, D_MODEL), BF16),
        compiler_params=_params(("parallel",)),
        name="peer_wcomb",
    )(keys, w_pq)


PEER_TOK = 1024
PEER_THRESH_TOK = 2048


def _peer_scores_kernel(w_ref, xt_ref, s3_ref):
    rows = w_ref.shape[0] // 2
    for g in range(2):
        st = _dot(w_ref[g * rows:(g + 1) * rows, :], xt_ref[...])
        for c in range(s3_ref.shape[1]):
            s3_ref[g * rows:(g + 1) * rows, c, :] = st[:, c * LANES:(c + 1) * LANES]


def _peer_scores(wcomb, h2t):
    n = h2t.shape[1]
    tl = min(n, PEER_TOK)
    rows = wcomb.shape[0]
    return pl.pallas_call(
        _peer_scores_kernel,
        grid=(n // tl,),
        in_specs=[
            pl.BlockSpec((rows, D_MODEL), lambda i: (0, 0)),
            pl.BlockSpec((D_MODEL, tl), lambda i: (0, i)),
        ],
        out_specs=pl.BlockSpec((rows, tl // LANES, LANES), lambda i: (0, i, 0)),
        out_shape=jax.ShapeDtypeStruct((rows, n // LANES, LANES), F32),
        compiler_params=_params(("parallel",)),
        name="peer_scores",
    )(wcomb, h2t)


PEER_LIST = PEER_TOPK + 1
PEER_EXPERT_BLOCK = 16 * N_KEYS
PEER_SUB = 4 * N_KEYS


def _insert_sorted(t, v):
    out = []
    for r in range(len(t)):
        out.append(jnp.maximum(t[r], v))
        v = jnp.minimum(t[r], v)
    return out


PEER_ROW = 4


def _count_above(vals, x, strict):
    above = (lambda v: v > x) if strict else (lambda v: v >= x)
    pick = lambda m, hi, lo: jnp.where(m, hi, lo)
    m8 = above(vals[7])
    m4 = above(pick(m8, vals[11], vals[3]))
    m2 = above(pick(m8, pick(m4, vals[13], vals[9]), pick(m4, vals[5], vals[1])))
    hi = pick(m4, pick(m2, vals[14], vals[12]), pick(m2, vals[10], vals[8]))
    lo = pick(m4, pick(m2, vals[6], vals[4]), pick(m2, vals[2], vals[0]))
    m1 = above(pick(m8, hi, lo))
    cnt = (pick(m8, 8.0, 0.0) + pick(m4, 4.0, 0.0)) + (pick(m2, 2.0, 0.0) + pick(m1, 1.0, 0.0))
    return pick(above(vals[15]), 16.0, cnt)


def _peer_thresh_kernel(s_ref, c_ref, e1_ref, q_ref, e2_ref, q_scr, e2_scr):
    shape = s_ref.shape[1:]
    neg = jnp.full(shape, -jnp.inf, F32)

    def top_list(base):
        def body(n, t):
            return tuple(_insert_sorted(list(t), s_ref[base + n]))
        return lax.fori_loop(0, N_KEYS, body, (neg,) * PEER_LIST, unroll=4)

    a = top_list(0)
    b = top_list(N_KEYS)
    t = [neg] * PEER_LIST
    for i in range(1, PEER_LIST + 1):
        for j in range(1, PEER_LIST // i + 1):
            t = _insert_sorted(t, a[i - 1] + b[j - 1])
    tau = 0.5 * (t[PEER_TOPK - 1] + t[PEER_TOPK])
    top = a[0] + b[0]
    z = jnp.exp(t[0] - top)
    for r in range(1, PEER_TOPK):
        z = z + jnp.exp(t[r] - top)
    cv = b[0] + jnp.log(2.0 * z)
    a16 = a[:PEER_TOPK]

    def write(n, carry):
        s1 = s_ref[n]
        s2 = s_ref[N_KEYS + n]
        c = 1.0 + _count_above(a16, s1, True)
        e1 = jnp.exp(s1 - a16[0])
        for gq in range(shape[0] // PEER_ROW):
            c_ref[n, gq] = c[gq * PEER_ROW:(gq + 1) * PEER_ROW, :]
            e1_ref[n, gq] = e1[gq * PEER_ROW:(gq + 1) * PEER_ROW, :]
        row0 = pl.multiple_of(n * nsub, nsub)
        q_scr[pl.ds(row0, nsub), :] = _count_above(a16, tau - s2, False)
        e2_scr[pl.ds(row0, nsub), :] = jnp.exp(s2 - cv)
        return carry

    nsub = shape[0]
    lax.fori_loop(0, N_KEYS, write, 0, unroll=2)
    for c in range(nsub):
        q_ref[:, c * LANES:(c + 1) * LANES] = q_scr[pl.ds(c, N_KEYS, stride=nsub), :].astype(q_ref.dtype)
        e2_ref[:, c * LANES:(c + 1) * LANES] = e2_scr[pl.ds(c, N_KEYS, stride=nsub), :].astype(e2_ref.dtype)


def _peer_thresh(s3):
    rows, nl, _ = s3.shape
    n = nl * LANES
    nb = min(PEER_THRESH_TOK // LANES, nl)
    half = rows // 2
    ng = nb // PEER_ROW
    row_spec = pl.BlockSpec((N_KEYS, ng, PEER_ROW, LANES), lambda i, h: (h, i, 0, 0))
    row_shape = jax.ShapeDtypeStruct((half, nl // PEER_ROW, PEER_ROW, LANES), F32)
    tile_spec = pl.BlockSpec((N_KEYS, nb * LANES), lambda i, h: (h, i))
    tile_shape = jax.ShapeDtypeStruct((half, n), BF16)
    c, e1, q, e2 = pl.pallas_call(
        _peer_thresh_kernel,
        grid=(nl // nb, PEER_HEADS),
        in_specs=[pl.BlockSpec((2 * N_KEYS, nb, LANES), lambda i, h: (h, i, 0))],
        out_specs=[row_spec, row_spec, tile_spec, tile_spec],
        out_shape=[row_shape, row_shape, tile_shape, tile_shape],
        scratch_shapes=[pltpu.VMEM((N_KEYS * nb, LANES), F32), pltpu.VMEM((N_KEYS * nb, LANES), F32)],
        compiler_params=_params(("parallel", "parallel")),
        name="peer_thresh",
    )(s3)
    shape5 = (PEER_HEADS, N_KEYS, nl // PEER_ROW, PEER_ROW, LANES)
    tiles = lambda v: v.reshape(PEER_HEADS, N_KEYS, n)
    return c.reshape(shape5), e1.reshape(shape5), tiles(q), tiles(e2)


def _peer_dense_kernel(xt_ref, u_ref, vt_ref, q_ref, e2_ref, c_ref, e1_ref, o_ref, acc_scr, a_scr, *, sub):
    e = pl.program_id(1)
    eb = u_ref.shape[0]

    @pl.when(e == 0)
    def _():
        acc_scr[...] = jnp.zeros(acc_scr.shape, F32)

    def row(ref, h, il):
        return jnp.concatenate([ref[h, il, 0, c:c + 1, :] for c in range(ref.shape[3])], axis=1).astype(BF16)

    zero = jnp.zeros((), BF16)

    xt = xt_ref[...]
    nsub = eb // sub
    score = lambda s: _dot(u_ref[s * sub:(s + 1) * sub, :], xt)
    st_next = score(0)
    for s in range(nsub):
        st, st_next = st_next, (score(s + 1) if s + 1 < nsub else None)
        for ii in range(sub // N_KEYS):
            il = s * (sub // N_KEYS) + ii
            gate = None
            for h in range(PEER_HEADS):
                term = row(e1_ref, h, il) * jnp.where(q_ref[h] >= row(c_ref, h, il), e2_ref[h], zero)
                gate = term if gate is None else gate + term
            a = _gelu_x2(st[ii * N_KEYS:(ii + 1) * N_KEYS, :]).astype(BF16) * gate
            a_scr[s * sub + ii * N_KEYS:s * sub + (ii + 1) * N_KEYS, :] = a
        acc_scr[...] += _dot(vt_ref[:, s * sub:(s + 1) * sub], a_scr[s * sub:(s + 1) * sub, :])

    @pl.when(e == pl.num_programs(1) - 1)
    def _():
        o_ref[...] = acc_scr[...].astype(o_ref.dtype)


def _peer_dense(h2t, q, e2, c, e1, u_tab, vt_tab):
    n = h2t.shape[1]
    t = min(n, PEER_ROW * LANES)
    eb = PEER_EXPERT_BLOCK
    row_spec = pl.BlockSpec((PEER_HEADS, eb // N_KEYS, 1, PEER_ROW, LANES), lambda i, e: (0, e, i, 0, 0))
    return pl.pallas_call(
        functools.partial(_peer_dense_kernel, sub=PEER_SUB),
        grid=(n // t, N_EXPERTS // eb),
        in_specs=[
            pl.BlockSpec((D_MODEL, t), lambda i, e: (0, i)),
            pl.BlockSpec((eb, D_MODEL), lambda i, e: (e, 0)),
            pl.BlockSpec((D_MODEL, eb), lambda i, e: (0, e)),
            pl.BlockSpec((PEER_HEADS, N_KEYS, t), lambda i, e: (0, 0, i)),
            pl.BlockSpec((PEER_HEADS, N_KEYS, t), lambda i, e: (0, 0, i)),
            row_spec, row_spec,
        ],
        out_specs=pl.BlockSpec((D_MODEL, t), lambda i, e: (0, i)),
        out_shape=jax.ShapeDtypeStruct((D_MODEL, n), BF16),
        scratch_shapes=[
            pltpu.VMEM((D_MODEL, t), F32),
            pltpu.VMEM((eb, t), BF16),
        ],
        compiler_params=_params(("parallel", "arbitrary")),
        name="peer_dense",
    )(h2t, u_tab, vt_tab, q, e2, c, e1)


def _final_ln_kernel(yt_ref, h2_ref, g_ref, b_ref, o_ref):
    y = yt_ref[...].astype(F32).T
    o_ref[...] = _ln(DEEPNORM_ALPHA * h2_ref[...].astype(F32) + y, g_ref[...], b_ref[...])


def _final_ln(yt, h2, ln_g, ln_b):
    n = h2.shape[0]
    tm = min(n, 512)
    vec = pl.BlockSpec((1, D_MODEL), lambda i: (0, 0))
    return pl.pallas_call(
        _final_ln_kernel,
        grid=(n // tm,),
        in_specs=[
            pl.BlockSpec((D_MODEL, tm), lambda i: (0, i)),
            pl.BlockSpec((tm, D_MODEL), lambda i: (i, 0)),
            vec, vec,
        ],
        out_specs=pl.BlockSpec((tm, D_MODEL), lambda i: (i, 0)),
        out_shape=jax.ShapeDtypeStruct((n, D_MODEL), F32),
        compiler_params=_params(("parallel",)),
        name="final_ln",
    )(yt, h2, ln_g, ln_b)


def _rot_half_cols(w):
    half = QK_ROPE // 2
    return jnp.concatenate([-w[..., half:], w[..., :half]], axis=-1)


def kernel(x, ln_in_g, ln_in_b, w_in, q_norm_g, w_uq, kv_norm_g, w_ukv, ssm_lam_re, ssm_lam_im, ssm_log_dt,
           ssm_b_re, ssm_b_im, ssm_c_re, ssm_c_im, ssm_d, w_glu, w_proj_attn, w_proj_ssm, w_out, ln_mix_g, ln_mix_b,
           w_peer_q, peer_sub_keys, peer_u, peer_v, ln_ffn_g, ln_ffn_b):
    batch, seq, _ = x.shape
    n = batch * seq
    x2 = x.reshape(n, D_MODEL)
    row = lambda v: v.reshape(1, -1).astype(F32)
    l = 0

    wi = w_in[l]
    w_main = jnp.concatenate([wi[:, OFF_SSM:OFF_KR], wi[:, OFF_GA:IN_WIDTH]], axis=1).astype(BF16)
    w_krc = wi[:, OFF_KR:OFF_GA]
    w_kr = jnp.concatenate([w_krc, _rot_half_cols(w_krc)], axis=1).astype(BF16)
    wq = w_uq[l].reshape(Q_LORA, N_HEADS, QK_HEAD)
    wq_ext = jnp.concatenate([wq, _rot_half_cols(wq[..., QK_NOPE:])], axis=-1)
    wq_ext = wq_ext.transpose(1, 0, 2).astype(BF16)
    wkv = w_ukv[l].reshape(KV_LORA, N_HEADS, QK_NOPE + V_HEAD).transpose(1, 0, 2).astype(BF16)

    cs = _rope_table(seq)
    u, z, kr = _inproj(x2, row(ln_in_g), row(ln_in_b), w_main, w_kr)

    q, k, v = _qkv(z, kr, cs, row(q_norm_g[l]), row(kv_norm_g[l]), wq_ext, wkv, batch, seq)
    attn = _attention(q, k, v).reshape(n, N_HEADS * V_HEAD)

    w, ws, wx, a16 = _ssm_wgen(ssm_lam_re[l], ssm_lam_im[l], ssm_log_dt[l], ssm_b_re[l], ssm_b_im[l],
                               ssm_c_re[l], ssm_c_im[l])
    g = _ssm_branch(u, batch, seq, w, ws, wx, a16, row(ssm_d[l]))

    merged = _merge(attn, g, z, w_glu[l].astype(BF16), w_proj_attn[l].astype(BF16), w_proj_ssm[l].astype(BF16))
    h2, h2t = _outproj(merged, x2, row(ln_in_g), row(ln_in_b), row(ln_mix_g[l]), row(ln_mix_b[l]),
                       w_out[l].astype(BF16))

    wcomb = _peer_wcomb(peer_sub_keys[l], w_peer_q[l].astype(BF16))
    s3 = _peer_scores(wcomb, h2t)
    c, e1, q, e2 = _peer_thresh(s3)
    yt = _peer_dense(h2t, q, e2, c, e1, peer_u[l].astype(BF16), peer_v[l].T.astype(BF16))
    out = _final_ln(yt, h2, row(ln_ffn_g[l]), row(ln_ffn_b[l]))
    return out.reshape(batch, seq, D_MODEL)
```

```python
import functools
import math

import jax
import jax.numpy as jnp
from jax import lax
from jax.experimental import pallas as pl
from jax.experimental.pallas import tpu as pltpu

F32 = jnp.float32
BF16 = jnp.bfloat16

D_MODEL = 2048
DEPTH = 1
CHUNK = 64
SSM_WIDTH = 1024
SSM_GROUP = 16
SSM_GROUPS = SSM_WIDTH // SSM_GROUP
SSM_STATE = 64
N_HEADS = 16
Q_LORA = 512
KV_LORA = 512
QK_NOPE = 128
QK_ROPE = 64
V_HEAD = 128
QK_HEAD = QK_NOPE + QK_ROPE
ROPE_THETA = 10000.0
PEER_HEADS = 8
N_KEYS = 128
N_EXPERTS = N_KEYS * N_KEYS
PEER_KEY_HALF = 128
PEER_TOPK = 16
LN_EPS = 1e-5
RMS_EPS = 1e-6
DEEPNORM_ALPHA = (2.0 * DEPTH) ** 0.25
OFF_SSM = 0
OFF_Q = OFF_SSM + SSM_WIDTH
OFF_KV = OFF_Q + Q_LORA
OFF_KR = OFF_KV + KV_LORA
OFF_GA = OFF_KR + QK_ROPE
OFF_GS = OFF_GA + D_MODEL
IN_WIDTH = OFF_GS + D_MODEL

LANES = 128
VMEM_LIMIT_BYTES = 56 * 1024 * 1024

SSM_CHUNK = 16
SSM_TILE_GROUPS = LANES // SSM_GROUP
SSM_TILES = SSM_WIDTH // LANES
SSM_TILE_STATE = SSM_TILE_GROUPS * SSM_STATE
NEG_BIG = -1e30


def _params(semantics):
    return pltpu.CompilerParams(dimension_semantics=semantics, vmem_limit_bytes=VMEM_LIMIT_BYTES)


def _dot(a, b):
    return jnp.dot(a, b, preferred_element_type=F32)


def _ln(x, g, b):
    mu = jnp.mean(x, axis=-1, keepdims=True)
    xc = x - mu
    var = jnp.mean(xc * xc, axis=-1, keepdims=True)
    return xc * lax.rsqrt(var + LN_EPS) * g + b


GELU_C = 0.7978845608028654
GELU_A = 0.044715


def _gelu(x):
    return 0.5 * x * (1.0 + jnp.tanh(GELU_C * (x + GELU_A * (x * x * x))))


def _gelu_x2(x):
    return x * (1.0 + jnp.tanh(x * (GELU_C + (GELU_C * GELU_A) * (x * x))))


def _sigmoid(x):
    return 1.0 / (1.0 + jnp.exp(-x))


def _rope_table_kernel(invf_ref, o_ref):
    ts = o_ref.shape[0]
    pos = (lax.broadcasted_iota(jnp.int32, (ts, LANES), 0) + pl.program_id(0) * ts).astype(F32)
    ang = pos * invf_ref[...]
    lane = lax.broadcasted_iota(jnp.int32, (ts, LANES), 1)
    o_ref[...] = jnp.where(lane < 2 * (QK_ROPE // 2), jnp.cos(ang), jnp.sin(ang))


def _rope_table(seq):
    half = QK_ROPE // 2
    invf = 1.0 / (ROPE_THETA ** (jnp.arange(half, dtype=F32) / half))
    invf4 = jnp.tile(invf, 4)[None, :]
    ts = min(seq, 1024)
    return pl.pallas_call(
        _rope_table_kernel,
        grid=(seq // ts,),
        in_specs=[pl.BlockSpec((1, LANES), lambda i: (0, 0))],
        out_specs=pl.BlockSpec((ts, LANES), lambda i: (i, 0)),
        out_shape=jax.ShapeDtypeStruct((seq, LANES), F32),
        compiler_params=_params(("parallel",)),
        name="rope_table",
    )(invf4)


def _inproj_kernel(x_ref, g_ref, b_ref, w_ref, wkr_ref, u_ref, z_ref, kr_ref, h_scr):
    j = pl.program_id(1)

    @pl.when(j == 0)
    def _():
        h = _ln(x_ref[...], g_ref[...], b_ref[...]).astype(BF16)
        h_scr[...] = h
        u_ref[...] = _dot(h, w_ref[...])
        kr_ref[...] = _dot(h, wkr_ref[...]).astype(kr_ref.dtype)

    @pl.when(j > 0)
    def _():
        z_ref[...] = _dot(h_scr[...], w_ref[...]).astype(z_ref.dtype)


def _inproj(x2, ln_g, ln_b, w_main, w_kr):
    n = x2.shape[0]
    tm = min(n, 1024)
    tn = 1024
    nj = w_main.shape[1] // tn
    return pl.pallas_call(
        _inproj_kernel,
        grid=(n // tm, nj),
        in_specs=[
            pl.BlockSpec((tm, D_MODEL), lambda i, j: (i, 0)),
            pl.BlockSpec((1, D_MODEL), lambda i, j: (0, 0)),
            pl.BlockSpec((1, D_MODEL), lambda i, j: (0, 0)),
            pl.BlockSpec((D_MODEL, tn), lambda i, j: (0, j)),
            pl.BlockSpec((D_MODEL, LANES), lambda i, j: (0, 0)),
        ],
        out_specs=[
            pl.BlockSpec((tm, tn), lambda i, j: (i, 0)),
            pl.BlockSpec((tm, tn), lambda i, j: (i, jnp.maximum(j - 1, 0))),
            pl.BlockSpec((tm, LANES), lambda i, j: (i, 0)),
        ],
        out_shape=[
            jax.ShapeDtypeStruct((n, tn), F32),
            jax.ShapeDtypeStruct((n, (nj - 1) * tn), BF16),
            jax.ShapeDtypeStruct((n, LANES), BF16),
        ],
        scratch_shapes=[pltpu.VMEM((tm, D_MODEL), BF16)],
        compiler_params=_params(("parallel", "arbitrary")),
        name="ln_inproj",
    )(x2, ln_g, ln_b, w_main, w_kr)


def _rope_mix(t2, cs):
    t = t2 * cs
    return t + pltpu.roll(t, QK_ROPE, 1)


def _qkv_kernel(cq_ref, ckv_ref, kr_ref, cs_ref, gq_ref, gkv_ref, wq_ref, wkv_ref,
                q_ref, k_ref, v_ref, cqn_scr, ckvn_scr, kro_scr):
    cs = cs_ref[...]

    @pl.when(pl.program_id(1) == 0)
    def _():
        def rms(c, g):
            c = c.astype(F32)
            return (c * lax.rsqrt(jnp.mean(c * c, axis=-1, keepdims=True) + RMS_EPS) * g).astype(BF16)

        cqn_scr[...] = rms(cq_ref[...], gq_ref[...])
        ckvn_scr[...] = rms(ckv_ref[...], gkv_ref[...])
        kro_scr[...] = _rope_mix(kr_ref[...].astype(F32), cs).astype(BF16)

    scale = math.log2(math.e) / math.sqrt(QK_HEAD)
    hp = wq_ref.shape[0]
    wide = 2 * LANES
    qe2 = _dot(cqn_scr[...], jnp.concatenate([wq_ref[j] for j in range(hp)], axis=1))
    kv2 = _dot(ckvn_scr[...], jnp.concatenate([wkv_ref[j] for j in range(hp)], axis=1))
    for j in range(hp):
        qe = qe2[:, j * wide:(j + 1) * wide]
        qr = _rope_mix(qe[:, QK_NOPE:], cs)
        q_ref[0, j, :, :QK_NOPE] = (qe[:, :QK_NOPE] * scale).astype(BF16)
        q_ref[0, j, :, QK_NOPE:] = (qr[:, :QK_ROPE] * scale).astype(BF16)
        kv = kv2[:, j * wide:(j + 1) * wide]
        k_ref[0, j, :, :QK_NOPE] = kv[:, :QK_NOPE].astype(BF16)
        k_ref[0, j, :, QK_NOPE:] = kro_scr[:, :QK_ROPE]
        v_ref[0, j] = kv[:, QK_NOPE:].astype(BF16)


def _qkv(z, kr, cs, gq, gkv, wq_ext, wkv, batch, seq):
    n = z.shape[0]
    tm = min(seq, 2048)
    nsb = seq // tm
    hp = 2
    return pl.pallas_call(
        _qkv_kernel,
        grid=(n // tm, N_HEADS // hp),
        in_specs=[
            pl.BlockSpec((tm, Q_LORA), lambda i, h: (i, 0)),
            pl.BlockSpec((tm, KV_LORA), lambda i, h: (i, 1)),
            pl.BlockSpec((tm, LANES), lambda i, h: (i, 0)),
            pl.BlockSpec((tm, LANES), lambda i, h: (i % nsb, 0)),
            pl.BlockSpec((1, Q_LORA), lambda i, h: (0, 0)),
            pl.BlockSpec((1, KV_LORA), lambda i, h: (0, 0)),
            pl.BlockSpec((hp, Q_LORA, 2 * LANES), lambda i, h: (h, 0, 0)),
            pl.BlockSpec((hp, KV_LORA, 2 * LANES), lambda i, h: (h, 0, 0)),
        ],
        out_specs=[
            pl.BlockSpec((1, hp, tm, QK_HEAD), lambda i, h: (i // nsb, h, i % nsb, 0)),
            pl.BlockSpec((1, hp, tm, QK_HEAD), lambda i, h: (i // nsb, h, i % nsb, 0)),
            pl.BlockSpec((1, hp, tm, V_HEAD), lambda i, h: (i // nsb, h, i % nsb, 0)),
        ],
        out_shape=[
            jax.ShapeDtypeStruct((batch, N_HEADS, seq, QK_HEAD), BF16),
            jax.ShapeDtypeStruct((batch, N_HEADS, seq, QK_HEAD), BF16),
            jax.ShapeDtypeStruct((batch, N_HEADS, seq, V_HEAD), BF16),
        ],
        scratch_shapes=[
            pltpu.VMEM((tm, Q_LORA), BF16),
            pltpu.VMEM((tm, KV_LORA), BF16),
            pltpu.VMEM((tm, LANES), BF16),
        ],
        compiler_params=_params(("parallel", "arbitrary")),
        name="qkv_rope",
    )(z, z, kr, cs, gq, gkv, wq_ext, wkv)


ATT_ROWS = 512
ATT_CHAINS = 8
ATT_BK_MAIN = 1024


def _attn_kernel(q_ref, k_ref, v_ref, o_ref, m_scr, acc_scr, *, rows, chains, bkm):
    qi = pl.program_id(2)
    m_scr[...] = jnp.full(m_scr.shape, -jnp.inf, F32)
    acc_scr[...] = jnp.zeros(acc_scr.shape, F32)

    def load_kv(j, bk):
        start = pl.multiple_of(j * bk, bk)
        kj = k_ref[0, 0, pl.ds(start, bk), :]
        vj = jnp.concatenate([v_ref[0, 0, pl.ds(start, bk), :], jnp.ones((bk, LANES), BF16)], axis=1)
        return kj, vj

    def block(c, kj, vj, masked):
        q = q_ref[0, 0, c * rows:(c + 1) * rows, :]
        s = lax.dot_general(q, kj, (((1,), (1,)), ((), ())), preferred_element_type=F32)
        if masked:
            rc = lax.broadcasted_iota(jnp.int32, s.shape, 0) // CHUNK
            cc = lax.broadcasted_iota(jnp.int32, s.shape, 1) // CHUNK
            s = jnp.where(cc <= rc, s, NEG_BIG)
        m_prev = m_scr[c]
        m_new = jnp.maximum(m_prev, jnp.max(s, axis=-1, keepdims=True))
        alpha = jnp.exp2(m_prev - m_new)
        p = jnp.exp2(s - jnp.tile(m_new, (1, s.shape[1] // LANES)))
        acc_scr[c] = jnp.tile(alpha, (1, 2)) * acc_scr[c] + _dot(p.astype(BF16), vj)
        m_scr[c] = m_new

    per_step = chains * rows // bkm
    group = 2 if per_step % 2 == 0 else 1

    def body(j, carry):
        for jj in range(group):
            kj, vj = load_kv(j * group + jj, bkm)
            for c in range(chains):
                block(c, kj, vj, False)
        return carry

    lax.fori_loop(0, qi * (per_step // group), body, 0)
    for jj in range(chains):
        kj, vj = load_kv(qi * chains + jj, rows)
        for c in range(jj, chains):
            block(c, kj, vj, c == jj)
    for c in range(chains):
        a = acc_scr[c]
        o_ref[0, c * rows:(c + 1) * rows, :] = (a[:, :V_HEAD] / a[:, V_HEAD:]).astype(o_ref.dtype)


def _attention(q, k, v):
    batch, _, seq, _ = q.shape
    rows = min(seq, ATT_ROWS)
    chains = min(ATT_CHAINS, seq // rows)
    bq = rows * chains
    bkm = min(ATT_BK_MAIN, bq)
    return pl.pallas_call(
        functools.partial(_attn_kernel, rows=rows, chains=chains, bkm=bkm),
        grid=(batch, N_HEADS, seq // bq),
        in_specs=[
            pl.BlockSpec((1, 1, bq, QK_HEAD), lambda b, h, i: (b, h, i, 0)),
            pl.BlockSpec((1, 1, seq, QK_HEAD), lambda b, h, i: (b, h, 0, 0)),
            pl.BlockSpec((1, 1, seq, V_HEAD), lambda b, h, i: (b, h, 0, 0)),
        ],
        out_specs=pl.BlockSpec((1, bq, V_HEAD), lambda b, h, i: (b, i, h)),
        out_shape=jax.ShapeDtypeStruct((batch, seq, N_HEADS * V_HEAD), BF16),
        scratch_shapes=[
            pltpu.VMEM((chains, rows, LANES), F32),
            pltpu.VMEM((chains, rows, 2 * V_HEAD), F32),
        ],
        compiler_params=_params(("parallel", "parallel", "arbitrary")),
        name="flash_attn",
    )(q, k, v)


def _cpow(lr, li, ldt, k):
    dt = jnp.exp(ldt)
    mag = jnp.exp(lr * dt * k)
    ang = li * dt * k
    return mag * jnp.cos(ang), mag * jnp.sin(ang)


def _ssm_wgen_kernel(lr_r, li_r, ldt_r, bre_ref, bim_ref, cre_ref, cim_ref,
                     w_ref, ws_ref, wx_ref, a16_ref):
    hp = lax.Precision.HIGHEST
    ns = SSM_TILE_STATE
    lr, li, ldt = lr_r[0], li_r[0], ldt_r[0]
    a_re, a_im = _cpow(lr, li, ldt, 1.0)
    den = lr * lr + li * li
    nr = a_re - 1.0
    ni = a_im
    cf_re = (nr * lr + ni * li) / den
    cf_im = (ni * lr - nr * li) / den
    bre, bim = bre_ref[0], bim_ref[0]
    cre, cim = cre_ref[0], cim_ref[0]
    w_ref[...] = jnp.zeros(w_ref.shape, w_ref.dtype)
    for k in range(SSM_CHUNK):
        p_re, p_im = _cpow(lr, li, ldt, float(k))
        m_re = cf_re * p_re - cf_im * p_im
        m_im = cf_re * p_im + cf_im * p_re
        bm_re = bre * m_re - bim * m_im
        bm_im = bre * m_im + bim * m_re
        bd = (jnp.dot(bm_re, cre, precision=hp, preferred_element_type=F32)
              - jnp.dot(bm_im, cim, precision=hp, preferred_element_type=F32)).astype(BF16)
        for j in range(SSM_CHUNK - k):
            i = j + k
            w_ref[0, j * LANES:(j + 1) * LANES, i * LANES:(i + 1) * LANES] = bd
        jj = SSM_CHUNK - 1 - k
        ws_ref[0, jj * LANES:(jj + 1) * LANES, :ns] = bm_re.astype(BF16)
        ws_ref[0, jj * LANES:(jj + 1) * LANES, ns:] = bm_im.astype(BF16)
    cre_t, cim_t = cre.T, cim.T
    for i in range(SSM_CHUNK):
        p_re, p_im = _cpow(lr, li, ldt, float(i + 1))
        wx_ref[0, :ns, i * LANES:(i + 1) * LANES] = (p_re * cre_t - p_im * cim_t).T.astype(BF16)
        wx_ref[0, ns:, i * LANES:(i + 1) * LANES] = (-(p_re * cim_t + p_im * cre_t)).T.astype(BF16)
    q_re, q_im = _cpow(lr, li, ldt, float(SSM_CHUNK))
    a16_ref[0, :, :ns] = q_re
    a16_ref[0, :, ns:] = q_im


def _ssm_wgen(lam_re, lam_im, log_dt, b_re, b_im, c_re, c_im):
    nt, tg, ns = SSM_TILES, SSM_TILE_GROUPS, SSM_TILE_STATE
    eye = jnp.eye(tg, dtype=F32)
    bbd = lambda b: jnp.einsum("cgph,gk->cghkp", b.reshape(nt, tg, SSM_STATE, SSM_GROUP), eye).reshape(nt, LANES, ns)
    cbd = lambda c: jnp.einsum("cghp,gk->cgpkh", c.reshape(nt, tg, SSM_GROUP, SSM_STATE), eye).reshape(nt, ns, LANES)
    ldt = jnp.repeat(log_dt, SSM_STATE)
    rows = [a.reshape(nt, 1, ns) for a in (lam_re, lam_im, ldt)]
    kc = SSM_CHUNK * LANES
    row_spec = pl.BlockSpec((1, 1, ns), lambda c: (c, 0, 0))
    return pl.pallas_call(
        _ssm_wgen_kernel,
        grid=(nt,),
        in_specs=[row_spec] * 3 + [
            pl.BlockSpec((1, LANES, ns), lambda c: (c, 0, 0)),
            pl.BlockSpec((1, LANES, ns), lambda c: (c, 0, 0)),
            pl.BlockSpec((1, ns, LANES), lambda c: (c, 0, 0)),
            pl.BlockSpec((1, ns, LANES), lambda c: (c, 0, 0)),
        ],
        out_specs=[
            pl.BlockSpec((1, kc, kc), lambda c: (c, 0, 0)),
            pl.BlockSpec((1, kc, 2 * ns), lambda c: (c, 0, 0)),
            pl.BlockSpec((1, 2 * ns, kc), lambda c: (c, 0, 0)),
            pl.BlockSpec((1, 1, 2 * ns), lambda c: (c, 0, 0)),
        ],
        out_shape=[
            jax.ShapeDtypeStruct((nt, kc, kc), BF16),
            jax.ShapeDtypeStruct((nt, kc, 2 * ns), BF16),
            jax.ShapeDtypeStruct((nt, 2 * ns, kc), BF16),
            jax.ShapeDtypeStruct((nt, 1, 2 * ns), F32),
        ],
        compiler_params=_params(("parallel",)),
        name="ssm_wgen",
    )(*rows, bbd(b_re), bbd(b_im), cbd(c_re), cbd(c_im))


def _chunk_rows(u_ref, tb):
    return [u_ref[pl.ds(j, tb, stride=SSM_CHUNK), :] for j in range(SSM_CHUNK)]


def _ssm_sum_kernel(u_ref, ws_ref, s_ref, *, tb):
    u16 = jnp.concatenate(_chunk_rows(u_ref, tb), axis=1).astype(BF16)
    s_ref[0] = _dot(u16, ws_ref[0])


def _ssm_scan_kernel(s_ref, a16_ref, x_ref):
    ns = SSM_TILE_STATE
    a_re = a16_ref[0, :, :ns]
    a_im = a16_ref[0, :, ns:]
    nrow = s_ref.shape[1]

    def body(r, carry):
        xr, xi = carry
        x_ref[0, pl.ds(r, 1), :ns] = xr
        x_ref[0, pl.ds(r, 1), ns:] = xi
        sr = s_ref[0, pl.ds(r, 1), :ns]
        si = s_ref[0, pl.ds(r, 1), ns:]
        return a_re * xr - a_im * xi + sr, a_re * xi + a_im * xr + si

    zero = jnp.zeros((1, ns), F32)
    lax.fori_loop(0, nrow, body, (zero, zero), unroll=8)


def _ssm_out_kernel(u_ref, x_ref, w_ref, wx_ref, d_ref, g_ref, *, tb):
    us = _chunk_rows(u_ref, tb)
    u16 = jnp.concatenate(us, axis=1).astype(BF16)
    y16 = _dot(u16, w_ref[0]) + _dot(x_ref[0].astype(BF16), wx_ref[0])
    d = d_ref[...]
    for i in range(SSM_CHUNK):
        y = y16[:, i * LANES:(i + 1) * LANES] + d * us[i]
        g_ref[pl.ds(i, tb, stride=SSM_CHUNK), :] = _gelu(y)


def _ssm_branch(u, batch, seq, w, ws, wx, a16, d_skip):
    n = u.shape[0]
    nt, ns = SSM_TILES, SSM_TILE_STATE
    kc = SSM_CHUNK * LANES
    nb = n // SSM_CHUNK
    nbs = seq // SSM_CHUNK
    tb = min(nbs, 512)
    tok = tb * SSM_CHUNK
    s = pl.pallas_call(
        functools.partial(_ssm_sum_kernel, tb=tb),
        grid=(nt, nb // tb),
        in_specs=[
            pl.BlockSpec((tok, LANES), lambda c, r: (r, c)),
            pl.BlockSpec((1, kc, 2 * ns), lambda c, r: (c, 0, 0)),
        ],
        out_specs=pl.BlockSpec((1, tb, 2 * ns), lambda c, r: (c, r, 0)),
        out_shape=jax.ShapeDtypeStruct((nt, nb, 2 * ns), F32),
        compiler_params=_params(("parallel", "parallel")),
        name="ssm_chunk_sum",
    )(u, ws)
    x = pl.pallas_call(
        _ssm_scan_kernel,
        grid=(nt, batch),
        in_specs=[
            pl.BlockSpec((1, nbs, 2 * ns), lambda c, b: (c, b, 0)),
            pl.BlockSpec((1, 1, 2 * ns), lambda c, b: (c, 0, 0)),
        ],
        out_specs=pl.BlockSpec((1, nbs, 2 * ns), lambda c, b: (c, b, 0)),
        out_shape=jax.ShapeDtypeStruct((nt, nb, 2 * ns), F32),
        compiler_params=_params(("parallel", "parallel")),
        name="ssm_state_scan",
    )(s, a16)
    return pl.pallas_call(
        functools.partial(_ssm_out_kernel, tb=tb),
        grid=(nt, nb // tb),
        in_specs=[
            pl.BlockSpec((tok, LANES), lambda c, r: (r, c)),
            pl.BlockSpec((1, tb, 2 * ns), lambda c, r: (c, r, 0)),
            pl.BlockSpec((1, kc, kc), lambda c, r: (c, 0, 0)),
            pl.BlockSpec((1, 2 * ns, kc), lambda c, r: (c, 0, 0)),
            pl.BlockSpec((1, LANES), lambda c, r: (0, c)),
        ],
        out_specs=pl.BlockSpec((tok, LANES), lambda c, r: (r, c)),
        out_shape=jax.ShapeDtypeStruct((n, SSM_WIDTH), F32),
        compiler_params=_params(("parallel", "parallel")),
        name="ssm_chunk_out",
    )(u, x, w, wx, d_skip)


def _merge_kernel(attn_ref, g_ref, ga_ref, gs_ref, wglu_ref, wpa_ref, wps_ref, o_ref, ssm_scr):
    @pl.when(pl.program_id(1) == 0)
    def _():
        g = g_ref[...]
        ssm_scr[...] = (g * _sigmoid(_dot(g.astype(BF16), wglu_ref[...]))).astype(BF16)

    pa = _dot(attn_ref[...], wpa_ref[...])
    ps = _dot(ssm_scr[...], wps_ref[...])
    o_ref[...] = (_sigmoid(ga_ref[...].astype(F32)) * pa + _sigmoid(gs_ref[...].astype(F32)) * ps).astype(o_ref.dtype)


def _merge(attn, g, z, w_glu, w_pa, w_ps):
    n = attn.shape[0]
    tm = min(n, 1024)
    tn = 1024
    nj = D_MODEL // tn
    return pl.pallas_call(
        _merge_kernel,
        grid=(n // tm, nj),
        in_specs=[
            pl.BlockSpec((tm, N_HEADS * V_HEAD), lambda i, j: (i, 0)),
            pl.BlockSpec((tm, SSM_WIDTH), lambda i, j: (i, 0)),
            pl.BlockSpec((tm, tn), lambda i, j: (i, 1 + j)),
            pl.BlockSpec((tm, tn), lambda i, j: (i, 1 + nj + j)),
            pl.BlockSpec((SSM_WIDTH, SSM_WIDTH), lambda i, j: (0, 0)),
            pl.BlockSpec((N_HEADS * V_HEAD, tn), lambda i, j: (0, j)),
            pl.BlockSpec((SSM_WIDTH, tn), lambda i, j: (0, j)),
        ],
        out_specs=pl.BlockSpec((tm, tn), lambda i, j: (i, j)),
        out_shape=jax.ShapeDtypeStruct((n, D_MODEL), BF16),
        scratch_shapes=[pltpu.VMEM((tm, SSM_WIDTH), BF16)],
        compiler_params=_params(("parallel", "arbitrary")),
        name="glu_merge",
    )(attn, g, z, z, w_glu, w_pa, w_ps)


def _outproj_kernel(m_ref, x_ref, g1_ref, b1_ref, g2_ref, b2_ref, w_ref, h2_ref, h2t_ref):
    y = _dot(m_ref[...], w_ref[...])
    h = _ln(x_ref[...], g1_ref[...], b1_ref[...])
    h2 = _ln(DEEPNORM_ALPHA * h + y, g2_ref[...], b2_ref[...])
    h2_ref[...] = h2.astype(h2_ref.dtype)
    h2t_ref[...] = h2.T.astype(h2t_ref.dtype)


def _outproj(merged, x2, g1, b1, g2, b2, w_out):
    n = merged.shape[0]
    tm = min(n, 512)
    vec = pl.BlockSpec((1, D_MODEL), lambda i: (0, 0))
    return pl.pallas_call(
        _outproj_kernel,
        grid=(n // tm,),
        in_specs=[
            pl.BlockSpec((tm, D_MODEL), lambda i: (i, 0)),
            pl.BlockSpec((tm, D_MODEL), lambda i: (i, 0)),
            vec, vec, vec, vec,
            pl.BlockSpec((D_MODEL, D_MODEL), lambda i: (0, 0)),
        ],
        out_specs=[
            pl.BlockSpec((tm, D_MODEL), lambda i: (i, 0)),
            pl.BlockSpec((D_MODEL, tm), lambda i: (0, i)),
        ],
        out_shape=[
            jax.ShapeDtypeStruct((n, D_MODEL), BF16),
            jax.ShapeDtypeStruct((D_MODEL, n), BF16),
        ],
        compiler_params=_params(("parallel",)),
        name="outproj_ln",
    )(merged, x2, g1, b1, g2, b2, w_out)


def _peer_wcomb_kernel(keys_ref, wpq_ref, o_ref):
    o_ref[...] = lax.dot_general(keys_ref[0].astype(BF16), wpq_ref[...], (((1,), (1,)), ((), ())),
                                 preferred_element_type=F32).astype(o_ref.dtype)


def _peer_wcomb(sub_keys, w_pq):
    nhc = PEER_HEADS * 2
    keys = sub_keys.reshape(nhc, N_KEYS, PEER_KEY_HALF)
    return pl.pallas_call(
        _peer_wcomb_kernel,
        grid=(nhc,),
        in_specs=[
            pl.BlockSpec((1, N_KEYS, PEER_KEY_HALF), lambda i: (i, 0, 0)),
            pl.BlockSpec((D_MODEL, PEER_KEY_HALF), lambda i: (0, i)),
        ],
        out_specs=pl.BlockSpec((N_KEYS, D_MODEL), lambda i: (i, 0)),
        out_shape=jax.ShapeDtypeStruct((nhc * N_KEYS, D_MODEL), BF16),
        compiler_params=_params(("parallel",)),
        name="peer_wcomb",
    )(keys, w_pq)


PEER_TOK = 1024
PEER_THRESH_TOK = 2048


def _peer_scores_kernel(w_ref, xt_ref, s3_ref):
    rows = w_ref.shape[0] // 2
    for g in range(2):
        st = _dot(w_ref[g * rows:(g + 1) * rows, :], xt_ref[...])
        for c in range(s3_ref.shape[1]):
            s3_ref[g * rows:(g + 1) * rows, c, :] = st[:, c * LANES:(c + 1) * LANES]


def _peer_scores(wcomb, h2t):
    n = h2t.shape[1]
    tl = min(n, PEER_TOK)
    rows = wcomb.shape[0]
    return pl.pallas_call(
        _peer_scores_kernel,
        grid=(n // tl,),
        in_specs=[
            pl.BlockSpec((rows, D_MODEL), lambda i: (0, 0)),
            pl.BlockSpec((D_MODEL, tl), lambda i: (0, i)),
        ],
        out_specs=pl.BlockSpec((rows, tl // LANES, LANES), lambda i: (0, i, 0)),
        out_shape=jax.ShapeDtypeStruct((rows, n // LANES, LANES), F32),
        compiler_params=_params(("parallel",)),
        name="peer_scores",
    )(wcomb, h2t)


PEER_LIST = PEER_TOPK + 1
PEER_EXPERT_BLOCK = 16 * N_KEYS
PEER_SUB = 4 * N_KEYS


def _insert_sorted(t, v):
    out = []
    for r in range(len(t)):
        out.append(jnp.maximum(t[r], v))
        v = jnp.minimum(t[r], v)
    return out


PEER_ROW = 4


def _bitonic_merge(x, desc):
    n = len(x)
    if n == 1:
        return x
    h = n // 2
    big = [jnp.maximum(x[i], x[i + h]) for i in range(h)]
    small = [jnp.minimum(x[i], x[i + h]) for i in range(h)]
    first, second = (big, small) if desc else (small, big)
    return _bitonic_merge(first, desc) + _bitonic_merge(second, desc)


def _bitonic_sort(x, desc):
    n = len(x)
    if n == 1:
        return x
    h = n // 2
    return _bitonic_merge(_bitonic_sort(x[:h], True) + _bitonic_sort(x[h:], False), desc)


def _merge_top(x, y):
    n = len(x)
    big = [jnp.maximum(x[i], y[n - 1 - i]) for i in range(n)]
    rest = functools.reduce(jnp.maximum, [jnp.minimum(x[i], y[n - 1 - i]) for i in range(n)])
    return _bitonic_merge(big, True), rest


def _count_above(vals, x, strict):
    above = (lambda v: v > x) if strict else (lambda v: v >= x)
    pick = lambda m, hi, lo: jnp.where(m, hi, lo)
    m8 = above(vals[7])
    m4 = above(pick(m8, vals[11], vals[3]))
    m2 = above(pick(m8, pick(m4, vals[13], vals[9]), pick(m4, vals[5], vals[1])))
    hi = pick(m4, pick(m2, vals[14], vals[12]), pick(m2, vals[10], vals[8]))
    lo = pick(m4, pick(m2, vals[6], vals[4]), pick(m2, vals[2], vals[0]))
    m1 = above(pick(m8, hi, lo))
    cnt = (pick(m8, 8.0, 0.0) + pick(m4, 4.0, 0.0)) + (pick(m2, 2.0, 0.0) + pick(m1, 1.0, 0.0))
    return pick(above(vals[15]), 16.0, cnt)


def _peer_thresh_kernel(s_ref, c_ref, e1_ref, q_ref, e2_ref, q_scr, e2_scr):
    shape = s_ref.shape[1:]
    neg = jnp.full(shape, -jnp.inf, F32)

    def top_list(base):
        def tree(lo, hi):
            if hi - lo == PEER_TOPK:
                return _bitonic_sort([s_ref[base + n] for n in range(lo, hi)], True), neg
            mid = (lo + hi) // 2
            (x, dx), (y, dy) = tree(lo, mid), tree(mid, hi)
            top, rest = _merge_top(x, y)
            return top, jnp.maximum(jnp.maximum(dx, dy), rest)

        top, nxt = tree(0, N_KEYS)
        return top + [nxt]

    a = top_list(0)
    b = top_list(N_KEYS)
    t = [neg] * PEER_LIST
    for i in range(1, PEER_LIST + 1):
        for j in range(1, PEER_LIST // i + 1):
            t = _insert_sorted(t, a[i - 1] + b[j - 1])
    tau = 0.5 * (t[PEER_TOPK - 1] + t[PEER_TOPK])
    top = a[0] + b[0]
    z = jnp.exp(t[0] - top)
    for r in range(1, PEER_TOPK):
        z = z + jnp.exp(t[r] - top)
    cv = b[0] + jnp.log(2.0 * z)
    a16 = a[:PEER_TOPK]

    def write(n, carry):
        s1 = s_ref[n]
        s2 = s_ref[N_KEYS + n]
        c = 1.0 + _count_above(a16, s1, True)
        e1 = jnp.exp(s1 - a16[0])
        for gq in range(shape[0] // PEER_ROW):
            c_ref[n, gq] = c[gq * PEER_ROW:(gq + 1) * PEER_ROW, :]
            e1_ref[n, gq] = e1[gq * PEER_ROW:(gq + 1) * PEER_ROW, :]
        row0 = pl.multiple_of(n * nsub, nsub)
        q_scr[pl.ds(row0, nsub), :] = _count_above(a16, tau - s2, False)
        e2_scr[pl.ds(row0, nsub), :] = jnp.exp(s2 - cv)
        return carry

    nsub = shape[0]
    lax.fori_loop(0, N_KEYS, write, 0, unroll=2)
    for c in range(nsub):
        q_ref[:, c * LANES:(c + 1) * LANES] = q_scr[pl.ds(c, N_KEYS, stride=nsub), :].astype(q_ref.dtype)
        e2_ref[:, c * LANES:(c + 1) * LANES] = e2_scr[pl.ds(c, N_KEYS, stride=nsub), :].astype(e2_ref.dtype)


def _peer_thresh(s3):
    rows, nl, _ = s3.shape
    n = nl * LANES
    nb = min(PEER_THRESH_TOK // LANES, nl)
    half = rows // 2
    ng = nb // PEER_ROW
    row_spec = pl.BlockSpec((N_KEYS, ng, PEER_ROW, LANES), lambda i, h: (h, i, 0, 0))
    row_shape = jax.ShapeDtypeStruct((half, nl // PEER_ROW, PEER_ROW, LANES), F32)
    tile_spec = pl.BlockSpec((N_KEYS, nb * LANES), lambda i, h: (h, i))
    tile_shape = jax.ShapeDtypeStruct((half, n), BF16)
    c, e1, q, e2 = pl.pallas_call(
        _peer_thresh_kernel,
        grid=(nl // nb, PEER_HEADS),
        in_specs=[pl.BlockSpec((2 * N_KEYS, nb, LANES), lambda i, h: (h, i, 0))],
        out_specs=[row_spec, row_spec, tile_spec, tile_spec],
        out_shape=[row_shape, row_shape, tile_shape, tile_shape],
        scratch_shapes=[pltpu.VMEM((N_KEYS * nb, LANES), F32), pltpu.VMEM((N_KEYS * nb, LANES), F32)],
        compiler_params=_params(("parallel", "parallel")),
        name="peer_thresh",
    )(s3)
    shape5 = (PEER_HEADS, N_KEYS, nl // PEER_ROW, PEER_ROW, LANES)
    tiles = lambda v: v.reshape(PEER_HEADS, N_KEYS, n)
    return c.reshape(shape5), e1.reshape(shape5), tiles(q), tiles(e2)


def _peer_dense_kernel(xt_ref, u_ref, vt_ref, q_ref, e2_ref, c_ref, e1_ref, o_ref, acc_scr, a_scr, *, sub):
    e = pl.program_id(1)
    eb = u_ref.shape[0]

    @pl.when(e == 0)
    def _():
        acc_scr[...] = jnp.zeros(acc_scr.shape, F32)

    def row(ref, h, il):
        return jnp.concatenate([ref[h, il, 0, c:c + 1, :] for c in range(ref.shape[3])], axis=1).astype(BF16)

    zero = jnp.zeros((), BF16)

    xt = xt_ref[...]
    nsub = eb // sub
    score = lambda s: _dot(u_ref[s * sub:(s + 1) * sub, :], xt)
    st_next = score(0)
    for s in range(nsub):
        st, st_next = st_next, (score(s + 1) if s + 1 < nsub else None)
        for ii in range(sub // N_KEYS):
            il = s * (sub // N_KEYS) + ii
            gate = None
            for h in range(PEER_HEADS):
                term = row(e1_ref, h, il) * jnp.where(q_ref[h] >= row(c_ref, h, il), e2_ref[h], zero)
                gate = term if gate is None else gate + term
            a = _gelu_x2(st[ii * N_KEYS:(ii + 1) * N_KEYS, :]).astype(BF16) * gate
            a_scr[s * sub + ii * N_KEYS:s * sub + (ii + 1) * N_KEYS, :] = a
        acc_scr[...] += _dot(vt_ref[:, s * sub:(s + 1) * sub], a_scr[s * sub:(s + 1) * sub, :])

    @pl.when(e == pl.num_programs(1) - 1)
    def _():
        o_ref[...] = acc_scr[...].astype(o_ref.dtype)


def _peer_dense(h2t, q, e2, c, e1, u_tab, vt_tab):
    n = h2t.shape[1]
    t = min(n, PEER_ROW * LANES)
    eb = PEER_EXPERT_BLOCK
    row_spec = pl.BlockSpec((PEER_HEADS, eb // N_KEYS, 1, PEER_ROW, LANES), lambda i, e: (0, e, i, 0, 0))
    return pl.pallas_call(
        functools.partial(_peer_dense_kernel, sub=PEER_SUB),
        grid=(n // t, N_EXPERTS // eb),
        in_specs=[
            pl.BlockSpec((D_MODEL, t), lambda i, e: (0, i)),
            pl.BlockSpec((eb, D_MODEL), lambda i, e: (e, 0)),
            pl.BlockSpec((D_MODEL, eb), lambda i, e: (0, e)),
            pl.BlockSpec((PEER_HEADS, N_KEYS, t), lambda i, e: (0, 0, i)),
            pl.BlockSpec((PEER_HEADS, N_KEYS, t), lambda i, e: (0, 0, i)),
            row_spec, row_spec,
        ],
        out_specs=pl.BlockSpec((D_MODEL, t), lambda i, e: (0, i)),
        out_shape=jax.ShapeDtypeStruct((D_MODEL, n), BF16),
        scratch_shapes=[
            pltpu.VMEM((D_MODEL, t), F32),
            pltpu.VMEM((eb, t), BF16),
        ],
        compiler_params=_params(("parallel", "arbitrary")),
        name="peer_dense",
    )(h2t, u_tab, vt_tab, q, e2, c, e1)


def _final_ln_kernel(yt_ref, h2_ref, g_ref, b_ref, o_ref):
    y = yt_ref[...].astype(F32).T
    o_ref[...] = _ln(DEEPNORM_ALPHA * h2_ref[...].astype(F32) + y, g_ref[...], b_ref[...])


def _final_ln(yt, h2, ln_g, ln_b):
    n = h2.shape[0]
    tm = min(n, 512)
    vec = pl.BlockSpec((1, D_MODEL), lambda i: (0, 0))
    return pl.pallas_call(
        _final_ln_kernel,
        grid=(n // tm,),
        in_specs=[
            pl.BlockSpec((D_MODEL, tm), lambda i: (0, i)),
            pl.BlockSpec((tm, D_MODEL), lambda i: (i, 0)),
            vec, vec,
        ],
        out_specs=pl.BlockSpec((tm, D_MODEL), lambda i: (i, 0)),
        out_shape=jax.ShapeDtypeStruct((n, D_MODEL), F32),
        compiler_params=_params(("parallel",)),
        name="final_ln",
    )(yt, h2, ln_g, ln_b)


def _rot_half_cols(w):
    half = QK_ROPE // 2
    return jnp.concatenate([-w[..., half:], w[..., :half]], axis=-1)


def kernel(x, ln_in_g, ln_in_b, w_in, q_norm_g, w_uq, kv_norm_g, w_ukv, ssm_lam_re, ssm_lam_im, ssm_log_dt,
           ssm_b_re, ssm_b_im, ssm_c_re, ssm_c_im, ssm_d, w_glu, w_proj_attn, w_proj_ssm, w_out, ln_mix_g, ln_mix_b,
           w_peer_q, peer_sub_keys, peer_u, peer_v, ln_ffn_g, ln_ffn_b):
    batch, seq, _ = x.shape
    n = batch * seq
    x2 = x.reshape(n, D_MODEL)
    row = lambda v: v.reshape(1, -1).astype(F32)
    l = 0

    wi = w_in[l]
    w_main = jnp.concatenate([wi[:, OFF_SSM:OFF_KR], wi[:, OFF_GA:IN_WIDTH]], axis=1).astype(BF16)
    w_krc = wi[:, OFF_KR:OFF_GA]
    w_kr = jnp.concatenate([w_krc, _rot_half_cols(w_krc)], axis=1).astype(BF16)
    wq = w_uq[l].reshape(Q_LORA, N_HEADS, QK_HEAD)
    wq_ext = jnp.concatenate([wq, _rot_half_cols(wq[..., QK_NOPE:])], axis=-1)
    wq_ext = wq_ext.transpose(1, 0, 2).astype(BF16)
    wkv = w_ukv[l].reshape(KV_LORA, N_HEADS, QK_NOPE + V_HEAD).transpose(1, 0, 2).astype(BF16)

    cs = _rope_table(seq)
    u, z, kr = _inproj(x2, row(ln_in_g), row(ln_in_b), w_main, w_kr)

    q, k, v = _qkv(z, kr, cs, row(q_norm_g[l]), row(kv_norm_g[l]), wq_ext, wkv, batch, seq)
    attn = _attention(q, k, v).reshape(n, N_HEADS * V_HEAD)

    w, ws, wx, a16 = _ssm_wgen(ssm_lam_re[l], ssm_lam_im[l], ssm_log_dt[l], ssm_b_re[l], ssm_b_im[l],
                               ssm_c_re[l], ssm_c_im[l])
    g = _ssm_branch(u, batch, seq, w, ws, wx, a16, row(ssm_d[l]))

    merged = _merge(attn, g, z, w_glu[l].astype(BF16), w_proj_attn[l].astype(BF16), w_proj_ssm[l].astype(BF16))
    h2, h2t = _outproj(merged, x2, row(ln_in_g), row(ln_in_b), row(ln_mix_g[l]), row(ln_mix_b[l]),
                       w_out[l].astype(BF16))

    wcomb = _peer_wcomb(peer_sub_keys[l], w_peer_q[l].astype(BF16))
    s3 = _peer_scores(wcomb, h2t)
    c, e1, q, e2 = _peer_thresh(s3)
    yt = _peer_dense(h2t, q, e2, c, e1, peer_u[l].astype(BF16), peer_v[l].T.astype(BF16))
    out = _final_ln(yt, h2, row(ln_ffn_g[l]), row(ln_ffn_b[l]))
    return out.reshape(batch, seq, D_MODEL)
```

```python
import functools
import math

import jax
import jax.numpy as jnp
from jax import lax
from jax.experimental import pallas as pl
from jax.experimental.pallas import tpu as pltpu

F32 = jnp.float32
BF16 = jnp.bfloat16

D_MODEL = 2048
DEPTH = 1
CHUNK = 64
SSM_WIDTH = 1024
SSM_GROUP = 16
SSM_GROUPS = SSM_WIDTH // SSM_GROUP
SSM_STATE = 64
N_HEADS = 16
Q_LORA = 512
KV_LORA = 512
QK_NOPE = 128
QK_ROPE = 64
V_HEAD = 128
QK_HEAD = QK_NOPE + QK_ROPE
ROPE_THETA = 10000.0
PEER_HEADS = 8
N_KEYS = 128
N_EXPERTS = N_KEYS * N_KEYS
PEER_KEY_HALF = 128
PEER_TOPK = 16
LN_EPS = 1e-5
RMS_EPS = 1e-6
DEEPNORM_ALPHA = (2.0 * DEPTH) ** 0.25
OFF_SSM = 0
OFF_Q = OFF_SSM + SSM_WIDTH
OFF_KV = OFF_Q + Q_LORA
OFF_KR = OFF_KV + KV_LORA
OFF_GA = OFF_KR + QK_ROPE
OFF_GS = OFF_GA + D_MODEL
IN_WIDTH = OFF_GS + D_MODEL

LANES = 128
VMEM_LIMIT_BYTES = 56 * 1024 * 1024

SSM_CHUNK = 16
SSM_TILE_GROUPS = LANES // SSM_GROUP
SSM_TILES = SSM_WIDTH // LANES
SSM_TILE_STATE = SSM_TILE_GROUPS * SSM_STATE
NEG_BIG = -1e30


def _params(semantics):
    return pltpu.CompilerParams(dimension_semantics=semantics, vmem_limit_bytes=VMEM_LIMIT_BYTES)


def _dot(a, b):
    return jnp.dot(a, b, preferred_element_type=F32)


def _ln(x, g, b):
    mu = jnp.mean(x, axis=-1, keepdims=True)
    xc = x - mu
    var = jnp.mean(xc * xc, axis=-1, keepdims=True)
    return xc * lax.rsqrt(var + LN_EPS) * g + b


GELU_C = 0.7978845608028654
GELU_A = 0.044715


def _gelu(x):
    return 0.5 * x * (1.0 + jnp.tanh(GELU_C * (x + GELU_A * (x * x * x))))


def _gelu_x2(x):
    return x * (1.0 + jnp.tanh(x * (GELU_C + (GELU_C * GELU_A) * (x * x))))


def _sigmoid(x):
    return 1.0 / (1.0 + jnp.exp(-x))


def _rope_table_kernel(invf_ref, o_ref):
    ts = o_ref.shape[0]
    pos = (lax.broadcasted_iota(jnp.int32, (ts, LANES), 0) + pl.program_id(0) * ts).astype(F32)
    ang = pos * invf_ref[...]
    lane = lax.broadcasted_iota(jnp.int32, (ts, LANES), 1)
    o_ref[...] = jnp.where(lane < 2 * (QK_ROPE // 2), jnp.cos(ang), jnp.sin(ang))


def _rope_table(seq):
    half = QK_ROPE // 2
    invf = 1.0 / (ROPE_THETA ** (jnp.arange(half, dtype=F32) / half))
    invf4 = jnp.tile(invf, 4)[None, :]
    ts = min(seq, 1024)
    return pl.pallas_call(
        _rope_table_kernel,
        grid=(seq // ts,),
        in_specs=[pl.BlockSpec((1, LANES), lambda i: (0, 0))],
        out_specs=pl.BlockSpec((ts, LANES), lambda i: (i, 0)),
        out_shape=jax.ShapeDtypeStruct((seq, LANES), F32),
        compiler_params=_params(("parallel",)),
        name="rope_table",
    )(invf4)


def _inproj_kernel(x_ref, g_ref, b_ref, w_ref, wkr_ref, u_ref, z_ref, kr_ref, h_scr):
    j = pl.program_id(1)

    @pl.when(j == 0)
    def _():
        h = _ln(x_ref[...], g_ref[...], b_ref[...]).astype(BF16)
        h_scr[...] = h
        u_ref[...] = _dot(h, w_ref[...])
        kr_ref[...] = _dot(h, wkr_ref[...]).astype(kr_ref.dtype)

    @pl.when(j > 0)
    def _():
        z_ref[...] = _dot(h_scr[...], w_ref[...]).astype(z_ref.dtype)


def _inproj(x2, ln_g, ln_b, w_main, w_kr):
    n = x2.shape[0]
    tm = min(n, 1024)
    tn = 1024
    nj = w_main.shape[1] // tn
    return pl.pallas_call(
        _inproj_kernel,
        grid=(n // tm, nj),
        in_specs=[
            pl.BlockSpec((tm, D_MODEL), lambda i, j: (i, 0)),
            pl.BlockSpec((1, D_MODEL), lambda i, j: (0, 0)),
            pl.BlockSpec((1, D_MODEL), lambda i, j: (0, 0)),
            pl.BlockSpec((D_MODEL, tn), lambda i, j: (0, j)),
            pl.BlockSpec((D_MODEL, LANES), lambda i, j: (0, 0)),
        ],
        out_specs=[
            pl.BlockSpec((tm, tn), lambda i, j: (i, 0)),
            pl.BlockSpec((tm, tn), lambda i, j: (i, jnp.maximum(j - 1, 0))),
            pl.BlockSpec((tm, LANES), lambda i, j: (i, 0)),
        ],
        out_shape=[
            jax.ShapeDtypeStruct((n, tn), F32),
            jax.ShapeDtypeStruct((n, (nj - 1) * tn), BF16),
            jax.ShapeDtypeStruct((n, LANES), BF16),
        ],
        scratch_shapes=[pltpu.VMEM((tm, D_MODEL), BF16)],
        compiler_params=_params(("parallel", "arbitrary")),
        name="ln_inproj",
    )(x2, ln_g, ln_b, w_main, w_kr)


def _rope_mix(t2, cs):
    t = t2 * cs
    return t + pltpu.roll(t, QK_ROPE, 1)


def _qkv_kernel(cq_ref, ckv_ref, kr_ref, cs_ref, gq_ref, gkv_ref, wq_ref, wkv_ref,
                q_ref, k_ref, v_ref, cqn_scr, ckvn_scr, kro_scr):
    cs = cs_ref[...]

    @pl.when(pl.program_id(1) == 0)
    def _():
        def rms(c, g):
            c = c.astype(F32)
            return (c * lax.rsqrt(jnp.mean(c * c, axis=-1, keepdims=True) + RMS_EPS) * g).astype(BF16)

        cqn_scr[...] = rms(cq_ref[...], gq_ref[...])
        ckvn_scr[...] = rms(ckv_ref[...], gkv_ref[...])
        kro_scr[...] = _rope_mix(kr_ref[...].astype(F32), cs).astype(BF16)

    scale = math.log2(math.e) / math.sqrt(QK_HEAD)
    hp = wq_ref.shape[0]
    wide = 2 * LANES
    qe2 = _dot(cqn_scr[...], jnp.concatenate([wq_ref[j] for j in range(hp)], axis=1))
    kv2 = _dot(ckvn_scr[...], jnp.concatenate([wkv_ref[j] for j in range(hp)], axis=1))
    for j in range(hp):
        qe = qe2[:, j * wide:(j + 1) * wide]
        qr = _rope_mix(qe[:, QK_NOPE:], cs)
        q_ref[0, j, :, :QK_NOPE] = (qe[:, :QK_NOPE] * scale).astype(BF16)
        q_ref[0, j, :, QK_NOPE:] = (qr[:, :QK_ROPE] * scale).astype(BF16)
        kv = kv2[:, j * wide:(j + 1) * wide]
        k_ref[0, j, :, :QK_NOPE] = kv[:, :QK_NOPE].astype(BF16)
        k_ref[0, j, :, QK_NOPE:] = kro_scr[:, :QK_ROPE]
        v_ref[0, j] = kv[:, QK_NOPE:].astype(BF16)


def _qkv(z, kr, cs, gq, gkv, wq_ext, wkv, batch, seq):
    n = z.shape[0]
    tm = min(seq, 2048)
    nsb = seq // tm
    hp = 2
    return pl.pallas_call(
        _qkv_kernel,
        grid=(n // tm, N_HEADS // hp),
        in_specs=[
            pl.BlockSpec((tm, Q_LORA), lambda i, h: (i, 0)),
            pl.BlockSpec((tm, KV_LORA), lambda i, h: (i, 1)),
            pl.BlockSpec((tm, LANES), lambda i, h: (i, 0)),
            pl.BlockSpec((tm, LANES), lambda i, h: (i % nsb, 0)),
            pl.BlockSpec((1, Q_LORA), lambda i, h: (0, 0)),
            pl.BlockSpec((1, KV_LORA), lambda i, h: (0, 0)),
            pl.BlockSpec((hp, Q_LORA, 2 * LANES), lambda i, h: (h, 0, 0)),
            pl.BlockSpec((hp, KV_LORA, 2 * LANES), lambda i, h: (h, 0, 0)),
        ],
        out_specs=[
            pl.BlockSpec((1, hp, tm, QK_HEAD), lambda i, h: (i // nsb, h, i % nsb, 0)),
            pl.BlockSpec((1, hp, tm, QK_HEAD), lambda i, h: (i // nsb, h, i % nsb, 0)),
            pl.BlockSpec((1, hp, tm, V_HEAD), lambda i, h: (i // nsb, h, i % nsb, 0)),
        ],
        out_shape=[
            jax.ShapeDtypeStruct((batch, N_HEADS, seq, QK_HEAD), BF16),
            jax.ShapeDtypeStruct((batch, N_HEADS, seq, QK_HEAD), BF16),
            jax.ShapeDtypeStruct((batch, N_HEADS, seq, V_HEAD), BF16),
        ],
        scratch_shapes=[
            pltpu.VMEM((tm, Q_LORA), BF16),
            pltpu.VMEM((tm, KV_LORA), BF16),
            pltpu.VMEM((tm, LANES), BF16),
        ],
        compiler_params=_params(("parallel", "arbitrary")),
        name="qkv_rope",
    )(z, z, kr, cs, gq, gkv, wq_ext, wkv)


ATT_ROWS = 512
ATT_CHAINS = 8
ATT_BK_MAIN = 1024


def _attn_kernel(q_ref, k_ref, v_ref, o_ref, m_scr, acc_scr, *, rows, chains, bkm):
    qi = pl.program_id(2)
    m_scr[...] = jnp.full(m_scr.shape, -jnp.inf, F32)
    acc_scr[...] = jnp.zeros(acc_scr.shape, F32)

    def load_kv(j, bk):
        start = pl.multiple_of(j * bk, bk)
        kj = k_ref[0, 0, pl.ds(start, bk), :]
        vj = jnp.concatenate([v_ref[0, 0, pl.ds(start, bk), :], jnp.ones((bk, LANES), BF16)], axis=1)
        return kj, vj

    def block(c, kj, vj, masked):
        q = q_ref[0, 0, c * rows:(c + 1) * rows, :]
        s = lax.dot_general(q, kj, (((1,), (1,)), ((), ())), preferred_element_type=F32)
        if masked:
            rc = lax.broadcasted_iota(jnp.int32, s.shape, 0) // CHUNK
            cc = lax.broadcasted_iota(jnp.int32, s.shape, 1) // CHUNK
            s = jnp.where(cc <= rc, s, NEG_BIG)
        m_prev = m_scr[c]
        m_new = jnp.maximum(m_prev, jnp.max(s, axis=-1, keepdims=True))
        alpha = jnp.exp2(m_prev - m_new)
        p = jnp.exp2(s - jnp.tile(m_new, (1, s.shape[1] // LANES)))
        acc_scr[c] = jnp.tile(alpha, (1, 2)) * acc_scr[c] + _dot(p.astype(BF16), vj)
        m_scr[c] = m_new

    per_step = chains * rows // bkm
    group = 2 if per_step % 2 == 0 else 1

    def body(j, carry):
        for jj in range(group):
            kj, vj = load_kv(j * group + jj, bkm)
            for c in range(chains):
                block(c, kj, vj, False)
        return carry

    lax.fori_loop(0, qi * (per_step // group), body, 0)
    for jj in range(chains):
        kj, vj = load_kv(qi * chains + jj, rows)
        for c in range(jj, chains):
            block(c, kj, vj, c == jj)
    for c in range(chains):
        a = acc_scr[c]
        o_ref[0, c * rows:(c + 1) * rows, :] = (a[:, :V_HEAD] / a[:, V_HEAD:]).astype(o_ref.dtype)


def _attention(q, k, v):
    batch, _, seq, _ = q.shape
    rows = min(seq, ATT_ROWS)
    chains = min(ATT_CHAINS, seq // rows)
    bq = rows * chains
    bkm = min(ATT_BK_MAIN, bq)
    return pl.pallas_call(
        functools.partial(_attn_kernel, rows=rows, chains=chains, bkm=bkm),
        grid=(batch, N_HEADS, seq // bq),
        in_specs=[
            pl.BlockSpec((1, 1, bq, QK_HEAD), lambda b, h, i: (b, h, i, 0)),
            pl.BlockSpec((1, 1, seq, QK_HEAD), lambda b, h, i: (b, h, 0, 0)),
            pl.BlockSpec((1, 1, seq, V_HEAD), lambda b, h, i: (b, h, 0, 0)),
        ],
        out_specs=pl.BlockSpec((1, bq, V_HEAD), lambda b, h, i: (b, i, h)),
        out_shape=jax.ShapeDtypeStruct((batch, seq, N_HEADS * V_HEAD), BF16),
        scratch_shapes=[
            pltpu.VMEM((chains, rows, LANES), F32),
            pltpu.VMEM((chains, rows, 2 * V_HEAD), F32),
        ],
        compiler_params=_params(("parallel", "parallel", "arbitrary")),
        name="flash_attn",
    )(q, k, v)


def _cpow(lr, li, ldt, k):
    dt = jnp.exp(ldt)
    mag = jnp.exp(lr * dt * k)
    ang = li * dt * k
    return mag * jnp.cos(ang), mag * jnp.sin(ang)


def _ssm_wgen_kernel(lr_r, li_r, ldt_r, bre_ref, bim_ref, cre_ref, cim_ref,
                     w_ref, ws_ref, wx_ref, a16_ref):
    hp = lax.Precision.HIGHEST
    ns = SSM_TILE_STATE
    lr, li, ldt = lr_r[0], li_r[0], ldt_r[0]
    a_re, a_im = _cpow(lr, li, ldt, 1.0)
    den = lr * lr + li * li
    nr = a_re - 1.0
    ni = a_im
    cf_re = (nr * lr + ni * li) / den
    cf_im = (ni * lr - nr * li) / den
    bre, bim = bre_ref[0], bim_ref[0]
    cre, cim = cre_ref[0], cim_ref[0]
    w_ref[...] = jnp.zeros(w_ref.shape, w_ref.dtype)
    for k in range(SSM_CHUNK):
        p_re, p_im = _cpow(lr, li, ldt, float(k))
        m_re = cf_re * p_re - cf_im * p_im
        m_im = cf_re * p_im + cf_im * p_re
        bm_re = bre * m_re - bim * m_im
        bm_im = bre * m_im + bim * m_re
        bd = (jnp.dot(bm_re, cre, precision=hp, preferred_element_type=F32)
              - jnp.dot(bm_im, cim, precision=hp, preferred_element_type=F32)).astype(BF16)
        for j in range(SSM_CHUNK - k):
            i = j + k
            w_ref[0, j * LANES:(j + 1) * LANES, i * LANES:(i + 1) * LANES] = bd
        jj = SSM_CHUNK - 1 - k
        ws_ref[0, jj * LANES:(jj + 1) * LANES, :ns] = bm_re.astype(BF16)
        ws_ref[0, jj * LANES:(jj + 1) * LANES, ns:] = bm_im.astype(BF16)
    cre_t, cim_t = cre.T, cim.T
    for i in range(SSM_CHUNK):
        p_re, p_im = _cpow(lr, li, ldt, float(i + 1))
        wx_ref[0, :ns, i * LANES:(i + 1) * LANES] = (p_re * cre_t - p_im * cim_t).T.astype(BF16)
        wx_ref[0, ns:, i * LANES:(i + 1) * LANES] = (-(p_re * cim_t + p_im * cre_t)).T.astype(BF16)
    q_re, q_im = _cpow(lr, li, ldt, float(SSM_CHUNK))
    a16_ref[0, :, :ns] = q_re
    a16_ref[0, :, ns:] = q_im


def _ssm_wgen(lam_re, lam_im, log_dt, b_re, b_im, c_re, c_im):
    nt, tg, ns = SSM_TILES, SSM_TILE_GROUPS, SSM_TILE_STATE
    eye = jnp.eye(tg, dtype=F32)
    bbd = lambda b: jnp.einsum("cgph,gk->cghkp", b.reshape(nt, tg, SSM_STATE, SSM_GROUP), eye).reshape(nt, LANES, ns)
    cbd = lambda c: jnp.einsum("cghp,gk->cgpkh", c.reshape(nt, tg, SSM_GROUP, SSM_STATE), eye).reshape(nt, ns, LANES)
    ldt = jnp.repeat(log_dt, SSM_STATE)
    rows = [a.reshape(nt, 1, ns) for a in (lam_re, lam_im, ldt)]
    kc = SSM_CHUNK * LANES
    row_spec = pl.BlockSpec((1, 1, ns), lambda c: (c, 0, 0))
    return pl.pallas_call(
        _ssm_wgen_kernel,
        grid=(nt,),
        in_specs=[row_spec] * 3 + [
            pl.BlockSpec((1, LANES, ns), lambda c: (c, 0, 0)),
            pl.BlockSpec((1, LANES, ns), lambda c: (c, 0, 0)),
            pl.BlockSpec((1, ns, LANES), lambda c: (c, 0, 0)),
            pl.BlockSpec((1, ns, LANES), lambda c: (c, 0, 0)),
        ],
        out_specs=[
            pl.BlockSpec((1, kc, kc), lambda c: (c, 0, 0)),
            pl.BlockSpec((1, kc, 2 * ns), lambda c: (c, 0, 0)),
            pl.BlockSpec((1, 2 * ns, kc), lambda c: (c, 0, 0)),
            pl.BlockSpec((1, 1, 2 * ns), lambda c: (c, 0, 0)),
        ],
        out_shape=[
            jax.ShapeDtypeStruct((nt, kc, kc), BF16),
            jax.ShapeDtypeStruct((nt, kc, 2 * ns), BF16),
            jax.ShapeDtypeStruct((nt, 2 * ns, kc), BF16),
            jax.ShapeDtypeStruct((nt, 1, 2 * ns), F32),
        ],
        compiler_params=_params(("parallel",)),
        name="ssm_wgen",
    )(*rows, bbd(b_re), bbd(b_im), cbd(c_re), cbd(c_im))


def _chunk_rows(u_ref, tb):
    return [u_ref[pl.ds(j, tb, stride=SSM_CHUNK), :] for j in range(SSM_CHUNK)]


def _ssm_sum_kernel(u_ref, ws_ref, s_ref, *, tb):
    u16 = jnp.concatenate(_chunk_rows(u_ref, tb), axis=1).astype(BF16)
    s_ref[0] = _dot(u16, ws_ref[0])


def _ssm_scan_kernel(s_ref, a16_ref, x_ref):
    ns = SSM_TILE_STATE
    a_re = a16_ref[0, :, :ns]
    a_im = a16_ref[0, :, ns:]
    nrow = s_ref.shape[1]

    def body(r, carry):
        xr, xi = carry
        x_ref[0, pl.ds(r, 1), :ns] = xr
        x_ref[0, pl.ds(r, 1), ns:] = xi
        sr = s_ref[0, pl.ds(r, 1), :ns]
        si = s_ref[0, pl.ds(r, 1), ns:]
        return a_re * xr - a_im * xi + sr, a_re * xi + a_im * xr + si

    zero = jnp.zeros((1, ns), F32)
    lax.fori_loop(0, nrow, body, (zero, zero), unroll=8)


def _ssm_out_kernel(u_ref, x_ref, w_ref, wx_ref, d_ref, g_ref, *, tb):
    us = _chunk_rows(u_ref, tb)
    u16 = jnp.concatenate(us, axis=1).astype(BF16)
    y16 = _dot(u16, w_ref[0]) + _dot(x_ref[0].astype(BF16), wx_ref[0])
    d = d_ref[...]
    for i in range(SSM_CHUNK):
        y = y16[:, i * LANES:(i + 1) * LANES] + d * us[i]
        g_ref[pl.ds(i, tb, stride=SSM_CHUNK), :] = _gelu(y)


def _ssm_branch(u, batch, seq, w, ws, wx, a16, d_skip):
    n = u.shape[0]
    nt, ns = SSM_TILES, SSM_TILE_STATE
    kc = SSM_CHUNK * LANES
    nb = n // SSM_CHUNK
    nbs = seq // SSM_CHUNK
    tb = min(nbs, 512)
    tok = tb * SSM_CHUNK
    s = pl.pallas_call(
        functools.partial(_ssm_sum_kernel, tb=tb),
        grid=(nt, nb // tb),
        in_specs=[
            pl.BlockSpec((tok, LANES), lambda c, r: (r, c)),
            pl.BlockSpec((1, kc, 2 * ns), lambda c, r: (c, 0, 0)),
        ],
        out_specs=pl.BlockSpec((1, tb, 2 * ns), lambda c, r: (c, r, 0)),
        out_shape=jax.ShapeDtypeStruct((nt, nb, 2 * ns), F32),
        compiler_params=_params(("parallel", "parallel")),
        name="ssm_chunk_sum",
    )(u, ws)
    x = pl.pallas_call(
        _ssm_scan_kernel,
        grid=(nt, batch),
        in_specs=[
            pl.BlockSpec((1, nbs, 2 * ns), lambda c, b: (c, b, 0)),
            pl.BlockSpec((1, 1, 2 * ns), lambda c, b: (c, 0, 0)),
        ],
        out_specs=pl.BlockSpec((1, nbs, 2 * ns), lambda c, b: (c, b, 0)),
        out_shape=jax.ShapeDtypeStruct((nt, nb, 2 * ns), F32),
        compiler_params=_params(("parallel", "parallel")),
        name="ssm_state_scan",
    )(s, a16)
    return pl.pallas_call(
        functools.partial(_ssm_out_kernel, tb=tb),
        grid=(nt, nb // tb),
        in_specs=[
            pl.BlockSpec((tok, LANES), lambda c, r: (r, c)),
            pl.BlockSpec((1, tb, 2 * ns), lambda c, r: (c, r, 0)),
            pl.BlockSpec((1, kc, kc), lambda c, r: (c, 0, 0)),
            pl.BlockSpec((1, 2 * ns, kc), lambda c, r: (c, 0, 0)),
            pl.BlockSpec((1, LANES), lambda c, r: (0, c)),
        ],
        out_specs=pl.BlockSpec((tok, LANES), lambda c, r: (r, c)),
        out_shape=jax.ShapeDtypeStruct((n, SSM_WIDTH), F32),
        compiler_params=_params(("parallel", "parallel")),
        name="ssm_chunk_out",
    )(u, x, w, wx, d_skip)


def _merge_kernel(attn_ref, g_ref, ga_ref, gs_ref, wglu_ref, wpa_ref, wps_ref, o_ref, ssm_scr):
    @pl.when(pl.program_id(1) == 0)
    def _():
        g = g_ref[...]
        ssm_scr[...] = (g * _sigmoid(_dot(g.astype(BF16), wglu_ref[...]))).astype(BF16)

    pa = _dot(attn_ref[...], wpa_ref[...])
    ps = _dot(ssm_scr[...], wps_ref[...])
    o_ref[...] = (_sigmoid(ga_ref[...].astype(F32)) * pa + _sigmoid(gs_ref[...].astype(F32)) * ps).astype(o_ref.dtype)


def _merge(attn, g, z, w_glu, w_pa, w_ps):
    n = attn.shape[0]
    tm = min(n, 1024)
    tn = 1024
    nj = D_MODEL // tn
    return pl.pallas_call(
        _merge_kernel,
        grid=(n // tm, nj),
        in_specs=[
            pl.BlockSpec((tm, N_HEADS * V_HEAD), lambda i, j: (i, 0)),
            pl.BlockSpec((tm, SSM_WIDTH), lambda i, j: (i, 0)),
            pl.BlockSpec((tm, tn), lambda i, j: (i, 1 + j)),
            pl.BlockSpec((tm, tn), lambda i, j: (i, 1 + nj + j)),
            pl.BlockSpec((SSM_WIDTH, SSM_WIDTH), lambda i, j: (0, 0)),
            pl.BlockSpec((N_HEADS * V_HEAD, tn), lambda i, j: (0, j)),
            pl.BlockSpec((SSM_WIDTH, tn), lambda i, j: (0, j)),
        ],
        out_specs=pl.BlockSpec((tm, tn), lambda i, j: (i, j)),
        out_shape=jax.ShapeDtypeStruct((n, D_MODEL), BF16),
        scratch_shapes=[pltpu.VMEM((tm, SSM_WIDTH), BF16)],
        compiler_params=_params(("parallel", "arbitrary")),
        name="glu_merge",
    )(attn, g, z, z, w_glu, w_pa, w_ps)


def _outproj_kernel(m_ref, x_ref, g1_ref, b1_ref, g2_ref, b2_ref, w_ref, h2_ref, h2t_ref):
    y = _dot(m_ref[...], w_ref[...])
    h = _ln(x_ref[...], g1_ref[...], b1_ref[...])
    h2 = _ln(DEEPNORM_ALPHA * h + y, g2_ref[...], b2_ref[...])
    h2_ref[...] = h2.astype(h2_ref.dtype)
    h2t_ref[...] = h2.T.astype(h2t_ref.dtype)


def _outproj(merged, x2, g1, b1, g2, b2, w_out):
    n = merged.shape[0]
    tm = min(n, 512)
    vec = pl.BlockSpec((1, D_MODEL), lambda i: (0, 0))
    return pl.pallas_call(
        _outproj_kernel,
        grid=(n // tm,),
        in_specs=[
            pl.BlockSpec((tm, D_MODEL), lambda i: (i, 0)),
            pl.BlockSpec((tm, D_MODEL), lambda i: (i, 0)),
            vec, vec, vec, vec,
            pl.BlockSpec((D_MODEL, D_MODEL), lambda i: (0, 0)),
        ],
        out_specs=[
            pl.BlockSpec((tm, D_MODEL), lambda i: (i, 0)),
            pl.BlockSpec((D_MODEL, tm), lambda i: (0, i)),
        ],
        out_shape=[
            jax.ShapeDtypeStruct((n, D_MODEL), BF16),
            jax.ShapeDtypeStruct((D_MODEL, n), BF16),
        ],
        compiler_params=_params(("parallel",)),
        name="outproj_ln",
    )(merged, x2, g1, b1, g2, b2, w_out)


def _peer_wcomb_kernel(keys_ref, wpq_ref, o_ref):
    o_ref[...] = lax.dot_general(keys_ref[0].astype(BF16), wpq_ref[...], (((1,), (1,)), ((), ())),
                                 preferred_element_type=F32).astype(o_ref.dtype)


def _peer_wcomb(sub_keys, w_pq):
    nhc = PEER_HEADS * 2
    keys = sub_keys.reshape(nhc, N_KEYS, PEER_KEY_HALF)
    return pl.pallas_call(
        _peer_wcomb_kernel,
        grid=(nhc,),
        in_specs=[
            pl.BlockSpec((1, N_KEYS, PEER_KEY_HALF), lambda i: (i, 0, 0)),
            pl.BlockSpec((D_MODEL, PEER_KEY_HALF), lambda i: (0, i)),
        ],
        out_specs=pl.BlockSpec((N_KEYS, D_MODEL), lambda i: (i, 0)),
        out_shape=jax.ShapeDtypeStruct((nhc * N_KEYS, D_MODEL), BF16),
        compiler_params=_params(("parallel",)),
        name="peer_wcomb",
    )(keys, w_pq)


PEER_TOK = 1024
PEER_THRESH_TOK = 2048


def _peer_scores_kernel(w_ref, xt_ref, s3_ref):
    rows = w_ref.shape[0] // 2
    for g in range(2):
        st = _dot(w_ref[g * rows:(g + 1) * rows, :], xt_ref[...])
        for c in range(s3_ref.shape[1]):
            s3_ref[g * rows:(g + 1) * rows, c, :] = st[:, c * LANES:(c + 1) * LANES]


def _peer_scores(wcomb, h2t):
    n = h2t.shape[1]
    tl = min(n, PEER_TOK)
    rows = wcomb.shape[0]
    return pl.pallas_call(
        _peer_scores_kernel,
        grid=(n // tl,),
        in_specs=[
            pl.BlockSpec((rows, D_MODEL), lambda i: (0, 0)),
            pl.BlockSpec((D_MODEL, tl), lambda i: (0, i)),
        ],
        out_specs=pl.BlockSpec((rows, tl // LANES, LANES), lambda i: (0, i, 0)),
        out_shape=jax.ShapeDtypeStruct((rows, n // LANES, LANES), F32),
        compiler_params=_params(("parallel",)),
        name="peer_scores",
    )(wcomb, h2t)


PEER_LIST = PEER_TOPK + 1
PEER_EXPERT_BLOCK = 16 * N_KEYS
PEER_SUB = 8 * N_KEYS


def _insert_sorted(t, v):
    out = []
    for r in range(len(t)):
        out.append(jnp.maximum(t[r], v))
        v = jnp.minimum(t[r], v)
    return out


PEER_ROW = 4


def _bitonic_merge(x, desc):
    n = len(x)
    if n == 1:
        return x
    h = n // 2
    big = [jnp.maximum(x[i], x[i + h]) for i in range(h)]
    small = [jnp.minimum(x[i], x[i + h]) for i in range(h)]
    first, second = (big, small) if desc else (small, big)
    return _bitonic_merge(first, desc) + _bitonic_merge(second, desc)


def _bitonic_sort(x, desc):
    n = len(x)
    if n == 1:
        return x
    h = n // 2
    return _bitonic_merge(_bitonic_sort(x[:h], True) + _bitonic_sort(x[h:], False), desc)


def _merge_top(x, y):
    n = len(x)
    big = [jnp.maximum(x[i], y[n - 1 - i]) for i in range(n)]
    rest = functools.reduce(jnp.maximum, [jnp.minimum(x[i], y[n - 1 - i]) for i in range(n)])
    return _bitonic_merge(big, True), rest


def _count_above(vals, x, strict):
    above = (lambda v: v > x) if strict else (lambda v: v >= x)
    pick = lambda m, hi, lo: jnp.where(m, hi, lo)
    m8 = above(vals[7])
    m4 = above(pick(m8, vals[11], vals[3]))
    m2 = above(pick(m8, pick(m4, vals[13], vals[9]), pick(m4, vals[5], vals[1])))
    hi = pick(m4, pick(m2, vals[14], vals[12]), pick(m2, vals[10], vals[8]))
    lo = pick(m4, pick(m2, vals[6], vals[4]), pick(m2, vals[2], vals[0]))
    m1 = above(pick(m8, hi, lo))
    cnt = (pick(m8, 8.0, 0.0) + pick(m4, 4.0, 0.0)) + (pick(m2, 2.0, 0.0) + pick(m1, 1.0, 0.0))
    return pick(above(vals[15]), 16.0, cnt)


def _peer_thresh_kernel(s_ref, c_ref, e1_ref, q_ref, e2_ref, q_scr, e2_scr):
    shape = s_ref.shape[1:]
    neg = jnp.full(shape, -jnp.inf, F32)

    def top_list(base):
        def tree(lo, hi):
            if hi - lo == PEER_TOPK:
                return _bitonic_sort([s_ref[base + n] for n in range(lo, hi)], True), neg
            mid = (lo + hi) // 2
            (x, dx), (y, dy) = tree(lo, mid), tree(mid, hi)
            top, rest = _merge_top(x, y)
            return top, jnp.maximum(jnp.maximum(dx, dy), rest)

        top, nxt = tree(0, N_KEYS)
        return top + [nxt]

    a = top_list(0)
    b = top_list(N_KEYS)
    t = [neg] * PEER_LIST
    for i in range(1, PEER_LIST + 1):
        for j in range(1, PEER_LIST // i + 1):
            t = _insert_sorted(t, a[i - 1] + b[j - 1])
    tau = 0.5 * (t[PEER_TOPK - 1] + t[PEER_TOPK])
    top = a[0] + b[0]
    z = jnp.exp(t[0] - top)
    for r in range(1, PEER_TOPK):
        z = z + jnp.exp(t[r] - top)
    cv = b[0] + jnp.log(2.0 * z)
    a16 = a[:PEER_TOPK]

    def write(n, carry):
        s1 = s_ref[n]
        s2 = s_ref[N_KEYS + n]
        c = 1.0 + _count_above(a16, s1, True)
        e1 = jnp.exp(s1 - a16[0])
        for gq in range(shape[0] // PEER_ROW):
            c_ref[n, gq] = c[gq * PEER_ROW:(gq + 1) * PEER_ROW, :]
            e1_ref[n, gq] = e1[gq * PEER_ROW:(gq + 1) * PEER_ROW, :]
        row0 = pl.multiple_of(n * nsub, nsub)
        q_scr[pl.ds(row0, nsub), :] = _count_above(a16, tau - s2, False)
        e2_scr[pl.ds(row0, nsub), :] = jnp.exp(s2 - cv)
        return carry

    nsub = shape[0]
    lax.fori_loop(0, N_KEYS, write, 0, unroll=2)
    for c in range(nsub):
        q_ref[:, c * LANES:(c + 1) * LANES] = q_scr[pl.ds(c, N_KEYS, stride=nsub), :].astype(q_ref.dtype)
        e2_ref[:, c * LANES:(c + 1) * LANES] = e2_scr[pl.ds(c, N_KEYS, stride=nsub), :].astype(e2_ref.dtype)


def _peer_thresh(s3):
    rows, nl, _ = s3.shape
    n = nl * LANES
    nb = min(PEER_THRESH_TOK // LANES, nl)
    half = rows // 2
    ng = nb // PEER_ROW
    row_spec = pl.BlockSpec((N_KEYS, ng, PEER_ROW, LANES), lambda i, h: (h, i, 0, 0))
    row_shape = jax.ShapeDtypeStruct((half, nl // PEER_ROW, PEER_ROW, LANES), F32)
    tile_spec = pl.BlockSpec((N_KEYS, nb * LANES), lambda i, h: (h, i))
    tile_shape = jax.ShapeDtypeStruct((half, n), BF16)
    c, e1, q, e2 = pl.pallas_call(
        _peer_thresh_kernel,
        grid=(nl // nb, PEER_HEADS),
        in_specs=[pl.BlockSpec((2 * N_KEYS, nb, LANES), lambda i, h: (h, i, 0))],
        out_specs=[row_spec, row_spec, tile_spec, tile_spec],
        out_shape=[row_shape, row_shape, tile_shape, tile_shape],
        scratch_shapes=[pltpu.VMEM((N_KEYS * nb, LANES), F32), pltpu.VMEM((N_KEYS * nb, LANES), F32)],
        compiler_params=_params(("parallel", "parallel")),
        name="peer_thresh",
    )(s3)
    shape5 = (PEER_HEADS, N_KEYS, nl // PEER_ROW, PEER_ROW, LANES)
    tiles = lambda v: v.reshape(PEER_HEADS, N_KEYS, n)
    return c.reshape(shape5), e1.reshape(shape5), tiles(q), tiles(e2)


def _peer_dense_kernel(xt_ref, u_ref, vt_ref, q_ref, e2_ref, c_ref, e1_ref, o_ref, acc_scr, a_scr, *, sub):
    e = pl.program_id(1)
    eb = u_ref.shape[0]

    @pl.when(e == 0)
    def _():
        acc_scr[...] = jnp.zeros(acc_scr.shape, F32)

    def row(ref, h, il):
        return jnp.concatenate([ref[h, il, 0, c:c + 1, :] for c in range(ref.shape[3])], axis=1).astype(BF16)

    zero = jnp.zeros((), BF16)

    xt = xt_ref[...]
    nsub = eb // sub
    score = lambda s: _dot(u_ref[s * sub:(s + 1) * sub, :], xt)
    st_next = score(0)
    for s in range(nsub):
        st, st_next = st_next, (score(s + 1) if s + 1 < nsub else None)
        for ii in range(sub // N_KEYS):
            il = s * (sub // N_KEYS) + ii
            gate = None
            for h in range(PEER_HEADS):
                term = row(e1_ref, h, il) * jnp.where(q_ref[h] >= row(c_ref, h, il), e2_ref[h], zero)
                gate = term if gate is None else gate + term
            a = _gelu_x2(st[ii * N_KEYS:(ii + 1) * N_KEYS, :]).astype(BF16) * gate
            a_scr[s * sub + ii * N_KEYS:s * sub + (ii + 1) * N_KEYS, :] = a
        acc_scr[...] += _dot(vt_ref[:, s * sub:(s + 1) * sub], a_scr[s * sub:(s + 1) * sub, :])

    @pl.when(e == pl.num_programs(1) - 1)
    def _():
        o_ref[...] = acc_scr[...].astype(o_ref.dtype)


def _peer_dense(h2t, q, e2, c, e1, u_tab, vt_tab):
    n = h2t.shape[1]
    t = min(n, PEER_ROW * LANES)
    eb = PEER_EXPERT_BLOCK
    row_spec = pl.BlockSpec((PEER_HEADS, eb // N_KEYS, 1, PEER_ROW, LANES), lambda i, e: (0, e, i, 0, 0))
    return pl.pallas_call(
        functools.partial(_peer_dense_kernel, sub=PEER_SUB),
        grid=(n // t, N_EXPERTS // eb),
        in_specs=[
            pl.BlockSpec((D_MODEL, t), lambda i, e: (0, i)),
            pl.BlockSpec((eb, D_MODEL), lambda i, e: (e, 0)),
            pl.BlockSpec((D_MODEL, eb), lambda i, e: (0, e)),
            pl.BlockSpec((PEER_HEADS, N_KEYS, t), lambda i, e: (0, 0, i)),
            pl.BlockSpec((PEER_HEADS, N_KEYS, t), lambda i, e: (0, 0, i)),
            row_spec, row_spec,
        ],
        out_specs=pl.BlockSpec((D_MODEL, t), lambda i, e: (0, i)),
        out_shape=jax.ShapeDtypeStruct((D_MODEL, n), BF16),
        scratch_shapes=[
            pltpu.VMEM((D_MODEL, t), F32),
            pltpu.VMEM((eb, t), BF16),
        ],
        compiler_params=_params(("parallel", "arbitrary")),
        name="peer_dense",
    )(h2t, u_tab, vt_tab, q, e2, c, e1)


def _final_ln_kernel(yt_ref, h2_ref, g_ref, b_ref, o_ref):
    y = yt_ref[...].astype(F32).T
    o_ref[...] = _ln(DEEPNORM_ALPHA * h2_ref[...].astype(F32) + y, g_ref[...], b_ref[...])


def _final_ln(yt, h2, ln_g, ln_b):
    n = h2.shape[0]
    tm = min(n, 512)
    vec = pl.BlockSpec((1, D_MODEL), lambda i: (0, 0))
    return pl.pallas_call(
        _final_ln_kernel,
        grid=(n // tm,),
        in_specs=[
            pl.BlockSpec((D_MODEL, tm), lambda i: (0, i)),
            pl.BlockSpec((tm, D_MODEL), lambda i: (i, 0)),
            vec, vec,
        ],
        out_specs=pl.BlockSpec((tm, D_MODEL), lambda i: (i, 0)),
        out_shape=jax.ShapeDtypeStruct((n, D_MODEL), F32),
        compiler_params=_params(("parallel",)),
        name="final_ln",
    )(yt, h2, ln_g, ln_b)


def _rot_half_cols(w):
    half = QK_ROPE // 2
    return jnp.concatenate([-w[..., half:], w[..., :half]], axis=-1)


def kernel(x, ln_in_g, ln_in_b, w_in, q_norm_g, w_uq, kv_norm_g, w_ukv, ssm_lam_re, ssm_lam_im, ssm_log_dt,
           ssm_b_re, ssm_b_im, ssm_c_re, ssm_c_im, ssm_d, w_glu, w_proj_attn, w_proj_ssm, w_out, ln_mix_g, ln_mix_b,
           w_peer_q, peer_sub_keys, peer_u, peer_v, ln_ffn_g, ln_ffn_b):
    batch, seq, _ = x.shape
    n = batch * seq
    x2 = x.reshape(n, D_MODEL)
    row = lambda v: v.reshape(1, -1).astype(F32)
    l = 0

    wi = w_in[l]
    w_main = jnp.concatenate([wi[:, OFF_SSM:OFF_KR], wi[:, OFF_GA:IN_WIDTH]], axis=1).astype(BF16)
    w_krc = wi[:, OFF_KR:OFF_GA]
    w_kr = jnp.concatenate([w_krc, _rot_half_cols(w_krc)], axis=1).astype(BF16)
    wq = w_uq[l].reshape(Q_LORA, N_HEADS, QK_HEAD)
    wq_ext = jnp.concatenate([wq, _rot_half_cols(wq[..., QK_NOPE:])], axis=-1)
    wq_ext = wq_ext.transpose(1, 0, 2).astype(BF16)
    wkv = w_ukv[l].reshape(KV_LORA, N_HEADS, QK_NOPE + V_HEAD).transpose(1, 0, 2).astype(BF16)

    cs = _rope_table(seq)
    u, z, kr = _inproj(x2, row(ln_in_g), row(ln_in_b), w_main, w_kr)

    q, k, v = _qkv(z, kr, cs, row(q_norm_g[l]), row(kv_norm_g[l]), wq_ext, wkv, batch, seq)
    attn = _attention(q, k, v).reshape(n, N_HEADS * V_HEAD)

    w, ws, wx, a16 = _ssm_wgen(ssm_lam_re[l], ssm_lam_im[l], ssm_log_dt[l], ssm_b_re[l], ssm_b_im[l],
                               ssm_c_re[l], ssm_c_im[l])
    g = _ssm_branch(u, batch, seq, w, ws, wx, a16, row(ssm_d[l]))

    merged = _merge(attn, g, z, w_glu[l].astype(BF16), w_proj_attn[l].astype(BF16), w_proj_ssm[l].astype(BF16))
    h2, h2t = _outproj(merged, x2, row(ln_in_g), row(ln_in_b), row(ln_mix_g[l]), row(ln_mix_b[l]),
                       w_out[l].astype(BF16))

    wcomb = _peer_wcomb(peer_sub_keys[l], w_peer_q[l].astype(BF16))
    s3 = _peer_scores(wcomb, h2t)
    c, e1, q, e2 = _peer_thresh(s3)
    yt = _peer_dense(h2t, q, e2, c, e1, peer_u[l].astype(BF16), peer_v[l].T.astype(BF16))
    out = _final_ln(yt, h2, row(ln_ffn_g[l]), row(ln_ffn_b[l]))
    return out.reshape(batch, seq, D_MODEL)
```

```python
import functools
import math

import jax
import jax.numpy as jnp
from jax import lax
from jax.experimental import pallas as pl
from jax.experimental.pallas import tpu as pltpu

F32 = jnp.float32
BF16 = jnp.bfloat16

D_MODEL = 2048
DEPTH = 1
CHUNK = 64
SSM_WIDTH = 1024
SSM_GROUP = 16
SSM_GROUPS = SSM_WIDTH // SSM_GROUP
SSM_STATE = 64
N_HEADS = 16
Q_LORA = 512
KV_LORA = 512
QK_NOPE = 128
QK_ROPE = 64
V_HEAD = 128
QK_HEAD = QK_NOPE + QK_ROPE
ROPE_THETA = 10000.0
PEER_HEADS = 8
N_KEYS = 128
N_EXPERTS = N_KEYS * N_KEYS
PEER_KEY_HALF = 128
PEER_TOPK = 16
LN_EPS = 1e-5
RMS_EPS = 1e-6
DEEPNORM_ALPHA = (2.0 * DEPTH) ** 0.25
OFF_SSM = 0
OFF_Q = OFF_SSM + SSM_WIDTH
OFF_KV = OFF_Q + Q_LORA
OFF_KR = OFF_KV + KV_LORA
OFF_GA = OFF_KR + QK_ROPE
OFF_GS = OFF_GA + D_MODEL
IN_WIDTH = OFF_GS + D_MODEL

LANES = 128
VMEM_LIMIT_BYTES = 56 * 1024 * 1024

SSM_CHUNK = 16
SSM_TILE_GROUPS = LANES // SSM_GROUP
SSM_TILES = SSM_WIDTH // LANES
SSM_TILE_STATE = SSM_TILE_GROUPS * SSM_STATE
NEG_BIG = -1e30


def _params(semantics):
    return pltpu.CompilerParams(dimension_semantics=semantics, vmem_limit_bytes=VMEM_LIMIT_BYTES)


def _dot(a, b):
    return jnp.dot(a, b, preferred_element_type=F32)


def _ln(x, g, b):
    mu = jnp.mean(x, axis=-1, keepdims=True)
    xc = x - mu
    var = jnp.mean(xc * xc, axis=-1, keepdims=True)
    return xc * lax.rsqrt(var + LN_EPS) * g + b


GELU_C = 0.7978845608028654
GELU_A = 0.044715


def _gelu(x):
    return 0.5 * x * (1.0 + jnp.tanh(GELU_C * (x + GELU_A * (x * x * x))))


def _gelu_x2(x):
    return x * (1.0 + jnp.tanh(x * (GELU_C + (GELU_C * GELU_A) * (x * x))))


def _sigmoid(x):
    return 1.0 / (1.0 + jnp.exp(-x))


def _rope_table_kernel(invf_ref, o_ref):
    ts = o_ref.shape[0]
    pos = (lax.broadcasted_iota(jnp.int32, (ts, LANES), 0) + pl.program_id(0) * ts).astype(F32)
    ang = pos * invf_ref[...]
    lane = lax.broadcasted_iota(jnp.int32, (ts, LANES), 1)
    o_ref[...] = jnp.where(lane < 2 * (QK_ROPE // 2), jnp.cos(ang), jnp.sin(ang))


def _rope_table(seq):
    half = QK_ROPE // 2
    invf = 1.0 / (ROPE_THETA ** (jnp.arange(half, dtype=F32) / half))
    invf4 = jnp.tile(invf, 4)[None, :]
    ts = min(seq, 1024)
    return pl.pallas_call(
        _rope_table_kernel,
        grid=(seq // ts,),
        in_specs=[pl.BlockSpec((1, LANES), lambda i: (0, 0))],
        out_specs=pl.BlockSpec((ts, LANES), lambda i: (i, 0)),
        out_shape=jax.ShapeDtypeStruct((seq, LANES), F32),
        compiler_params=_params(("parallel",)),
        name="rope_table",
    )(invf4)


def _inproj_kernel(x_ref, g_ref, b_ref, w_ref, wkr_ref, u_ref, z_ref, kr_ref, h_scr):
    j = pl.program_id(1)

    @pl.when(j == 0)
    def _():
        h = _ln(x_ref[...], g_ref[...], b_ref[...]).astype(BF16)
        h_scr[...] = h
        u_ref[...] = _dot(h, w_ref[...])
        kr_ref[...] = _dot(h, wkr_ref[...]).astype(kr_ref.dtype)

    @pl.when(j > 0)
    def _():
        z_ref[...] = _dot(h_scr[...], w_ref[...]).astype(z_ref.dtype)


def _inproj(x2, ln_g, ln_b, w_main, w_kr):
    n = x2.shape[0]
    tm = min(n, 1024)
    tn = 1024
    nj = w_main.shape[1] // tn
    return pl.pallas_call(
        _inproj_kernel,
        grid=(n // tm, nj),
        in_specs=[
            pl.BlockSpec((tm, D_MODEL), lambda i, j: (i, 0)),
            pl.BlockSpec((1, D_MODEL), lambda i, j: (0, 0)),
            pl.BlockSpec((1, D_MODEL), lambda i, j: (0, 0)),
            pl.BlockSpec((D_MODEL, tn), lambda i, j: (0, j)),
            pl.BlockSpec((D_MODEL, LANES), lambda i, j: (0, 0)),
        ],
        out_specs=[
            pl.BlockSpec((tm, tn), lambda i, j: (i, 0)),
            pl.BlockSpec((tm, tn), lambda i, j: (i, jnp.maximum(j - 1, 0))),
            pl.BlockSpec((tm, LANES), lambda i, j: (i, 0)),
        ],
        out_shape=[
            jax.ShapeDtypeStruct((n, tn), F32),
            jax.ShapeDtypeStruct((n, (nj - 1) * tn), BF16),
            jax.ShapeDtypeStruct((n, LANES), BF16),
        ],
        scratch_shapes=[pltpu.VMEM((tm, D_MODEL), BF16)],
        compiler_params=_params(("parallel", "arbitrary")),
        name="ln_inproj",
    )(x2, ln_g, ln_b, w_main, w_kr)


def _rope_mix(t2, cs):
    t = t2 * cs
    return t + pltpu.roll(t, QK_ROPE, 1)


def _qkv_kernel(cq_ref, ckv_ref, kr_ref, cs_ref, gq_ref, gkv_ref, wq_ref, wkv_ref,
                q_ref, k_ref, v_ref, cqn_scr, ckvn_scr, kro_scr):
    cs = cs_ref[...]

    @pl.when(pl.program_id(1) == 0)
    def _():
        def rms(c, g):
            c = c.astype(F32)
            return (c * lax.rsqrt(jnp.mean(c * c, axis=-1, keepdims=True) + RMS_EPS) * g).astype(BF16)

        cqn_scr[...] = rms(cq_ref[...], gq_ref[...])
        ckvn_scr[...] = rms(ckv_ref[...], gkv_ref[...])
        kro_scr[...] = _rope_mix(kr_ref[...].astype(F32), cs).astype(BF16)

    scale = math.log2(math.e) / math.sqrt(QK_HEAD)
    hp = wq_ref.shape[0]
    wide = 2 * LANES
    qe2 = _dot(cqn_scr[...], jnp.concatenate([wq_ref[j] for j in range(hp)], axis=1))
    kv2 = _dot(ckvn_scr[...], jnp.concatenate([wkv_ref[j] for j in range(hp)], axis=1))
    for j in range(hp):
        qe = qe2[:, j * wide:(j + 1) * wide]
        qr = _rope_mix(qe[:, QK_NOPE:], cs)
        q_ref[0, j, :, :QK_NOPE] = (qe[:, :QK_NOPE] * scale).astype(BF16)
        q_ref[0, j, :, QK_NOPE:] = (qr[:, :QK_ROPE] * scale).astype(BF16)
        kv = kv2[:, j * wide:(j + 1) * wide]
        k_ref[0, j, :, :QK_NOPE] = kv[:, :QK_NOPE].astype(BF16)
        k_ref[0, j, :, QK_NOPE:] = kro_scr[:, :QK_ROPE]
        v_ref[0, j] = kv[:, QK_NOPE:].astype(BF16)


def _qkv(z, kr, cs, gq, gkv, wq_ext, wkv, batch, seq):
    n = z.shape[0]
    tm = min(seq, 2048)
    nsb = seq // tm
    hp = 2
    return pl.pallas_call(
        _qkv_kernel,
        grid=(n // tm, N_HEADS // hp),
        in_specs=[
            pl.BlockSpec((tm, Q_LORA), lambda i, h: (i, 0)),
            pl.BlockSpec((tm, KV_LORA), lambda i, h: (i, 1)),
            pl.BlockSpec((tm, LANES), lambda i, h: (i, 0)),
            pl.BlockSpec((tm, LANES), lambda i, h: (i % nsb, 0)),
            pl.BlockSpec((1, Q_LORA), lambda i, h: (0, 0)),
            pl.BlockSpec((1, KV_LORA), lambda i, h: (0, 0)),
            pl.BlockSpec((hp, Q_LORA, 2 * LANES), lambda i, h: (h, 0, 0)),
            pl.BlockSpec((hp, KV_LORA, 2 * LANES), lambda i, h: (h, 0, 0)),
        ],
        out_specs=[
            pl.BlockSpec((1, hp, tm, QK_HEAD), lambda i, h: (i // nsb, h, i % nsb, 0)),
            pl.BlockSpec((1, hp, tm, QK_HEAD), lambda i, h: (i // nsb, h, i % nsb, 0)),
            pl.BlockSpec((1, hp, tm, V_HEAD), lambda i, h: (i // nsb, h, i % nsb, 0)),
        ],
        out_shape=[
            jax.ShapeDtypeStruct((batch, N_HEADS, seq, QK_HEAD), BF16),
            jax.ShapeDtypeStruct((batch, N_HEADS, seq, QK_HEAD), BF16),
            jax.ShapeDtypeStruct((batch, N_HEADS, seq, V_HEAD), BF16),
        ],
        scratch_shapes=[
            pltpu.VMEM((tm, Q_LORA), BF16),
            pltpu.VMEM((tm, KV_LORA), BF16),
            pltpu.VMEM((tm, LANES), BF16),
        ],
        compiler_params=_params(("parallel", "arbitrary")),
        name="qkv_rope",
    )(z, z, kr, cs, gq, gkv, wq_ext, wkv)


ATT_ROWS = 512
ATT_CHAINS = 8
ATT_BK_MAIN = 1024


def _attn_kernel(q_ref, k_ref, v_ref, o_ref, m_scr, acc_scr, *, rows, chains, bkm):
    qi = pl.program_id(2)
    m_scr[...] = jnp.full(m_scr.shape, -jnp.inf, F32)
    acc_scr[...] = jnp.zeros(acc_scr.shape, F32)

    def load_kv(j, bk):
        start = pl.multiple_of(j * bk, bk)
        kj = k_ref[0, 0, pl.ds(start, bk), :]
        vj = jnp.concatenate([v_ref[0, 0, pl.ds(start, bk), :], jnp.ones((bk, LANES), BF16)], axis=1)
        return kj, vj

    def block(c, kj, vj, masked):
        q = q_ref[0, 0, c * rows:(c + 1) * rows, :]
        s = lax.dot_general(q, kj, (((1,), (1,)), ((), ())), preferred_element_type=F32)
        if masked:
            rc = lax.broadcasted_iota(jnp.int32, s.shape, 0) // CHUNK
            cc = lax.broadcasted_iota(jnp.int32, s.shape, 1) // CHUNK
            s = jnp.where(cc <= rc, s, NEG_BIG)
        m_prev = m_scr[c]
        m_new = jnp.maximum(m_prev, jnp.max(s, axis=-1, keepdims=True))
        alpha = jnp.exp2(m_prev - m_new)
        p = jnp.exp2(s - jnp.tile(m_new, (1, s.shape[1] // LANES)))
        acc_scr[c] = jnp.tile(alpha, (1, 2)) * acc_scr[c] + _dot(p.astype(BF16), vj)
        m_scr[c] = m_new

    per_step = chains * rows // bkm
    group = 2 if per_step % 2 == 0 else 1

    def body(j, carry):
        for jj in range(group):
            kj, vj = load_kv(j * group + jj, bkm)
            for c in range(chains):
                block(c, kj, vj, False)
        return carry

    lax.fori_loop(0, qi * (per_step // group), body, 0)
    for jj in range(chains):
        kj, vj = load_kv(qi * chains + jj, rows)
        for c in range(jj, chains):
            block(c, kj, vj, c == jj)
    for c in range(chains):
        a = acc_scr[c]
        o_ref[0, c * rows:(c + 1) * rows, :] = (a[:, :V_HEAD] / a[:, V_HEAD:]).astype(o_ref.dtype)


def _attention(q, k, v):
    batch, _, seq, _ = q.shape
    rows = min(seq, ATT_ROWS)
    chains = min(ATT_CHAINS, seq // rows)
    bq = rows * chains
    bkm = min(ATT_BK_MAIN, bq)
    return pl.pallas_call(
        functools.partial(_attn_kernel, rows=rows, chains=chains, bkm=bkm),
        grid=(batch, N_HEADS, seq // bq),
        in_specs=[
            pl.BlockSpec((1, 1, bq, QK_HEAD), lambda b, h, i: (b, h, i, 0)),
            pl.BlockSpec((1, 1, seq, QK_HEAD), lambda b, h, i: (b, h, 0, 0)),
            pl.BlockSpec((1, 1, seq, V_HEAD), lambda b, h, i: (b, h, 0, 0)),
        ],
        out_specs=pl.BlockSpec((1, bq, V_HEAD), lambda b, h, i: (b, i, h)),
        out_shape=jax.ShapeDtypeStruct((batch, seq, N_HEADS * V_HEAD), BF16),
        scratch_shapes=[
            pltpu.VMEM((chains, rows, LANES), F32),
            pltpu.VMEM((chains, rows, 2 * V_HEAD), F32),
        ],
        compiler_params=_params(("parallel", "parallel", "arbitrary")),
        name="flash_attn",
    )(q, k, v)


def _cpow(lr, li, ldt, k):
    dt = jnp.exp(ldt)
    mag = jnp.exp(lr * dt * k)
    ang = li * dt * k
    return mag * jnp.cos(ang), mag * jnp.sin(ang)


def _ssm_wgen_kernel(lr_r, li_r, ldt_r, bre_ref, bim_ref, cre_ref, cim_ref,
                     w_ref, ws_ref, wx_ref, a16_ref):
    hp = lax.Precision.HIGHEST
    ns = SSM_TILE_STATE
    lr, li, ldt = lr_r[0], li_r[0], ldt_r[0]
    a_re, a_im = _cpow(lr, li, ldt, 1.0)
    den = lr * lr + li * li
    nr = a_re - 1.0
    ni = a_im
    cf_re = (nr * lr + ni * li) / den
    cf_im = (ni * lr - nr * li) / den
    bre, bim = bre_ref[0], bim_ref[0]
    cre, cim = cre_ref[0], cim_ref[0]
    w_ref[...] = jnp.zeros(w_ref.shape, w_ref.dtype)
    for k in range(SSM_CHUNK):
        p_re, p_im = _cpow(lr, li, ldt, float(k))
        m_re = cf_re * p_re - cf_im * p_im
        m_im = cf_re * p_im + cf_im * p_re
        bm_re = bre * m_re - bim * m_im
        bm_im = bre * m_im + bim * m_re
        bd = (jnp.dot(bm_re, cre, precision=hp, preferred_element_type=F32)
              - jnp.dot(bm_im, cim, precision=hp, preferred_element_type=F32)).astype(BF16)
        for j in range(SSM_CHUNK - k):
            i = j + k
            w_ref[0, j * LANES:(j + 1) * LANES, i * LANES:(i + 1) * LANES] = bd
        jj = SSM_CHUNK - 1 - k
        ws_ref[0, jj * LANES:(jj + 1) * LANES, :ns] = bm_re.astype(BF16)
        ws_ref[0, jj * LANES:(jj + 1) * LANES, ns:] = bm_im.astype(BF16)
    cre_t, cim_t = cre.T, cim.T
    for i in range(SSM_CHUNK):
        p_re, p_im = _cpow(lr, li, ldt, float(i + 1))
        wx_ref[0, :ns, i * LANES:(i + 1) * LANES] = (p_re * cre_t - p_im * cim_t).T.astype(BF16)
        wx_ref[0, ns:, i * LANES:(i + 1) * LANES] = (-(p_re * cim_t + p_im * cre_t)).T.astype(BF16)
    q_re, q_im = _cpow(lr, li, ldt, float(SSM_CHUNK))
    a16_ref[0, :, :ns] = q_re
    a16_ref[0, :, ns:] = q_im


def _ssm_wgen(lam_re, lam_im, log_dt, b_re, b_im, c_re, c_im):
    nt, tg, ns = SSM_TILES, SSM_TILE_GROUPS, SSM_TILE_STATE
    eye = jnp.eye(tg, dtype=F32)
    bbd = lambda b: jnp.einsum("cgph,gk->cghkp", b.reshape(nt, tg, SSM_STATE, SSM_GROUP), eye).reshape(nt, LANES, ns)
    cbd = lambda c: jnp.einsum("cghp,gk->cgpkh", c.reshape(nt, tg, SSM_GROUP, SSM_STATE), eye).reshape(nt, ns, LANES)
    ldt = jnp.repeat(log_dt, SSM_STATE)
    rows = [a.reshape(nt, 1, ns) for a in (lam_re, lam_im, ldt)]
    kc = SSM_CHUNK * LANES
    row_spec = pl.BlockSpec((1, 1, ns), lambda c: (c, 0, 0))
    return pl.pallas_call(
        _ssm_wgen_kernel,
        grid=(nt,),
        in_specs=[row_spec] * 3 + [
            pl.BlockSpec((1, LANES, ns), lambda c: (c, 0, 0)),
            pl.BlockSpec((1, LANES, ns), lambda c: (c, 0, 0)),
            pl.BlockSpec((1, ns, LANES), lambda c: (c, 0, 0)),
            pl.BlockSpec((1, ns, LANES), lambda c: (c, 0, 0)),
        ],
        out_specs=[
            pl.BlockSpec((1, kc, kc), lambda c: (c, 0, 0)),
            pl.BlockSpec((1, kc, 2 * ns), lambda c: (c, 0, 0)),
            pl.BlockSpec((1, 2 * ns, kc), lambda c: (c, 0, 0)),
            pl.BlockSpec((1, 1, 2 * ns), lambda c: (c, 0, 0)),
        ],
        out_shape=[
            jax.ShapeDtypeStruct((nt, kc, kc), BF16),
            jax.ShapeDtypeStruct((nt, kc, 2 * ns), BF16),
            jax.ShapeDtypeStruct((nt, 2 * ns, kc), BF16),
            jax.ShapeDtypeStruct((nt, 1, 2 * ns), F32),
        ],
        compiler_params=_params(("parallel",)),
        name="ssm_wgen",
    )(*rows, bbd(b_re), bbd(b_im), cbd(c_re), cbd(c_im))


def _chunk_rows(u_ref, tb):
    return [u_ref[pl.ds(j, tb, stride=SSM_CHUNK), :] for j in range(SSM_CHUNK)]


def _ssm_sum_kernel(u_ref, ws_ref, s_ref, *, tb):
    u16 = jnp.concatenate(_chunk_rows(u_ref, tb), axis=1).astype(BF16)
    s_ref[0] = _dot(u16, ws_ref[0])


def _ssm_scan_kernel(s_ref, a16_ref, x_ref):
    ns = SSM_TILE_STATE
    a_re = a16_ref[0, :, :ns]
    a_im = a16_ref[0, :, ns:]
    nrow = s_ref.shape[1]

    def body(r, carry):
        xr, xi = carry
        x_ref[0, pl.ds(r, 1), :ns] = xr
        x_ref[0, pl.ds(r, 1), ns:] = xi
        sr = s_ref[0, pl.ds(r, 1), :ns]
        si = s_ref[0, pl.ds(r, 1), ns:]
        return a_re * xr - a_im * xi + sr, a_re * xi + a_im * xr + si

    zero = jnp.zeros((1, ns), F32)
    lax.fori_loop(0, nrow, body, (zero, zero), unroll=8)


def _ssm_out_kernel(u_ref, x_ref, w_ref, wx_ref, d_ref, g_ref, *, tb):
    us = _chunk_rows(u_ref, tb)
    u16 = jnp.concatenate(us, axis=1).astype(BF16)
    y16 = _dot(u16, w_ref[0]) + _dot(x_ref[0].astype(BF16), wx_ref[0])
    d = d_ref[...]
    for i in range(SSM_CHUNK):
        y = y16[:, i * LANES:(i + 1) * LANES] + d * us[i]
        g_ref[pl.ds(i, tb, stride=SSM_CHUNK), :] = _gelu(y)


def _ssm_branch(u, batch, seq, w, ws, wx, a16, d_skip):
    n = u.shape[0]
    nt, ns = SSM_TILES, SSM_TILE_STATE
    kc = SSM_CHUNK * LANES
    nb = n // SSM_CHUNK
    nbs = seq // SSM_CHUNK
    tb = min(nbs, 512)
    tok = tb * SSM_CHUNK
    s = pl.pallas_call(
        functools.partial(_ssm_sum_kernel, tb=tb),
        grid=(nt, nb // tb),
        in_specs=[
            pl.BlockSpec((tok, LANES), lambda c, r: (r, c)),
            pl.BlockSpec((1, kc, 2 * ns), lambda c, r: (c, 0, 0)),
        ],
        out_specs=pl.BlockSpec((1, tb, 2 * ns), lambda c, r: (c, r, 0)),
        out_shape=jax.ShapeDtypeStruct((nt, nb, 2 * ns), F32),
        compiler_params=_params(("parallel", "parallel")),
        name="ssm_chunk_sum",
    )(u, ws)
    x = pl.pallas_call(
        _ssm_scan_kernel,
        grid=(nt, batch),
        in_specs=[
            pl.BlockSpec((1, nbs, 2 * ns), lambda c, b: (c, b, 0)),
            pl.BlockSpec((1, 1, 2 * ns), lambda c, b: (c, 0, 0)),
        ],
        out_specs=pl.BlockSpec((1, nbs, 2 * ns), lambda c, b: (c, b, 0)),
        out_shape=jax.ShapeDtypeStruct((nt, nb, 2 * ns), F32),
        compiler_params=_params(("parallel", "parallel")),
        name="ssm_state_scan",
    )(s, a16)
    return pl.pallas_call(
        functools.partial(_ssm_out_kernel, tb=tb),
        grid=(nt, nb // tb),
        in_specs=[
            pl.BlockSpec((tok, LANES), lambda c, r: (r, c)),
            pl.BlockSpec((1, tb, 2 * ns), lambda c, r: (c, r, 0)),
            pl.BlockSpec((1, kc, kc), lambda c, r: (c, 0, 0)),
            pl.BlockSpec((1, 2 * ns, kc), lambda c, r: (c, 0, 0)),
            pl.BlockSpec((1, LANES), lambda c, r: (0, c)),
        ],
        out_specs=pl.BlockSpec((tok, LANES), lambda c, r: (r, c)),
        out_shape=jax.ShapeDtypeStruct((n, SSM_WIDTH), F32),
        compiler_params=_params(("parallel", "parallel")),
        name="ssm_chunk_out",
    )(u, x, w, wx, d_skip)


def _merge_kernel(attn_ref, g_ref, ga_ref, gs_ref, wglu_ref, wpa_ref, wps_ref, o_ref, ssm_scr):
    @pl.when(pl.program_id(1) == 0)
    def _():
        g = g_ref[...]
        ssm_scr[...] = (g * _sigmoid(_dot(g.astype(BF16), wglu_ref[...]))).astype(BF16)

    pa = _dot(attn_ref[...], wpa_ref[...])
    ps = _dot(ssm_scr[...], wps_ref[...])
    o_ref[...] = (_sigmoid(ga_ref[...].astype(F32)) * pa + _sigmoid(gs_ref[...].astype(F32)) * ps).astype(o_ref.dtype)


def _merge(attn, g, z, w_glu, w_pa, w_ps):
    n = attn.shape[0]
    tm = min(n, 1024)
    tn = 1024
    nj = D_MODEL // tn
    return pl.pallas_call(
        _merge_kernel,
        grid=(n // tm, nj),
        in_specs=[
            pl.BlockSpec((tm, N_HEADS * V_HEAD), lambda i, j: (i, 0)),
            pl.BlockSpec((tm, SSM_WIDTH), lambda i, j: (i, 0)),
            pl.BlockSpec((tm, tn), lambda i, j: (i, 1 + j)),
            pl.BlockSpec((tm, tn), lambda i, j: (i, 1 + nj + j)),
            pl.BlockSpec((SSM_WIDTH, SSM_WIDTH), lambda i, j: (0, 0)),
            pl.BlockSpec((N_HEADS * V_HEAD, tn), lambda i, j: (0, j)),
            pl.BlockSpec((SSM_WIDTH, tn), lambda i, j: (0, j)),
        ],
        out_specs=pl.BlockSpec((tm, tn), lambda i, j: (i, j)),
        out_shape=jax.ShapeDtypeStruct((n, D_MODEL), BF16),
        scratch_shapes=[pltpu.VMEM((tm, SSM_WIDTH), BF16)],
        compiler_params=_params(("parallel", "arbitrary")),
        name="glu_merge",
    )(attn, g, z, z, w_glu, w_pa, w_ps)


def _outproj_kernel(m_ref, x_ref, g1_ref, b1_ref, g2_ref, b2_ref, w_ref, h2_ref, h2t_ref):
    y = _dot(m_ref[...], w_ref[...])
    h = _ln(x_ref[...], g1_ref[...], b1_ref[...])
    h2 = _ln(DEEPNORM_ALPHA * h + y, g2_ref[...], b2_ref[...])
    h2_ref[...] = h2.astype(h2_ref.dtype)
    h2t_ref[...] = h2.T.astype(h2t_ref.dtype)


def _outproj(merged, x2, g1, b1, g2, b2, w_out):
    n = merged.shape[0]
    tm = min(n, 512)
    vec = pl.BlockSpec((1, D_MODEL), lambda i: (0, 0))
    return pl.pallas_call(
        _outproj_kernel,
        grid=(n // tm,),
        in_specs=[
            pl.BlockSpec((tm, D_MODEL), lambda i: (i, 0)),
            pl.BlockSpec((tm, D_MODEL), lambda i: (i, 0)),
            vec, vec, vec, vec,
            pl.BlockSpec((D_MODEL, D_MODEL), lambda i: (0, 0)),
        ],
        out_specs=[
            pl.BlockSpec((tm, D_MODEL), lambda i: (i, 0)),
            pl.BlockSpec((D_MODEL, tm), lambda i: (0, i)),
        ],
        out_shape=[
            jax.ShapeDtypeStruct((n, D_MODEL), BF16),
            jax.ShapeDtypeStruct((D_MODEL, n), BF16),
        ],
        compiler_params=_params(("parallel",)),
        name="outproj_ln",
    )(merged, x2, g1, b1, g2, b2, w_out)


def _peer_wcomb_kernel(keys_ref, wpq_ref, o_ref):
    o_ref[...] = lax.dot_general(keys_ref[0].astype(BF16), wpq_ref[...], (((1,), (1,)), ((), ())),
                                 preferred_element_type=F32).astype(o_ref.dtype)


def _peer_wcomb(sub_keys, w_pq):
    nhc = PEER_HEADS * 2
    keys = sub_keys.reshape(nhc, N_KEYS, PEER_KEY_HALF)
    return pl.pallas_call(
        _peer_wcomb_kernel,
        grid=(nhc,),
        in_specs=[
            pl.BlockSpec((1, N_KEYS, PEER_KEY_HALF), lambda i: (i, 0, 0)),
            pl.BlockSpec((D_MODEL, PEER_KEY_HALF), lambda i: (0, i)),
        ],
        out_specs=pl.BlockSpec((N_KEYS, D_MODEL), lambda i: (i, 0)),
        out_shape=jax.ShapeDtypeStruct((nhc * N_KEYS, D_MODEL), BF16),
        compiler_params=_params(("parallel",)),
        name="peer_wcomb",
    )(keys, w_pq)


PEER_TOK = 1024
PEER_THRESH_TOK = 2048


def _peer_scores_kernel(w_ref, xt_ref, s3_ref):
    rows = w_ref.shape[0] // 2
    for g in range(2):
        st = _dot(w_ref[g * rows:(g + 1) * rows, :], xt_ref[...])
        for c in range(s3_ref.shape[1]):
            s3_ref[g * rows:(g + 1) * rows, c, :] = st[:, c * LANES:(c + 1) * LANES]


def _peer_scores(wcomb, h2t):
    n = h2t.shape[1]
    tl = min(n, PEER_TOK)
    rows = wcomb.shape[0]
    return pl.pallas_call(
        _peer_scores_kernel,
        grid=(n // tl,),
        in_specs=[
            pl.BlockSpec((rows, D_MODEL), lambda i: (0, 0)),
            pl.BlockSpec((D_MODEL, tl), lambda i: (0, i)),
        ],
        out_specs=pl.BlockSpec((rows, tl // LANES, LANES), lambda i: (0, i, 0)),
        out_shape=jax.ShapeDtypeStruct((rows, n // LANES, LANES), F32),
        compiler_params=_params(("parallel",)),
        name="peer_scores",
    )(wcomb, h2t)


PEER_LIST = PEER_TOPK + 1
PEER_EXPERT_BLOCK = 16 * N_KEYS
PEER_SUB = 8 * N_KEYS


PEER_ROW = 4


def _bitonic_merge(x, desc):
    n = len(x)
    if n == 1:
        return x
    h = n // 2
    big = [jnp.maximum(x[i], x[i + h]) for i in range(h)]
    small = [jnp.minimum(x[i], x[i + h]) for i in range(h)]
    first, second = (big, small) if desc else (small, big)
    return _bitonic_merge(first, desc) + _bitonic_merge(second, desc)


def _bitonic_sort(x, desc):
    n = len(x)
    if n == 1:
        return x
    h = n // 2
    return _bitonic_merge(_bitonic_sort(x[:h], True) + _bitonic_sort(x[h:], False), desc)


def _merge_top(x, y):
    n = len(x)
    big = [jnp.maximum(x[i], y[n - 1 - i]) for i in range(n)]
    rest = functools.reduce(jnp.maximum, [jnp.minimum(x[i], y[n - 1 - i]) for i in range(n)])
    return _bitonic_merge(big, True), rest


def _count_above(vals, x, strict):
    above = (lambda v: v > x) if strict else (lambda v: v >= x)
    pick = lambda m, hi, lo: jnp.where(m, hi, lo)
    m8 = above(vals[7])
    m4 = above(pick(m8, vals[11], vals[3]))
    m2 = above(pick(m8, pick(m4, vals[13], vals[9]), pick(m4, vals[5], vals[1])))
    hi = pick(m4, pick(m2, vals[14], vals[12]), pick(m2, vals[10], vals[8]))
    lo = pick(m4, pick(m2, vals[6], vals[4]), pick(m2, vals[2], vals[0]))
    m1 = above(pick(m8, hi, lo))
    cnt = (pick(m8, 8.0, 0.0) + pick(m4, 4.0, 0.0)) + (pick(m2, 2.0, 0.0) + pick(m1, 1.0, 0.0))
    return pick(above(vals[15]), 16.0, cnt)


def _peer_thresh_kernel(s_ref, c_ref, e1_ref, q_ref, e2_ref, q_scr, e2_scr):
    shape = s_ref.shape[1:]
    neg = jnp.full(shape, -jnp.inf, F32)

    def top_list(load, count):
        def tree(lo, hi):
            if hi - lo == PEER_TOPK:
                return _bitonic_sort([load(n) for n in range(lo, hi)], True), neg
            mid = (lo + hi) // 2
            (x, dx), (y, dy) = tree(lo, mid), tree(mid, hi)
            top, rest = _merge_top(x, y)
            return top, jnp.maximum(jnp.maximum(dx, dy), rest)

        top, nxt = tree(0, count)
        return top + [nxt]

    a = top_list(lambda n: s_ref[n], N_KEYS)
    b = top_list(lambda n: s_ref[N_KEYS + n], N_KEYS)
    sums = [a[i - 1] + b[j - 1] for i in range(1, PEER_LIST + 1) for j in range(1, PEER_LIST // i + 1)]
    sums = sums + [neg] * (4 * PEER_TOPK - len(sums))
    t = top_list(lambda n: sums[n], len(sums))
    tau = 0.5 * (t[PEER_TOPK - 1] + t[PEER_TOPK])
    top = a[0] + b[0]
    z = jnp.exp(t[0] - top)
    for r in range(1, PEER_TOPK):
        z = z + jnp.exp(t[r] - top)
    cv = b[0] + jnp.log(2.0 * z)
    a16 = a[:PEER_TOPK]

    def write(n, carry):
        s1 = s_ref[n]
        s2 = s_ref[N_KEYS + n]
        c = 1.0 + _count_above(a16, s1, True)
        e1 = jnp.exp(s1 - a16[0])
        for gq in range(shape[0] // PEER_ROW):
            c_ref[n, gq] = c[gq * PEER_ROW:(gq + 1) * PEER_ROW, :]
            e1_ref[n, gq] = e1[gq * PEER_ROW:(gq + 1) * PEER_ROW, :]
        row0 = pl.multiple_of(n * nsub, nsub)
        q_scr[pl.ds(row0, nsub), :] = _count_above(a16, tau - s2, False)
        e2_scr[pl.ds(row0, nsub), :] = jnp.exp(s2 - cv)
        return carry

    nsub = shape[0]
    lax.fori_loop(0, N_KEYS, write, 0, unroll=2)
    for c in range(nsub):
        q_ref[:, c * LANES:(c + 1) * LANES] = q_scr[pl.ds(c, N_KEYS, stride=nsub), :].astype(q_ref.dtype)
        e2_ref[:, c * LANES:(c + 1) * LANES] = e2_scr[pl.ds(c, N_KEYS, stride=nsub), :].astype(e2_ref.dtype)


def _peer_thresh(s3):
    rows, nl, _ = s3.shape
    n = nl * LANES
    nb = min(PEER_THRESH_TOK // LANES, nl)
    half = rows // 2
    ng = nb // PEER_ROW
    row_spec = pl.BlockSpec((N_KEYS, ng, PEER_ROW, LANES), lambda i, h: (h, i, 0, 0))
    row_shape = jax.ShapeDtypeStruct((half, nl // PEER_ROW, PEER_ROW, LANES), F32)
    tile_spec = pl.BlockSpec((N_KEYS, nb * LANES), lambda i, h: (h, i))
    tile_shape = jax.ShapeDtypeStruct((half, n), BF16)
    c, e1, q, e2 = pl.pallas_call(
        _peer_thresh_kernel,
        grid=(nl // nb, PEER_HEADS),
        in_specs=[pl.BlockSpec((2 * N_KEYS, nb, LANES), lambda i, h: (h, i, 0))],
        out_specs=[row_spec, row_spec, tile_spec, tile_spec],
        out_shape=[row_shape, row_shape, tile_shape, tile_shape],
        scratch_shapes=[pltpu.VMEM((N_KEYS * nb, LANES), F32), pltpu.VMEM((N_KEYS * nb, LANES), F32)],
        compiler_params=_params(("parallel", "parallel")),
        name="peer_thresh",
    )(s3)
    shape5 = (PEER_HEADS, N_KEYS, nl // PEER_ROW, PEER_ROW, LANES)
    tiles = lambda v: v.reshape(PEER_HEADS, N_KEYS, n)
    return c.reshape(shape5), e1.reshape(shape5), tiles(q), tiles(e2)


def _peer_dense_kernel(xt_ref, u_ref, vt_ref, q_ref, e2_ref, c_ref, e1_ref, o_ref, acc_scr, a_scr, *, sub):
    e = pl.program_id(1)
    eb = u_ref.shape[0]

    @pl.when(e == 0)
    def _():
        acc_scr[...] = jnp.zeros(acc_scr.shape, F32)

    def row(ref, h, il):
        return jnp.concatenate([ref[h, il, 0, c:c + 1, :] for c in range(ref.shape[3])], axis=1).astype(BF16)

    zero = jnp.zeros((), BF16)

    xt = xt_ref[...]
    nsub = eb // sub
    score = lambda s: _dot(u_ref[s * sub:(s + 1) * sub, :], xt)
    st_next = score(0)
    for s in range(nsub):
        st, st_next = st_next, (score(s + 1) if s + 1 < nsub else None)
        for ii in range(sub // N_KEYS):
            il = s * (sub // N_KEYS) + ii
            gate = None
            for h in range(PEER_HEADS):
                term = row(e1_ref, h, il) * jnp.where(q_ref[h] >= row(c_ref, h, il), e2_ref[h], zero)
                gate = term if gate is None else gate + term
            a = _gelu_x2(st[ii * N_KEYS:(ii + 1) * N_KEYS, :]).astype(BF16) * gate
            a_scr[s * sub + ii * N_KEYS:s * sub + (ii + 1) * N_KEYS, :] = a
        acc_scr[...] += _dot(vt_ref[:, s * sub:(s + 1) * sub], a_scr[s * sub:(s + 1) * sub, :])

    @pl.when(e == pl.num_programs(1) - 1)
    def _():
        o_ref[...] = acc_scr[...].astype(o_ref.dtype)


def _peer_dense(h2t, q, e2, c, e1, u_tab, vt_tab):
    n = h2t.shape[1]
    t = min(n, PEER_ROW * LANES)
    eb = PEER_EXPERT_BLOCK
    row_spec = pl.BlockSpec((PEER_HEADS, eb // N_KEYS, 1, PEER_ROW, LANES), lambda i, e: (0, e, i, 0, 0))
    return pl.pallas_call(
        functools.partial(_peer_dense_kernel, sub=PEER_SUB),
        grid=(n // t, N_EXPERTS // eb),
        in_specs=[
            pl.BlockSpec((D_MODEL, t), lambda i, e: (0, i)),
            pl.BlockSpec((eb, D_MODEL), lambda i, e: (e, 0)),
            pl.BlockSpec((D_MODEL, eb), lambda i, e: (0, e)),
            pl.BlockSpec((PEER_HEADS, N_KEYS, t), lambda i, e: (0, 0, i)),
            pl.BlockSpec((PEER_HEADS, N_KEYS, t), lambda i, e: (0, 0, i)),
            row_spec, row_spec,
        ],
        out_specs=pl.BlockSpec((D_MODEL, t), lambda i, e: (0, i)),
        out_shape=jax.ShapeDtypeStruct((D_MODEL, n), BF16),
        scratch_shapes=[
            pltpu.VMEM((D_MODEL, t), F32),
            pltpu.VMEM((eb, t), BF16),
        ],
        compiler_params=_params(("parallel", "arbitrary")),
        name="peer_dense",
    )(h2t, u_tab, vt_tab, q, e2, c, e1)


def _final_ln_kernel(yt_ref, h2_ref, g_ref, b_ref, o_ref):
    y = yt_ref[...].astype(F32).T
    o_ref[...] = _ln(DEEPNORM_ALPHA * h2_ref[...].astype(F32) + y, g_ref[...], b_ref[...])


def _final_ln(yt, h2, ln_g, ln_b):
    n = h2.shape[0]
    tm = min(n, 512)
    vec = pl.BlockSpec((1, D_MODEL), lambda i: (0, 0))
    return pl.pallas_call(
        _final_ln_kernel,
        grid=(n // tm,),
        in_specs=[
            pl.BlockSpec((D_MODEL, tm), lambda i: (0, i)),
            pl.BlockSpec((tm, D_MODEL), lambda i: (i, 0)),
            vec, vec,
        ],
        out_specs=pl.BlockSpec((tm, D_MODEL), lambda i: (i, 0)),
        out_shape=jax.ShapeDtypeStruct((n, D_MODEL), F32),
        compiler_params=_params(("parallel",)),
        name="final_ln",
    )(yt, h2, ln_g, ln_b)


def _rot_half_cols(w):
    half = QK_ROPE // 2
    return jnp.concatenate([-w[..., half:], w[..., :half]], axis=-1)


def kernel(x, ln_in_g, ln_in_b, w_in, q_norm_g, w_uq, kv_norm_g, w_ukv, ssm_lam_re, ssm_lam_im, ssm_log_dt,
           ssm_b_re, ssm_b_im, ssm_c_re, ssm_c_im, ssm_d, w_glu, w_proj_attn, w_proj_ssm, w_out, ln_mix_g, ln_mix_b,
           w_peer_q, peer_sub_keys, peer_u, peer_v, ln_ffn_g, ln_ffn_b):
    batch, seq, _ = x.shape
    n = batch * seq
    x2 = x.reshape(n, D_MODEL)
    row = lambda v: v.reshape(1, -1).astype(F32)
    l = 0

    wi = w_in[l]
    w_main = jnp.concatenate([wi[:, OFF_SSM:OFF_KR], wi[:, OFF_GA:IN_WIDTH]], axis=1).astype(BF16)
    w_krc = wi[:, OFF_KR:OFF_GA]
    w_kr = jnp.concatenate([w_krc, _rot_half_cols(w_krc)], axis=1).astype(BF16)
    wq = w_uq[l].reshape(Q_LORA, N_HEADS, QK_HEAD)
    wq_ext = jnp.concatenate([wq, _rot_half_cols(wq[..., QK_NOPE:])], axis=-1)
    wq_ext = wq_ext.transpose(1, 0, 2).astype(BF16)
    wkv = w_ukv[l].reshape(KV_LORA, N_HEADS, QK_NOPE + V_HEAD).transpose(1, 0, 2).astype(BF16)

    cs = _rope_table(seq)
    u, z, kr = _inproj(x2, row(ln_in_g), row(ln_in_b), w_main, w_kr)

    q, k, v = _qkv(z, kr, cs, row(q_norm_g[l]), row(kv_norm_g[l]), wq_ext, wkv, batch, seq)
    attn = _attention(q, k, v).reshape(n, N_HEADS * V_HEAD)

    w, ws, wx, a16 = _ssm_wgen(ssm_lam_re[l], ssm_lam_im[l], ssm_log_dt[l], ssm_b_re[l], ssm_b_im[l],
                               ssm_c_re[l], ssm_c_im[l])
    g = _ssm_branch(u, batch, seq, w, ws, wx, a16, row(ssm_d[l]))

    merged = _merge(attn, g, z, w_glu[l].astype(BF16), w_proj_attn[l].astype(BF16), w_proj_ssm[l].astype(BF16))
    h2, h2t = _outproj(merged, x2, row(ln_in_g), row(ln_in_b), row(ln_mix_g[l]), row(ln_mix_b[l]),
                       w_out[l].astype(BF16))

    wcomb = _peer_wcomb(peer_sub_keys[l], w_peer_q[l].astype(BF16))
    s3 = _peer_scores(wcomb, h2t)
    c, e1, q, e2 = _peer_thresh(s3)
    yt = _peer_dense(h2t, q, e2, c, e1, peer_u[l].astype(BF16), peer_v[l].T.astype(BF16))
    out = _final_ln(yt, h2, row(ln_ffn_g[l]), row(ln_ffn_b[l]))
    return out.reshape(batch, seq, D_MODEL)
```

```python
import functools
import math

import jax
import jax.numpy as jnp
from jax import lax
from jax.experimental import pallas as pl
from jax.experimental.pallas import tpu as pltpu

F32 = jnp.float32
BF16 = jnp.bfloat16

D_MODEL = 2048
DEPTH = 1
CHUNK = 64
SSM_WIDTH = 1024
SSM_GROUP = 16
SSM_GROUPS = SSM_WIDTH // SSM_GROUP
SSM_STATE = 64
N_HEADS = 16
Q_LORA = 512
KV_LORA = 512
QK_NOPE = 128
QK_ROPE = 64
V_HEAD = 128
QK_HEAD = QK_NOPE + QK_ROPE
ROPE_THETA = 10000.0
PEER_HEADS = 8
N_KEYS = 128
N_EXPERTS = N_KEYS * N_KEYS
PEER_KEY_HALF = 128
PEER_TOPK = 16
LN_EPS = 1e-5
RMS_EPS = 1e-6
DEEPNORM_ALPHA = (2.0 * DEPTH) ** 0.25
OFF_SSM = 0
OFF_Q = OFF_SSM + SSM_WIDTH
OFF_KV = OFF_Q + Q_LORA
OFF_KR = OFF_KV + KV_LORA
OFF_GA = OFF_KR + QK_ROPE
OFF_GS = OFF_GA + D_MODEL
IN_WIDTH = OFF_GS + D_MODEL

LANES = 128
VMEM_LIMIT_BYTES = 56 * 1024 * 1024

SSM_CHUNK = 16
SSM_TILE_GROUPS = LANES // SSM_GROUP
SSM_TILES = SSM_WIDTH // LANES
SSM_TILE_STATE = SSM_TILE_GROUPS * SSM_STATE
NEG_BIG = -1e30


def _params(semantics):
    return pltpu.CompilerParams(dimension_semantics=semantics, vmem_limit_bytes=VMEM_LIMIT_BYTES)


def _dot(a, b):
    return jnp.dot(a, b, preferred_element_type=F32)


def _ln(x, g, b):
    mu = jnp.mean(x, axis=-1, keepdims=True)
    xc = x - mu
    var = jnp.mean(xc * xc, axis=-1, keepdims=True)
    return xc * lax.rsqrt(var + LN_EPS) * g + b


GELU_C = 0.7978845608028654
GELU_A = 0.044715


def _gelu(x):
    return 0.5 * x * (1.0 + jnp.tanh(GELU_C * (x + GELU_A * (x * x * x))))


def _gelu_x2(x):
    return x * (1.0 + jnp.tanh(x * (GELU_C + (GELU_C * GELU_A) * (x * x))))


def _sigmoid(x):
    return 1.0 / (1.0 + jnp.exp(-x))


def _rope_table_kernel(invf_ref, o_ref):
    ts = o_ref.shape[0]
    pos = (lax.broadcasted_iota(jnp.int32, (ts, LANES), 0) + pl.program_id(0) * ts).astype(F32)
    ang = pos * invf_ref[...]
    lane = lax.broadcasted_iota(jnp.int32, (ts, LANES), 1)
    o_ref[...] = jnp.where(lane < 2 * (QK_ROPE // 2), jnp.cos(ang), jnp.sin(ang))


def _rope_table(seq):
    half = QK_ROPE // 2
    invf = 1.0 / (ROPE_THETA ** (jnp.arange(half, dtype=F32) / half))
    invf4 = jnp.tile(invf, 4)[None, :]
    ts = min(seq, 1024)
    return pl.pallas_call(
        _rope_table_kernel,
        grid=(seq // ts,),
        in_specs=[pl.BlockSpec((1, LANES), lambda i: (0, 0))],
        out_specs=pl.BlockSpec((ts, LANES), lambda i: (i, 0)),
        out_shape=jax.ShapeDtypeStruct((seq, LANES), F32),
        compiler_params=_params(("parallel",)),
        name="rope_table",
    )(invf4)


def _inproj_kernel(x_ref, g_ref, b_ref, w_ref, wkr_ref, u_ref, z_ref, kr_ref, h_scr):
    j = pl.program_id(1)

    @pl.when(j == 0)
    def _():
        h = _ln(x_ref[...], g_ref[...], b_ref[...]).astype(BF16)
        h_scr[...] = h
        u_ref[...] = _dot(h, w_ref[...])
        kr_ref[...] = _dot(h, wkr_ref[...]).astype(kr_ref.dtype)

    @pl.when(j > 0)
    def _():
        z_ref[...] = _dot(h_scr[...], w_ref[...]).astype(z_ref.dtype)


def _inproj(x2, ln_g, ln_b, w_main, w_kr):
    n = x2.shape[0]
    tm = min(n, 1024)
    tn = 1024
    nj = w_main.shape[1] // tn
    return pl.pallas_call(
        _inproj_kernel,
        grid=(n // tm, nj),
        in_specs=[
            pl.BlockSpec((tm, D_MODEL), lambda i, j: (i, 0)),
            pl.BlockSpec((1, D_MODEL), lambda i, j: (0, 0)),
            pl.BlockSpec((1, D_MODEL), lambda i, j: (0, 0)),
            pl.BlockSpec((D_MODEL, tn), lambda i, j: (0, j)),
            pl.BlockSpec((D_MODEL, LANES), lambda i, j: (0, 0)),
        ],
        out_specs=[
            pl.BlockSpec((tm, tn), lambda i, j: (i, 0)),
            pl.BlockSpec((tm, tn), lambda i, j: (i, jnp.maximum(j - 1, 0))),
            pl.BlockSpec((tm, LANES), lambda i, j: (i, 0)),
        ],
        out_shape=[
            jax.ShapeDtypeStruct((n, tn), F32),
            jax.ShapeDtypeStruct((n, (nj - 1) * tn), BF16),
            jax.ShapeDtypeStruct((n, LANES), BF16),
        ],
        scratch_shapes=[pltpu.VMEM((tm, D_MODEL), BF16)],
        compiler_params=_params(("parallel", "arbitrary")),
        name="ln_inproj",
    )(x2, ln_g, ln_b, w_main, w_kr)


def _rope_mix(t2, cs):
    t = t2 * cs
    return t + pltpu.roll(t, QK_ROPE, 1)


def _qkv_kernel(cq_ref, ckv_ref, kr_ref, cs_ref, gq_ref, gkv_ref, wq_ref, wkv_ref,
                q_ref, k_ref, v_ref, cqn_scr, ckvn_scr, kro_scr):
    cs = cs_ref[...]

    @pl.when(pl.program_id(1) == 0)
    def _():
        def rms(c, g):
            c = c.astype(F32)
            return (c * lax.rsqrt(jnp.mean(c * c, axis=-1, keepdims=True) + RMS_EPS) * g).astype(BF16)

        cqn_scr[...] = rms(cq_ref[...], gq_ref[...])
        ckvn_scr[...] = rms(ckv_ref[...], gkv_ref[...])
        kro_scr[...] = _rope_mix(kr_ref[...].astype(F32), cs).astype(BF16)

    scale = math.log2(math.e) / math.sqrt(QK_HEAD)
    hp = wq_ref.shape[0]
    wide = 2 * LANES
    qe2 = _dot(cqn_scr[...], jnp.concatenate([wq_ref[j] for j in range(hp)], axis=1))
    kv2 = _dot(ckvn_scr[...], jnp.concatenate([wkv_ref[j] for j in range(hp)], axis=1))
    for j in range(hp):
        qe = qe2[:, j * wide:(j + 1) * wide]
        qr = _rope_mix(qe[:, QK_NOPE:], cs)
        q_ref[0, j, :, :QK_NOPE] = (qe[:, :QK_NOPE] * scale).astype(BF16)
        q_ref[0, j, :, QK_NOPE:] = (qr[:, :QK_ROPE] * scale).astype(BF16)
        kv = kv2[:, j * wide:(j + 1) * wide]
        k_ref[0, j, :, :QK_NOPE] = kv[:, :QK_NOPE].astype(BF16)
        k_ref[0, j, :, QK_NOPE:] = kro_scr[:, :QK_ROPE]
        v_ref[0, j] = kv[:, QK_NOPE:].astype(BF16)


def _qkv(z, kr, cs, gq, gkv, wq_ext, wkv, batch, seq):
    n = z.shape[0]
    tm = min(seq, 2048)
    nsb = seq // tm
    hp = 2
    return pl.pallas_call(
        _qkv_kernel,
        grid=(n // tm, N_HEADS // hp),
        in_specs=[
            pl.BlockSpec((tm, Q_LORA), lambda i, h: (i, 0)),
            pl.BlockSpec((tm, KV_LORA), lambda i, h: (i, 1)),
            pl.BlockSpec((tm, LANES), lambda i, h: (i, 0)),
            pl.BlockSpec((tm, LANES), lambda i, h: (i % nsb, 0)),
            pl.BlockSpec((1, Q_LORA), lambda i, h: (0, 0)),
            pl.BlockSpec((1, KV_LORA), lambda i, h: (0, 0)),
            pl.BlockSpec((hp, Q_LORA, 2 * LANES), lambda i, h: (h, 0, 0)),
            pl.BlockSpec((hp, KV_LORA, 2 * LANES), lambda i, h: (h, 0, 0)),
        ],
        out_specs=[
            pl.BlockSpec((1, hp, tm, QK_HEAD), lambda i, h: (i // nsb, h, i % nsb, 0)),
            pl.BlockSpec((1, hp, tm, QK_HEAD), lambda i, h: (i // nsb, h, i % nsb, 0)),
            pl.BlockSpec((1, hp, tm, V_HEAD), lambda i, h: (i // nsb, h, i % nsb, 0)),
        ],
        out_shape=[
            jax.ShapeDtypeStruct((batch, N_HEADS, seq, QK_HEAD), BF16),
            jax.ShapeDtypeStruct((batch, N_HEADS, seq, QK_HEAD), BF16),
            jax.ShapeDtypeStruct((batch, N_HEADS, seq, V_HEAD), BF16),
        ],
        scratch_shapes=[
            pltpu.VMEM((tm, Q_LORA), BF16),
            pltpu.VMEM((tm, KV_LORA), BF16),
            pltpu.VMEM((tm, LANES), BF16),
        ],
        compiler_params=_params(("parallel", "arbitrary")),
        name="qkv_rope",
    )(z, z, kr, cs, gq, gkv, wq_ext, wkv)


ATT_ROWS = 512
ATT_CHAINS = 8
ATT_BK_MAIN = 1024


def _attn_kernel(q_ref, k_ref, v_ref, o_ref, m_scr, acc_scr, *, rows, chains, bkm):
    qi = pl.program_id(2)
    m_scr[...] = jnp.full(m_scr.shape, -jnp.inf, F32)
    acc_scr[...] = jnp.zeros(acc_scr.shape, F32)

    def load_kv(j, bk):
        start = pl.multiple_of(j * bk, bk)
        kj = k_ref[0, 0, pl.ds(start, bk), :]
        vj = jnp.concatenate([v_ref[0, 0, pl.ds(start, bk), :], jnp.ones((bk, LANES), BF16)], axis=1)
        return kj, vj

    def block(c, kj, vj, masked):
        q = q_ref[0, 0, c * rows:(c + 1) * rows, :]
        s = lax.dot_general(q, kj, (((1,), (1,)), ((), ())), preferred_element_type=F32)
        if masked:
            rc = lax.broadcasted_iota(jnp.int32, s.shape, 0) // CHUNK
            cc = lax.broadcasted_iota(jnp.int32, s.shape, 1) // CHUNK
            s = jnp.where(cc <= rc, s, NEG_BIG)
        m_prev = m_scr[c]
        m_new = jnp.maximum(m_prev, jnp.max(s, axis=-1, keepdims=True))
        alpha = jnp.exp2(m_prev - m_new)
        p = jnp.exp2(s - jnp.tile(m_new, (1, s.shape[1] // LANES)))
        acc_scr[c] = jnp.tile(alpha, (1, 2)) * acc_scr[c] + _dot(p.astype(BF16), vj)
        m_scr[c] = m_new

    per_step = chains * rows // bkm
    group = 2 if per_step % 2 == 0 else 1

    def body(j, carry):
        for jj in range(group):
            kj, vj = load_kv(j * group + jj, bkm)
            for c in range(chains):
                block(c, kj, vj, False)
        return carry

    lax.fori_loop(0, qi * (per_step // group), body, 0)
    for jj in range(chains):
        kj, vj = load_kv(qi * chains + jj, rows)
        for c in range(jj, chains):
            block(c, kj, vj, c == jj)
    for c in range(chains):
        a = acc_scr[c]
        o_ref[0, c * rows:(c + 1) * rows, :] = (a[:, :V_HEAD] / a[:, V_HEAD:]).astype(o_ref.dtype)


def _attention(q, k, v):
    batch, _, seq, _ = q.shape
    rows = min(seq, ATT_ROWS)
    chains = min(ATT_CHAINS, seq // rows)
    bq = rows * chains
    bkm = min(ATT_BK_MAIN, bq)
    return pl.pallas_call(
        functools.partial(_attn_kernel, rows=rows, chains=chains, bkm=bkm),
        grid=(batch, N_HEADS, seq // bq),
        in_specs=[
            pl.BlockSpec((1, 1, bq, QK_HEAD), lambda b, h, i: (b, h, i, 0)),
            pl.BlockSpec((1, 1, seq, QK_HEAD), lambda b, h, i: (b, h, 0, 0)),
            pl.BlockSpec((1, 1, seq, V_HEAD), lambda b, h, i: (b, h, 0, 0)),
        ],
        out_specs=pl.BlockSpec((1, bq, V_HEAD), lambda b, h, i: (b, i, h)),
        out_shape=jax.ShapeDtypeStruct((batch, seq, N_HEADS * V_HEAD), BF16),
        scratch_shapes=[
            pltpu.VMEM((chains, rows, LANES), F32),
            pltpu.VMEM((chains, rows, 2 * V_HEAD), F32),
        ],
        compiler_params=_params(("parallel", "parallel", "arbitrary")),
        name="flash_attn",
    )(q, k, v)


def _cpow(lr, li, ldt, k):
    dt = jnp.exp(ldt)
    mag = jnp.exp(lr * dt * k)
    ang = li * dt * k
    return mag * jnp.cos(ang), mag * jnp.sin(ang)


def _ssm_wgen_kernel(lr_r, li_r, ldt_r, bre_ref, bim_ref, cre_ref, cim_ref,
                     w_ref, ws_ref, wx_ref, a16_ref):
    hp = lax.Precision.HIGHEST
    ns = SSM_TILE_STATE
    lr, li, ldt = lr_r[0], li_r[0], ldt_r[0]
    a_re, a_im = _cpow(lr, li, ldt, 1.0)
    den = lr * lr + li * li
    nr = a_re - 1.0
    ni = a_im
    cf_re = (nr * lr + ni * li) / den
    cf_im = (ni * lr - nr * li) / den
    bre, bim = bre_ref[0], bim_ref[0]
    cre, cim = cre_ref[0], cim_ref[0]
    w_ref[...] = jnp.zeros(w_ref.shape, w_ref.dtype)
    for k in range(SSM_CHUNK):
        p_re, p_im = _cpow(lr, li, ldt, float(k))
        m_re = cf_re * p_re - cf_im * p_im
        m_im = cf_re * p_im + cf_im * p_re
        bm_re = bre * m_re - bim * m_im
        bm_im = bre * m_im + bim * m_re
        bd = (jnp.dot(bm_re, cre, precision=hp, preferred_element_type=F32)
              - jnp.dot(bm_im, cim, precision=hp, preferred_element_type=F32)).astype(BF16)
        for j in range(SSM_CHUNK - k):
            i = j + k
            w_ref[0, j * LANES:(j + 1) * LANES, i * LANES:(i + 1) * LANES] = bd
        jj = SSM_CHUNK - 1 - k
        ws_ref[0, jj * LANES:(jj + 1) * LANES, :ns] = bm_re.astype(BF16)
        ws_ref[0, jj * LANES:(jj + 1) * LANES, ns:] = bm_im.astype(BF16)
    cre_t, cim_t = cre.T, cim.T
    for i in range(SSM_CHUNK):
        p_re, p_im = _cpow(lr, li, ldt, float(i + 1))
        wx_ref[0, :ns, i * LANES:(i + 1) * LANES] = (p_re * cre_t - p_im * cim_t).T.astype(BF16)
        wx_ref[0, ns:, i * LANES:(i + 1) * LANES] = (-(p_re * cim_t + p_im * cre_t)).T.astype(BF16)
    q_re, q_im = _cpow(lr, li, ldt, float(SSM_CHUNK))
    a16_ref[0, :, :ns] = q_re
    a16_ref[0, :, ns:] = q_im


def _ssm_wgen(lam_re, lam_im, log_dt, b_re, b_im, c_re, c_im):
    nt, tg, ns = SSM_TILES, SSM_TILE_GROUPS, SSM_TILE_STATE
    eye = jnp.eye(tg, dtype=F32)
    bbd = lambda b: jnp.einsum("cgph,gk->cghkp", b.reshape(nt, tg, SSM_STATE, SSM_GROUP), eye).reshape(nt, LANES, ns)
    cbd = lambda c: jnp.einsum("cghp,gk->cgpkh", c.reshape(nt, tg, SSM_GROUP, SSM_STATE), eye).reshape(nt, ns, LANES)
    ldt = jnp.repeat(log_dt, SSM_STATE)
    rows = [a.reshape(nt, 1, ns) for a in (lam_re, lam_im, ldt)]
    kc = SSM_CHUNK * LANES
    row_spec = pl.BlockSpec((1, 1, ns), lambda c: (c, 0, 0))
    return pl.pallas_call(
        _ssm_wgen_kernel,
        grid=(nt,),
        in_specs=[row_spec] * 3 + [
            pl.BlockSpec((1, LANES, ns), lambda c: (c, 0, 0)),
            pl.BlockSpec((1, LANES, ns), lambda c: (c, 0, 0)),
            pl.BlockSpec((1, ns, LANES), lambda c: (c, 0, 0)),
            pl.BlockSpec((1, ns, LANES), lambda c: (c, 0, 0)),
        ],
        out_specs=[
            pl.BlockSpec((1, kc, kc), lambda c: (c, 0, 0)),
            pl.BlockSpec((1, kc, 2 * ns), lambda c: (c, 0, 0)),
            pl.BlockSpec((1, 2 * ns, kc), lambda c: (c, 0, 0)),
            pl.BlockSpec((1, 1, 2 * ns), lambda c: (c, 0, 0)),
        ],
        out_shape=[
            jax.ShapeDtypeStruct((nt, kc, kc), BF16),
            jax.ShapeDtypeStruct((nt, kc, 2 * ns), BF16),
            jax.ShapeDtypeStruct((nt, 2 * ns, kc), BF16),
            jax.ShapeDtypeStruct((nt, 1, 2 * ns), F32),
        ],
        compiler_params=_params(("parallel",)),
        name="ssm_wgen",
    )(*rows, bbd(b_re), bbd(b_im), cbd(c_re), cbd(c_im))


def _chunk_rows(u_ref, tb):
    return [u_ref[pl.ds(j, tb, stride=SSM_CHUNK), :] for j in range(SSM_CHUNK)]


def _ssm_sum_kernel(u_ref, ws_ref, s_ref, *, tb):
    u16 = jnp.concatenate(_chunk_rows(u_ref, tb), axis=1).astype(BF16)
    s_ref[0] = _dot(u16, ws_ref[0])


def _ssm_scan_kernel(s_ref, a16_ref, x_ref):
    ns = SSM_TILE_STATE
    a_re = a16_ref[0, :, :ns]
    a_im = a16_ref[0, :, ns:]
    nrow = s_ref.shape[1]

    def body(r, carry):
        xr, xi = carry
        x_ref[0, pl.ds(r, 1), :ns] = xr
        x_ref[0, pl.ds(r, 1), ns:] = xi
        sr = s_ref[0, pl.ds(r, 1), :ns]
        si = s_ref[0, pl.ds(r, 1), ns:]
        return a_re * xr - a_im * xi + sr, a_re * xi + a_im * xr + si

    zero = jnp.zeros((1, ns), F32)
    lax.fori_loop(0, nrow, body, (zero, zero), unroll=8)


def _ssm_out_kernel(u_ref, x_ref, w_ref, wx_ref, d_ref, g_ref, *, tb):
    us = _chunk_rows(u_ref, tb)
    u16 = jnp.concatenate(us, axis=1).astype(BF16)
    y16 = _dot(u16, w_ref[0]) + _dot(x_ref[0].astype(BF16), wx_ref[0])
    d = d_ref[...]
    for i in range(SSM_CHUNK):
        y = y16[:, i * LANES:(i + 1) * LANES] + d * us[i]
        g_ref[pl.ds(i, tb, stride=SSM_CHUNK), :] = _gelu(y)


def _ssm_branch(u, batch, seq, w, ws, wx, a16, d_skip):
    n = u.shape[0]
    nt, ns = SSM_TILES, SSM_TILE_STATE
    kc = SSM_CHUNK * LANES
    nb = n // SSM_CHUNK
    nbs = seq // SSM_CHUNK
    tb = min(nbs, 512)
    tok = tb * SSM_CHUNK
    s = pl.pallas_call(
        functools.partial(_ssm_sum_kernel, tb=tb),
        grid=(nt, nb // tb),
        in_specs=[
            pl.BlockSpec((tok, LANES), lambda c, r: (r, c)),
            pl.BlockSpec((1, kc, 2 * ns), lambda c, r: (c, 0, 0)),
        ],
        out_specs=pl.BlockSpec((1, tb, 2 * ns), lambda c, r: (c, r, 0)),
        out_shape=jax.ShapeDtypeStruct((nt, nb, 2 * ns), F32),
        compiler_params=_params(("parallel", "parallel")),
        name="ssm_chunk_sum",
    )(u, ws)
    x = pl.pallas_call(
        _ssm_scan_kernel,
        grid=(nt, batch),
        in_specs=[
            pl.BlockSpec((1, nbs, 2 * ns), lambda c, b: (c, b, 0)),
            pl.BlockSpec((1, 1, 2 * ns), lambda c, b: (c, 0, 0)),
        ],
        out_specs=pl.BlockSpec((1, nbs, 2 * ns), lambda c, b: (c, b, 0)),
        out_shape=jax.ShapeDtypeStruct((nt, nb, 2 * ns), F32),
        compiler_params=_params(("parallel", "parallel")),
        name="ssm_state_scan",
    )(s, a16)
    return pl.pallas_call(
        functools.partial(_ssm_out_kernel, tb=tb),
        grid=(nt, nb // tb),
        in_specs=[
            pl.BlockSpec((tok, LANES), lambda c, r: (r, c)),
            pl.BlockSpec((1, tb, 2 * ns), lambda c, r: (c, r, 0)),
            pl.BlockSpec((1, kc, kc), lambda c, r: (c, 0, 0)),
            pl.BlockSpec((1, 2 * ns, kc), lambda c, r: (c, 0, 0)),
            pl.BlockSpec((1, LANES), lambda c, r: (0, c)),
        ],
        out_specs=pl.BlockSpec((tok, LANES), lambda c, r: (r, c)),
        out_shape=jax.ShapeDtypeStruct((n, SSM_WIDTH), F32),
        compiler_params=_params(("parallel", "parallel")),
        name="ssm_chunk_out",
    )(u, x, w, wx, d_skip)


def _merge_kernel(attn_ref, g_ref, ga_ref, gs_ref, wglu_ref, wpa_ref, wps_ref, o_ref, ssm_scr):
    @pl.when(pl.program_id(1) == 0)
    def _():
        g = g_ref[...]
        ssm_scr[...] = (g * _sigmoid(_dot(g.astype(BF16), wglu_ref[...]))).astype(BF16)

    pa = _dot(attn_ref[...], wpa_ref[...])
    ps = _dot(ssm_scr[...], wps_ref[...])
    o_ref[...] = (_sigmoid(ga_ref[...].astype(F32)) * pa + _sigmoid(gs_ref[...].astype(F32)) * ps).astype(o_ref.dtype)


def _merge(attn, g, z, w_glu, w_pa, w_ps):
    n = attn.shape[0]
    tm = min(n, 1024)
    tn = 1024
    nj = D_MODEL // tn
    return pl.pallas_call(
        _merge_kernel,
        grid=(n // tm, nj),
        in_specs=[
            pl.BlockSpec((tm, N_HEADS * V_HEAD), lambda i, j: (i, 0)),
            pl.BlockSpec((tm, SSM_WIDTH), lambda i, j: (i, 0)),
            pl.BlockSpec((tm, tn), lambda i, j: (i, 1 + j)),
            pl.BlockSpec((tm, tn), lambda i, j: (i, 1 + nj + j)),
            pl.BlockSpec((SSM_WIDTH, SSM_WIDTH), lambda i, j: (0, 0)),
            pl.BlockSpec((N_HEADS * V_HEAD, tn), lambda i, j: (0, j)),
            pl.BlockSpec((SSM_WIDTH, tn), lambda i, j: (0, j)),
        ],
        out_specs=pl.BlockSpec((tm, tn), lambda i, j: (i, j)),
        out_shape=jax.ShapeDtypeStruct((n, D_MODEL), BF16),
        scratch_shapes=[pltpu.VMEM((tm, SSM_WIDTH), BF16)],
        compiler_params=_params(("parallel", "arbitrary")),
        name="glu_merge",
    )(attn, g, z, z, w_glu, w_pa, w_ps)


def _outproj_kernel(m_ref, x_ref, g1_ref, b1_ref, g2_ref, b2_ref, w_ref, h2_ref, h2t_ref):
    y = _dot(m_ref[...], w_ref[...])
    h = _ln(x_ref[...], g1_ref[...], b1_ref[...])
    h2 = _ln(DEEPNORM_ALPHA * h + y, g2_ref[...], b2_ref[...])
    h2_ref[...] = h2.astype(h2_ref.dtype)
    h2t_ref[...] = h2.T.astype(h2t_ref.dtype)


def _outproj(merged, x2, g1, b1, g2, b2, w_out):
    n = merged.shape[0]
    tm = min(n, 512)
    vec = pl.BlockSpec((1, D_MODEL), lambda i: (0, 0))
    return pl.pallas_call(
        _outproj_kernel,
        grid=(n // tm,),
        in_specs=[
            pl.BlockSpec((tm, D_MODEL), lambda i: (i, 0)),
            pl.BlockSpec((tm, D_MODEL), lambda i: (i, 0)),
            vec, vec, vec, vec,
            pl.BlockSpec((D_MODEL, D_MODEL), lambda i: (0, 0)),
        ],
        out_specs=[
            pl.BlockSpec((tm, D_MODEL), lambda i: (i, 0)),
            pl.BlockSpec((D_MODEL, tm), lambda i: (0, i)),
        ],
        out_shape=[
            jax.ShapeDtypeStruct((n, D_MODEL), BF16),
            jax.ShapeDtypeStruct((D_MODEL, n), BF16),
        ],
        compiler_params=_params(("parallel",)),
        name="outproj_ln",
    )(merged, x2, g1, b1, g2, b2, w_out)


def _peer_wcomb_kernel(keys_ref, wpq_ref, o_ref):
    o_ref[...] = lax.dot_general(keys_ref[0].astype(BF16), wpq_ref[...], (((1,), (1,)), ((), ())),
                                 preferred_element_type=F32).astype(o_ref.dtype)


def _peer_wcomb(sub_keys, w_pq):
    nhc = PEER_HEADS * 2
    keys = sub_keys.reshape(nhc, N_KEYS, PEER_KEY_HALF)
    return pl.pallas_call(
        _peer_wcomb_kernel,
        grid=(nhc,),
        in_specs=[
            pl.BlockSpec((1, N_KEYS, PEER_KEY_HALF), lambda i: (i, 0, 0)),
            pl.BlockSpec((D_MODEL, PEER_KEY_HALF), lambda i: (0, i)),
        ],
        out_specs=pl.BlockSpec((N_KEYS, D_MODEL), lambda i: (i, 0)),
        out_shape=jax.ShapeDtypeStruct((nhc * N_KEYS, D_MODEL), BF16),
        compiler_params=_params(("parallel",)),
        name="peer_wcomb",
    )(keys, w_pq)


PEER_TOK = 1024
PEER_THRESH_TOK = 2048


def _peer_scores_kernel(w_ref, xt_ref, s3_ref):
    rows = w_ref.shape[0] // 2
    cpt = xt_ref.shape[1] // LANES
    for g in range(2):
        st = _dot(w_ref[g * rows:(g + 1) * rows, :], xt_ref[...])
        for c in range(cpt):
            s3_ref[0, pl.ds(g * rows * cpt + c, rows, stride=cpt), :] = st[:, c * LANES:(c + 1) * LANES]


def _peer_scores(wcomb, h2t):
    n = h2t.shape[1]
    tl = min(n, PEER_TOK)
    rows = wcomb.shape[0]
    cpt = tl // LANES
    return pl.pallas_call(
        _peer_scores_kernel,
        grid=(n // tl,),
        in_specs=[
            pl.BlockSpec((rows, D_MODEL), lambda i: (0, 0)),
            pl.BlockSpec((D_MODEL, tl), lambda i: (0, i)),
        ],
        out_specs=pl.BlockSpec((1, rows * cpt, LANES), lambda i: (i, 0, 0)),
        out_shape=jax.ShapeDtypeStruct((n // tl, rows * cpt, LANES), F32),
        compiler_params=_params(("parallel",)),
        name="peer_scores",
    )(wcomb, h2t)


PEER_LIST = PEER_TOPK + 1
PEER_EXPERT_BLOCK = 16 * N_KEYS
PEER_SUB = 8 * N_KEYS


PEER_ROW = 4


def _bitonic_merge(x, desc):
    n = len(x)
    if n == 1:
        return x
    h = n // 2
    big = [jnp.maximum(x[i], x[i + h]) for i in range(h)]
    small = [jnp.minimum(x[i], x[i + h]) for i in range(h)]
    first, second = (big, small) if desc else (small, big)
    return _bitonic_merge(first, desc) + _bitonic_merge(second, desc)


def _bitonic_sort(x, desc):
    n = len(x)
    if n == 1:
        return x
    h = n // 2
    return _bitonic_merge(_bitonic_sort(x[:h], True) + _bitonic_sort(x[h:], False), desc)


def _merge_top(x, y):
    n = len(x)
    big = [jnp.maximum(x[i], y[n - 1 - i]) for i in range(n)]
    rest = functools.reduce(jnp.maximum, [jnp.minimum(x[i], y[n - 1 - i]) for i in range(n)])
    return _bitonic_merge(big, True), rest


def _count_above(vals, x, strict):
    above = (lambda v: v > x) if strict else (lambda v: v >= x)
    pick = lambda m, hi, lo: jnp.where(m, hi, lo)
    m8 = above(vals[7])
    m4 = above(pick(m8, vals[11], vals[3]))
    m2 = above(pick(m8, pick(m4, vals[13], vals[9]), pick(m4, vals[5], vals[1])))
    hi = pick(m4, pick(m2, vals[14], vals[12]), pick(m2, vals[10], vals[8]))
    lo = pick(m4, pick(m2, vals[6], vals[4]), pick(m2, vals[2], vals[0]))
    m1 = above(pick(m8, hi, lo))
    cnt = (pick(m8, 8.0, 0.0) + pick(m4, 4.0, 0.0)) + (pick(m2, 2.0, 0.0) + pick(m1, 1.0, 0.0))
    return pick(above(vals[15]), 16.0, cnt)


def _peer_thresh_kernel(s_ref, c_ref, e1_ref, q_ref, e2_ref, q_scr, e2_scr):
    steps, cpt = s_ref.shape[0], s_ref.shape[1] // (2 * N_KEYS)
    shape = (steps * cpt, LANES)
    neg = jnp.full(shape, -jnp.inf, F32)

    def key(n):
        r0 = n * cpt if isinstance(n, int) else pl.multiple_of(n * cpt, cpt)
        return jnp.concatenate([s_ref[s, pl.ds(r0, cpt), :] for s in range(steps)], axis=0)

    def top_list(load, count):
        def tree(lo, hi):
            if hi - lo == PEER_TOPK:
                return _bitonic_sort([load(n) for n in range(lo, hi)], True), neg
            mid = (lo + hi) // 2
            (x, dx), (y, dy) = tree(lo, mid), tree(mid, hi)
            top, rest = _merge_top(x, y)
            return top, jnp.maximum(jnp.maximum(dx, dy), rest)

        top, nxt = tree(0, count)
        return top + [nxt]

    a = top_list(key, N_KEYS)
    b = top_list(lambda n: key(N_KEYS + n), N_KEYS)
    sums = [a[i - 1] + b[j - 1] for i in range(1, PEER_LIST + 1) for j in range(1, PEER_LIST // i + 1)]
    sums = sums + [neg] * (4 * PEER_TOPK - len(sums))
    t = top_list(lambda n: sums[n], len(sums))
    tau = 0.5 * (t[PEER_TOPK - 1] + t[PEER_TOPK])
    top = a[0] + b[0]
    z = jnp.exp(t[0] - top)
    for r in range(1, PEER_TOPK):
        z = z + jnp.exp(t[r] - top)
    cv = b[0] + jnp.log(2.0 * z)
    a16 = a[:PEER_TOPK]

    def write(n, carry):
        s1 = key(n)
        s2 = key(N_KEYS + n)
        c = 1.0 + _count_above(a16, s1, True)
        e1 = jnp.exp(s1 - a16[0])
        for gq in range(shape[0] // PEER_ROW):
            c_ref[n, gq] = c[gq * PEER_ROW:(gq + 1) * PEER_ROW, :]
            e1_ref[n, gq] = e1[gq * PEER_ROW:(gq + 1) * PEER_ROW, :]
        row0 = pl.multiple_of(n * nsub, nsub)
        q_scr[pl.ds(row0, nsub), :] = _count_above(a16, tau - s2, False)
        e2_scr[pl.ds(row0, nsub), :] = jnp.exp(s2 - cv)
        return carry

    nsub = shape[0]
    lax.fori_loop(0, N_KEYS, write, 0, unroll=2)
    for c in range(nsub):
        q_ref[:, c * LANES:(c + 1) * LANES] = q_scr[pl.ds(c, N_KEYS, stride=nsub), :].astype(q_ref.dtype)
        e2_ref[:, c * LANES:(c + 1) * LANES] = e2_scr[pl.ds(c, N_KEYS, stride=nsub), :].astype(e2_ref.dtype)


def _peer_thresh(s3):
    nsteps, rc, _ = s3.shape
    rows = 2 * N_KEYS * PEER_HEADS
    cpt = rc // rows
    nl = nsteps * cpt
    n = nl * LANES
    steps = min(max(PEER_THRESH_TOK // (cpt * LANES), 1), nsteps)
    nb = steps * cpt
    half = rows // 2
    ng = nb // PEER_ROW
    row_spec = pl.BlockSpec((N_KEYS, ng, PEER_ROW, LANES), lambda i, h: (h, i, 0, 0))
    row_shape = jax.ShapeDtypeStruct((half, nl // PEER_ROW, PEER_ROW, LANES), F32)
    tile_spec = pl.BlockSpec((N_KEYS, nb * LANES), lambda i, h: (h, i))
    tile_shape = jax.ShapeDtypeStruct((half, n), BF16)
    c, e1, q, e2 = pl.pallas_call(
        _peer_thresh_kernel,
        grid=(nl // nb, PEER_HEADS),
        in_specs=[pl.BlockSpec((steps, 2 * N_KEYS * cpt, LANES), lambda i, h: (i, h, 0))],
        out_specs=[row_spec, row_spec, tile_spec, tile_spec],
        out_shape=[row_shape, row_shape, tile_shape, tile_shape],
        scratch_shapes=[pltpu.VMEM((N_KEYS * nb, LANES), F32), pltpu.VMEM((N_KEYS * nb, LANES), F32)],
        compiler_params=_params(("parallel", "parallel")),
        name="peer_thresh",
    )(s3)
    shape5 = (PEER_HEADS, N_KEYS, nl // PEER_ROW, PEER_ROW, LANES)
    tiles = lambda v: v.reshape(PEER_HEADS, N_KEYS, n)
    return c.reshape(shape5), e1.reshape(shape5), tiles(q), tiles(e2)


def _peer_dense_kernel(xt_ref, u_ref, vt_ref, q_ref, e2_ref, c_ref, e1_ref, o_ref, acc_scr, a_scr, *, sub):
    e = pl.program_id(1)
    eb = u_ref.shape[0]

    @pl.when(e == 0)
    def _():
        acc_scr[...] = jnp.zeros(acc_scr.shape, F32)

    def row(ref, h, il):
        return jnp.concatenate([ref[h, il, 0, c:c + 1, :] for c in range(ref.shape[3])], axis=1).astype(BF16)

    zero = jnp.zeros((), BF16)

    xt = xt_ref[...]
    nsub = eb // sub
    score = lambda s: _dot(u_ref[s * sub:(s + 1) * sub, :], xt)
    st_next = score(0)
    for s in range(nsub):
        st, st_next = st_next, (score(s + 1) if s + 1 < nsub else None)
        for ii in range(sub // N_KEYS):
            il = s * (sub // N_KEYS) + ii
            gate = None
            for h in range(PEER_HEADS):
                term = row(e1_ref, h, il) * jnp.where(q_ref[h] >= row(c_ref, h, il), e2_ref[h], zero)
                gate = term if gate is None else gate + term
            a = _gelu_x2(st[ii * N_KEYS:(ii + 1) * N_KEYS, :]).astype(BF16) * gate
            a_scr[s * sub + ii * N_KEYS:s * sub + (ii + 1) * N_KEYS, :] = a
        acc_scr[...] += _dot(vt_ref[:, s * sub:(s + 1) * sub], a_scr[s * sub:(s + 1) * sub, :])

    @pl.when(e == pl.num_programs(1) - 1)
    def _():
        o_ref[...] = acc_scr[...].astype(o_ref.dtype)


def _peer_dense(h2t, q, e2, c, e1, u_tab, vt_tab):
    n = h2t.shape[1]
    t = min(n, PEER_ROW * LANES)
    eb = PEER_EXPERT_BLOCK
    row_spec = pl.BlockSpec((PEER_HEADS, eb // N_KEYS, 1, PEER_ROW, LANES), lambda i, e: (0, e, i, 0, 0))
    return pl.pallas_call(
        functools.partial(_peer_dense_kernel, sub=PEER_SUB),
        grid=(n // t, N_EXPERTS // eb),
        in_specs=[
            pl.BlockSpec((D_MODEL, t), lambda i, e: (0, i)),
            pl.BlockSpec((eb, D_MODEL), lambda i, e: (e, 0)),
            pl.BlockSpec((D_MODEL, eb), lambda i, e: (0, e)),
            pl.BlockSpec((PEER_HEADS, N_KEYS, t), lambda i, e: (0, 0, i)),
            pl.BlockSpec((PEER_HEADS, N_KEYS, t), lambda i, e: (0, 0, i)),
            row_spec, row_spec,
        ],
        out_specs=pl.BlockSpec((D_MODEL, t), lambda i, e: (0, i)),
        out_shape=jax.ShapeDtypeStruct((D_MODEL, n), BF16),
        scratch_shapes=[
            pltpu.VMEM((D_MODEL, t), F32),
            pltpu.VMEM((eb, t), BF16),
        ],
        compiler_params=_params(("parallel", "arbitrary")),
        name="peer_dense",
    )(h2t, u_tab, vt_tab, q, e2, c, e1)


def _final_ln_kernel(yt_ref, h2_ref, g_ref, b_ref, o_ref):
    y = yt_ref[...].astype(F32).T
    o_ref[...] = _ln(DEEPNORM_ALPHA * h2_ref[...].astype(F32) + y, g_ref[...], b_ref[...])


def _final_ln(yt, h2, ln_g, ln_b):
    n = h2.shape[0]
    tm = min(n, 512)
    vec = pl.BlockSpec((1, D_MODEL), lambda i: (0, 0))
    return pl.pallas_call(
        _final_ln_kernel,
        grid=(n // tm,),
        in_specs=[
            pl.BlockSpec((D_MODEL, tm), lambda i: (0, i)),
            pl.BlockSpec((tm, D_MODEL), lambda i: (i, 0)),
            vec, vec,
        ],
        out_specs=pl.BlockSpec((tm, D_MODEL), lambda i: (i, 0)),
        out_shape=jax.ShapeDtypeStruct((n, D_MODEL), F32),
        compiler_params=_params(("parallel",)),
        name="final_ln",
    )(yt, h2, ln_g, ln_b)


def _rot_half_cols(w):
    half = QK_ROPE // 2
    return jnp.concatenate([-w[..., half:], w[..., :half]], axis=-1)


def kernel(x, ln_in_g, ln_in_b, w_in, q_norm_g, w_uq, kv_norm_g, w_ukv, ssm_lam_re, ssm_lam_im, ssm_log_dt,
           ssm_b_re, ssm_b_im, ssm_c_re, ssm_c_im, ssm_d, w_glu, w_proj_attn, w_proj_ssm, w_out, ln_mix_g, ln_mix_b,
           w_peer_q, peer_sub_keys, peer_u, peer_v, ln_ffn_g, ln_ffn_b):
    batch, seq, _ = x.shape
    n = batch * seq
    x2 = x.reshape(n, D_MODEL)
    row = lambda v: v.reshape(1, -1).astype(F32)
    l = 0

    wi = w_in[l]
    w_main = jnp.concatenate([wi[:, OFF_SSM:OFF_KR], wi[:, OFF_GA:IN_WIDTH]], axis=1).astype(BF16)
    w_krc = wi[:, OFF_KR:OFF_GA]
    w_kr = jnp.concatenate([w_krc, _rot_half_cols(w_krc)], axis=1).astype(BF16)
    wq = w_uq[l].reshape(Q_LORA, N_HEADS, QK_HEAD)
    wq_ext = jnp.concatenate([wq, _rot_half_cols(wq[..., QK_NOPE:])], axis=-1)
    wq_ext = wq_ext.transpose(1, 0, 2).astype(BF16)
    wkv = w_ukv[l].reshape(KV_LORA, N_HEADS, QK_NOPE + V_HEAD).transpose(1, 0, 2).astype(BF16)

    cs = _rope_table(seq)
    u, z, kr = _inproj(x2, row(ln_in_g), row(ln_in_b), w_main, w_kr)

    q, k, v = _qkv(z, kr, cs, row(q_norm_g[l]), row(kv_norm_g[l]), wq_ext, wkv, batch, seq)
    attn = _attention(q, k, v).reshape(n, N_HEADS * V_HEAD)

    w, ws, wx, a16 = _ssm_wgen(ssm_lam_re[l], ssm_lam_im[l], ssm_log_dt[l], ssm_b_re[l], ssm_b_im[l],
                               ssm_c_re[l], ssm_c_im[l])
    g = _ssm_branch(u, batch, seq, w, ws, wx, a16, row(ssm_d[l]))

    merged = _merge(attn, g, z, w_glu[l].astype(BF16), w_proj_attn[l].astype(BF16), w_proj_ssm[l].astype(BF16))
    h2, h2t = _outproj(merged, x2, row(ln_in_g), row(ln_in_b), row(ln_mix_g[l]), row(ln_mix_b[l]),
                       w_out[l].astype(BF16))

    wcomb = _peer_wcomb(peer_sub_keys[l], w_peer_q[l].astype(BF16))
    s3 = _peer_scores(wcomb, h2t)
    c, e1, q, e2 = _peer_thresh(s3)
    yt = _peer_dense(h2t, q, e2, c, e1, peer_u[l].astype(BF16), peer_v[l].T.astype(BF16))
    out = _final_ln(yt, h2, row(ln_ffn_g[l]), row(ln_ffn_b[l]))
    return out.reshape(batch, seq, D_MODEL)
```

```python
import functools
import math

import jax
import jax.numpy as jnp
from jax import lax
from jax.experimental import pallas as pl
from jax.experimental.pallas import tpu as pltpu

F32 = jnp.float32
BF16 = jnp.bfloat16

D_MODEL = 2048
DEPTH = 1
CHUNK = 64
SSM_WIDTH = 1024
SSM_GROUP = 16
SSM_GROUPS = SSM_WIDTH // SSM_GROUP
SSM_STATE = 64
N_HEADS = 16
Q_LORA = 512
KV_LORA = 512
QK_NOPE = 128
QK_ROPE = 64
V_HEAD = 128
QK_HEAD = QK_NOPE + QK_ROPE
ROPE_THETA = 10000.0
PEER_HEADS = 8
N_KEYS = 128
N_EXPERTS = N_KEYS * N_KEYS
PEER_KEY_HALF = 128
PEER_TOPK = 16
LN_EPS = 1e-5
RMS_EPS = 1e-6
DEEPNORM_ALPHA = (2.0 * DEPTH) ** 0.25
OFF_SSM = 0
OFF_Q = OFF_SSM + SSM_WIDTH
OFF_KV = OFF_Q + Q_LORA
OFF_KR = OFF_KV + KV_LORA
OFF_GA = OFF_KR + QK_ROPE
OFF_GS = OFF_GA + D_MODEL
IN_WIDTH = OFF_GS + D_MODEL

LANES = 128
VMEM_LIMIT_BYTES = 56 * 1024 * 1024

SSM_CHUNK = 16
SSM_TILE_GROUPS = LANES // SSM_GROUP
SSM_TILES = SSM_WIDTH // LANES
SSM_TILE_STATE = SSM_TILE_GROUPS * SSM_STATE
NEG_BIG = -1e30


def _params(semantics):
    return pltpu.CompilerParams(dimension_semantics=semantics, vmem_limit_bytes=VMEM_LIMIT_BYTES)


def _dot(a, b):
    return jnp.dot(a, b, preferred_element_type=F32)


def _ln(x, g, b):
    mu = jnp.mean(x, axis=-1, keepdims=True)
    xc = x - mu
    var = jnp.mean(xc * xc, axis=-1, keepdims=True)
    return xc * lax.rsqrt(var + LN_EPS) * g + b


GELU_C = 0.7978845608028654
GELU_A = 0.044715


def _gelu(x):
    return 0.5 * x * (1.0 + jnp.tanh(GELU_C * (x + GELU_A * (x * x * x))))


def _gelu_x2(x):
    return x * (1.0 + jnp.tanh(x * (GELU_C + (GELU_C * GELU_A) * (x * x))))


def _sigmoid(x):
    return 1.0 / (1.0 + jnp.exp(-x))


def _rope_table_kernel(invf_ref, o_ref):
    ts = o_ref.shape[0]
    pos = (lax.broadcasted_iota(jnp.int32, (ts, LANES), 0) + pl.program_id(0) * ts).astype(F32)
    ang = pos * invf_ref[...]
    lane = lax.broadcasted_iota(jnp.int32, (ts, LANES), 1)
    o_ref[...] = jnp.where(lane < 2 * (QK_ROPE // 2), jnp.cos(ang), jnp.sin(ang))


def _rope_table(seq):
    half = QK_ROPE // 2
    invf = 1.0 / (ROPE_THETA ** (jnp.arange(half, dtype=F32) / half))
    invf4 = jnp.tile(invf, 4)[None, :]
    ts = min(seq, 1024)
    return pl.pallas_call(
        _rope_table_kernel,
        grid=(seq // ts,),
        in_specs=[pl.BlockSpec((1, LANES), lambda i: (0, 0))],
        out_specs=pl.BlockSpec((ts, LANES), lambda i: (i, 0)),
        out_shape=jax.ShapeDtypeStruct((seq, LANES), F32),
        compiler_params=_params(("parallel",)),
        name="rope_table",
    )(invf4)


def _inproj_kernel(x_ref, g_ref, b_ref, w_ref, wkr_ref, u_ref, z_ref, kr_ref, h_scr):
    j = pl.program_id(1)

    @pl.when(j == 0)
    def _():
        h = _ln(x_ref[...], g_ref[...], b_ref[...]).astype(BF16)
        h_scr[...] = h
        u_ref[...] = _dot(h, w_ref[...])
        kr_ref[...] = _dot(h, wkr_ref[...]).astype(kr_ref.dtype)

    @pl.when(j > 0)
    def _():
        z_ref[...] = _dot(h_scr[...], w_ref[...]).astype(z_ref.dtype)


def _inproj(x2, ln_g, ln_b, w_main, w_kr):
    n = x2.shape[0]
    tm = min(n, 1024)
    tn = 1024
    nj = w_main.shape[1] // tn
    return pl.pallas_call(
        _inproj_kernel,
        grid=(n // tm, nj),
        in_specs=[
            pl.BlockSpec((tm, D_MODEL), lambda i, j: (i, 0)),
            pl.BlockSpec((1, D_MODEL), lambda i, j: (0, 0)),
            pl.BlockSpec((1, D_MODEL), lambda i, j: (0, 0)),
            pl.BlockSpec((D_MODEL, tn), lambda i, j: (0, j)),
            pl.BlockSpec((D_MODEL, LANES), lambda i, j: (0, 0)),
        ],
        out_specs=[
            pl.BlockSpec((tm, tn), lambda i, j: (i, 0)),
            pl.BlockSpec((tm, tn), lambda i, j: (i, jnp.maximum(j - 1, 0))),
            pl.BlockSpec((tm, LANES), lambda i, j: (i, 0)),
        ],
        out_shape=[
            jax.ShapeDtypeStruct((n, tn), F32),
            jax.ShapeDtypeStruct((n, (nj - 1) * tn), BF16),
            jax.ShapeDtypeStruct((n, LANES), BF16),
        ],
        scratch_shapes=[pltpu.VMEM((tm, D_MODEL), BF16)],
        compiler_params=_params(("parallel", "arbitrary")),
        name="ln_inproj",
    )(x2, ln_g, ln_b, w_main, w_kr)


def _rope_mix(t2, cs):
    t = t2 * cs
    return t + pltpu.roll(t, QK_ROPE, 1)


def _qkv_kernel(cq_ref, ckv_ref, kr_ref, cs_ref, gq_ref, gkv_ref, wq_ref, wkv_ref,
                q_ref, k_ref, v_ref, cqn_scr, ckvn_scr, kro_scr):
    cs = cs_ref[...]

    @pl.when(pl.program_id(1) == 0)
    def _():
        def rms(c, g):
            c = c.astype(F32)
            return (c * lax.rsqrt(jnp.mean(c * c, axis=-1, keepdims=True) + RMS_EPS) * g).astype(BF16)

        cqn_scr[...] = rms(cq_ref[...], gq_ref[...])
        ckvn_scr[...] = rms(ckv_ref[...], gkv_ref[...])
        kro_scr[...] = _rope_mix(kr_ref[...].astype(F32), cs).astype(BF16)

    scale = math.log2(math.e) / math.sqrt(QK_HEAD)
    hp = wq_ref.shape[0]
    wide = 2 * LANES
    qe2 = _dot(cqn_scr[...], jnp.concatenate([wq_ref[j] for j in range(hp)], axis=1))
    kv2 = _dot(ckvn_scr[...], jnp.concatenate([wkv_ref[j] for j in range(hp)], axis=1))
    for j in range(hp):
        qe = qe2[:, j * wide:(j + 1) * wide]
        qr = _rope_mix(qe[:, QK_NOPE:], cs)
        q_ref[0, j, :, :QK_NOPE] = (qe[:, :QK_NOPE] * scale).astype(BF16)
        q_ref[0, j, :, QK_NOPE:] = (qr[:, :QK_ROPE] * scale).astype(BF16)
        kv = kv2[:, j * wide:(j + 1) * wide]
        k_ref[0, j, :, :QK_NOPE] = kv[:, :QK_NOPE].astype(BF16)
        k_ref[0, j, :, QK_NOPE:] = kro_scr[:, :QK_ROPE]
        v_ref[0, j] = kv[:, QK_NOPE:].astype(BF16)


def _qkv(z, kr, cs, gq, gkv, wq_ext, wkv, batch, seq):
    n = z.shape[0]
    tm = min(seq, 2048)
    nsb = seq // tm
    hp = 4
    return pl.pallas_call(
        _qkv_kernel,
        grid=(n // tm, N_HEADS // hp),
        in_specs=[
            pl.BlockSpec((tm, Q_LORA), lambda i, h: (i, 0)),
            pl.BlockSpec((tm, KV_LORA), lambda i, h: (i, 1)),
            pl.BlockSpec((tm, LANES), lambda i, h: (i, 0)),
            pl.BlockSpec((tm, LANES), lambda i, h: (i % nsb, 0)),
            pl.BlockSpec((1, Q_LORA), lambda i, h: (0, 0)),
            pl.BlockSpec((1, KV_LORA), lambda i, h: (0, 0)),
            pl.BlockSpec((hp, Q_LORA, 2 * LANES), lambda i, h: (h, 0, 0)),
            pl.BlockSpec((hp, KV_LORA, 2 * LANES), lambda i, h: (h, 0, 0)),
        ],
        out_specs=[
            pl.BlockSpec((1, hp, tm, QK_HEAD), lambda i, h: (i // nsb, h, i % nsb, 0)),
            pl.BlockSpec((1, hp, tm, QK_HEAD), lambda i, h: (i // nsb, h, i % nsb, 0)),
            pl.BlockSpec((1, hp, tm, V_HEAD), lambda i, h: (i // nsb, h, i % nsb, 0)),
        ],
        out_shape=[
            jax.ShapeDtypeStruct((batch, N_HEADS, seq, QK_HEAD), BF16),
            jax.ShapeDtypeStruct((batch, N_HEADS, seq, QK_HEAD), BF16),
            jax.ShapeDtypeStruct((batch, N_HEADS, seq, V_HEAD), BF16),
        ],
        scratch_shapes=[
            pltpu.VMEM((tm, Q_LORA), BF16),
            pltpu.VMEM((tm, KV_LORA), BF16),
            pltpu.VMEM((tm, LANES), BF16),
        ],
        compiler_params=_params(("parallel", "arbitrary")),
        name="qkv_rope",
    )(z, z, kr, cs, gq, gkv, wq_ext, wkv)


ATT_ROWS = 512
ATT_CHAINS = 8
ATT_BK_MAIN = 1024


def _attn_kernel(q_ref, k_ref, v_ref, o_ref, m_scr, acc_scr, *, rows, chains, bkm):
    qi = pl.program_id(2)
    m_scr[...] = jnp.full(m_scr.shape, -jnp.inf, F32)
    acc_scr[...] = jnp.zeros(acc_scr.shape, F32)

    def load_kv(j, bk):
        start = pl.multiple_of(j * bk, bk)
        kj = k_ref[0, 0, pl.ds(start, bk), :]
        vj = jnp.concatenate([v_ref[0, 0, pl.ds(start, bk), :], jnp.ones((bk, LANES), BF16)], axis=1)
        return kj, vj

    def block(c, kj, vj, masked):
        q = q_ref[0, 0, c * rows:(c + 1) * rows, :]
        s = lax.dot_general(q, kj, (((1,), (1,)), ((), ())), preferred_element_type=F32)
        if masked:
            rc = lax.broadcasted_iota(jnp.int32, s.shape, 0) // CHUNK
            cc = lax.broadcasted_iota(jnp.int32, s.shape, 1) // CHUNK
            s = jnp.where(cc <= rc, s, NEG_BIG)
        m_prev = m_scr[c]
        m_new = jnp.maximum(m_prev, jnp.max(s, axis=-1, keepdims=True))
        alpha = jnp.exp2(m_prev - m_new)
        p = jnp.exp2(s - jnp.tile(m_new, (1, s.shape[1] // LANES)))
        acc_scr[c] = jnp.tile(alpha, (1, 2)) * acc_scr[c] + _dot(p.astype(BF16), vj)
        m_scr[c] = m_new

    per_step = chains * rows // bkm
    group = 2 if per_step % 2 == 0 else 1

    def body(j, carry):
        for jj in range(group):
            kj, vj = load_kv(j * group + jj, bkm)
            for c in range(chains):
                block(c, kj, vj, False)
        return carry

    lax.fori_loop(0, qi * (per_step // group), body, 0)
    for jj in range(chains):
        kj, vj = load_kv(qi * chains + jj, rows)
        for c in range(jj, chains):
            block(c, kj, vj, c == jj)
    for c in range(chains):
        a = acc_scr[c]
        o_ref[0, c * rows:(c + 1) * rows, :] = (a[:, :V_HEAD] / a[:, V_HEAD:]).astype(o_ref.dtype)


def _attention(q, k, v):
    batch, _, seq, _ = q.shape
    rows = min(seq, ATT_ROWS)
    chains = min(ATT_CHAINS, seq // rows)
    bq = rows * chains
    bkm = min(ATT_BK_MAIN, bq)
    return pl.pallas_call(
        functools.partial(_attn_kernel, rows=rows, chains=chains, bkm=bkm),
        grid=(batch, N_HEADS, seq // bq),
        in_specs=[
            pl.BlockSpec((1, 1, bq, QK_HEAD), lambda b, h, i: (b, h, i, 0)),
            pl.BlockSpec((1, 1, seq, QK_HEAD), lambda b, h, i: (b, h, 0, 0)),
            pl.BlockSpec((1, 1, seq, V_HEAD), lambda b, h, i: (b, h, 0, 0)),
        ],
        out_specs=pl.BlockSpec((1, bq, V_HEAD), lambda b, h, i: (b, i, h)),
        out_shape=jax.ShapeDtypeStruct((batch, seq, N_HEADS * V_HEAD), BF16),
        scratch_shapes=[
            pltpu.VMEM((chains, rows, LANES), F32),
            pltpu.VMEM((chains, rows, 2 * V_HEAD), F32),
        ],
        compiler_params=_params(("parallel", "parallel", "arbitrary")),
        name="flash_attn",
    )(q, k, v)


def _cpow(lr, li, ldt, k):
    dt = jnp.exp(ldt)
    mag = jnp.exp(lr * dt * k)
    ang = li * dt * k
    return mag * jnp.cos(ang), mag * jnp.sin(ang)


def _ssm_wgen_kernel(lr_r, li_r, ldt_r, bre_ref, bim_ref, cre_ref, cim_ref,
                     w_ref, ws_ref, wx_ref, a16_ref):
    hp = lax.Precision.HIGHEST
    ns = SSM_TILE_STATE
    lr, li, ldt = lr_r[0], li_r[0], ldt_r[0]
    a_re, a_im = _cpow(lr, li, ldt, 1.0)
    den = lr * lr + li * li
    nr = a_re - 1.0
    ni = a_im
    cf_re = (nr * lr + ni * li) / den
    cf_im = (ni * lr - nr * li) / den
    bre, bim = bre_ref[0], bim_ref[0]
    cre, cim = cre_ref[0], cim_ref[0]
    w_ref[...] = jnp.zeros(w_ref.shape, w_ref.dtype)
    for k in range(SSM_CHUNK):
        p_re, p_im = _cpow(lr, li, ldt, float(k))
        m_re = cf_re * p_re - cf_im * p_im
        m_im = cf_re * p_im + cf_im * p_re
        bm_re = bre * m_re - bim * m_im
        bm_im = bre * m_im + bim * m_re
        bd = (jnp.dot(bm_re, cre, precision=hp, preferred_element_type=F32)
              - jnp.dot(bm_im, cim, precision=hp, preferred_element_type=F32)).astype(BF16)
        for j in range(SSM_CHUNK - k):
            i = j + k
            w_ref[0, j * LANES:(j + 1) * LANES, i * LANES:(i + 1) * LANES] = bd
        jj = SSM_CHUNK - 1 - k
        ws_ref[0, jj * LANES:(jj + 1) * LANES, :ns] = bm_re.astype(BF16)
        ws_ref[0, jj * LANES:(jj + 1) * LANES, ns:] = bm_im.astype(BF16)
    cre_t, cim_t = cre.T, cim.T
    for i in range(SSM_CHUNK):
        p_re, p_im = _cpow(lr, li, ldt, float(i + 1))
        wx_ref[0, :ns, i * LANES:(i + 1) * LANES] = (p_re * cre_t - p_im * cim_t).T.astype(BF16)
        wx_ref[0, ns:, i * LANES:(i + 1) * LANES] = (-(p_re * cim_t + p_im * cre_t)).T.astype(BF16)
    q_re, q_im = _cpow(lr, li, ldt, float(SSM_CHUNK))
    a16_ref[0, :, :ns] = q_re
    a16_ref[0, :, ns:] = q_im


def _ssm_wgen(lam_re, lam_im, log_dt, b_re, b_im, c_re, c_im):
    nt, tg, ns = SSM_TILES, SSM_TILE_GROUPS, SSM_TILE_STATE
    eye = jnp.eye(tg, dtype=F32)
    bbd = lambda b: jnp.einsum("cgph,gk->cghkp", b.reshape(nt, tg, SSM_STATE, SSM_GROUP), eye).reshape(nt, LANES, ns)
    cbd = lambda c: jnp.einsum("cghp,gk->cgpkh", c.reshape(nt, tg, SSM_GROUP, SSM_STATE), eye).reshape(nt, ns, LANES)
    ldt = jnp.repeat(log_dt, SSM_STATE)
    rows = [a.reshape(nt, 1, ns) for a in (lam_re, lam_im, ldt)]
    kc = SSM_CHUNK * LANES
    row_spec = pl.BlockSpec((1, 1, ns), lambda c: (c, 0, 0))
    return pl.pallas_call(
        _ssm_wgen_kernel,
        grid=(nt,),
        in_specs=[row_spec] * 3 + [
            pl.BlockSpec((1, LANES, ns), lambda c: (c, 0, 0)),
            pl.BlockSpec((1, LANES, ns), lambda c: (c, 0, 0)),
            pl.BlockSpec((1, ns, LANES), lambda c: (c, 0, 0)),
            pl.BlockSpec((1, ns, LANES), lambda c: (c, 0, 0)),
        ],
        out_specs=[
            pl.BlockSpec((1, kc, kc), lambda c: (c, 0, 0)),
            pl.BlockSpec((1, kc, 2 * ns), lambda c: (c, 0, 0)),
            pl.BlockSpec((1, 2 * ns, kc), lambda c: (c, 0, 0)),
            pl.BlockSpec((1, 1, 2 * ns), lambda c: (c, 0, 0)),
        ],
        out_shape=[
            jax.ShapeDtypeStruct((nt, kc, kc), BF16),
            jax.ShapeDtypeStruct((nt, kc, 2 * ns), BF16),
            jax.ShapeDtypeStruct((nt, 2 * ns, kc), BF16),
            jax.ShapeDtypeStruct((nt, 1, 2 * ns), F32),
        ],
        compiler_params=_params(("parallel",)),
        name="ssm_wgen",
    )(*rows, bbd(b_re), bbd(b_im), cbd(c_re), cbd(c_im))


def _chunk_rows(u_ref, tb):
    return [u_ref[pl.ds(j, tb, stride=SSM_CHUNK), :] for j in range(SSM_CHUNK)]


def _ssm_sum_kernel(u_ref, ws_ref, s_ref, *, tb):
    u16 = jnp.concatenate(_chunk_rows(u_ref, tb), axis=1).astype(BF16)
    s_ref[0] = _dot(u16, ws_ref[0])


def _ssm_scan_kernel(s_ref, a16_ref, x_ref):
    ns = SSM_TILE_STATE
    a_re = a16_ref[0, :, :ns]
    a_im = a16_ref[0, :, ns:]
    nrow = s_ref.shape[1]

    def body(r, carry):
        xr, xi = carry
        x_ref[0, pl.ds(r, 1), :ns] = xr
        x_ref[0, pl.ds(r, 1), ns:] = xi
        sr = s_ref[0, pl.ds(r, 1), :ns]
        si = s_ref[0, pl.ds(r, 1), ns:]
        return a_re * xr - a_im * xi + sr, a_re * xi + a_im * xr + si

    zero = jnp.zeros((1, ns), F32)
    lax.fori_loop(0, nrow, body, (zero, zero), unroll=8)


def _ssm_out_kernel(u_ref, x_ref, w_ref, wx_ref, d_ref, g_ref, *, tb):
    us = _chunk_rows(u_ref, tb)
    u16 = jnp.concatenate(us, axis=1).astype(BF16)
    y16 = _dot(u16, w_ref[0]) + _dot(x_ref[0].astype(BF16), wx_ref[0])
    d = d_ref[...]
    for i in range(SSM_CHUNK):
        y = y16[:, i * LANES:(i + 1) * LANES] + d * us[i]
        g_ref[pl.ds(i, tb, stride=SSM_CHUNK), :] = _gelu(y)


def _ssm_branch(u, batch, seq, w, ws, wx, a16, d_skip):
    n = u.shape[0]
    nt, ns = SSM_TILES, SSM_TILE_STATE
    kc = SSM_CHUNK * LANES
    nb = n // SSM_CHUNK
    nbs = seq // SSM_CHUNK
    tb = min(nbs, 512)
    tok = tb * SSM_CHUNK
    s = pl.pallas_call(
        functools.partial(_ssm_sum_kernel, tb=tb),
        grid=(nt, nb // tb),
        in_specs=[
            pl.BlockSpec((tok, LANES), lambda c, r: (r, c)),
            pl.BlockSpec((1, kc, 2 * ns), lambda c, r: (c, 0, 0)),
        ],
        out_specs=pl.BlockSpec((1, tb, 2 * ns), lambda c, r: (c, r, 0)),
        out_shape=jax.ShapeDtypeStruct((nt, nb, 2 * ns), F32),
        compiler_params=_params(("parallel", "parallel")),
        name="ssm_chunk_sum",
    )(u, ws)
    x = pl.pallas_call(
        _ssm_scan_kernel,
        grid=(nt, batch),
        in_specs=[
            pl.BlockSpec((1, nbs, 2 * ns), lambda c, b: (c, b, 0)),
            pl.BlockSpec((1, 1, 2 * ns), lambda c, b: (c, 0, 0)),
        ],
        out_specs=pl.BlockSpec((1, nbs, 2 * ns), lambda c, b: (c, b, 0)),
        out_shape=jax.ShapeDtypeStruct((nt, nb, 2 * ns), F32),
        compiler_params=_params(("parallel", "parallel")),
        name="ssm_state_scan",
    )(s, a16)
    return pl.pallas_call(
        functools.partial(_ssm_out_kernel, tb=tb),
        grid=(nt, nb // tb),
        in_specs=[
            pl.BlockSpec((tok, LANES), lambda c, r: (r, c)),
            pl.BlockSpec((1, tb, 2 * ns), lambda c, r: (c, r, 0)),
            pl.BlockSpec((1, kc, kc), lambda c, r: (c, 0, 0)),
            pl.BlockSpec((1, 2 * ns, kc), lambda c, r: (c, 0, 0)),
            pl.BlockSpec((1, LANES), lambda c, r: (0, c)),
        ],
        out_specs=pl.BlockSpec((tok, LANES), lambda c, r: (r, c)),
        out_shape=jax.ShapeDtypeStruct((n, SSM_WIDTH), F32),
        compiler_params=_params(("parallel", "parallel")),
        name="ssm_chunk_out",
    )(u, x, w, wx, d_skip)


def _merge_kernel(attn_ref, g_ref, ga_ref, gs_ref, wglu_ref, wpa_ref, wps_ref, o_ref, ssm_scr):
    @pl.when(pl.program_id(1) == 0)
    def _():
        g = g_ref[...]
        ssm_scr[...] = (g * _sigmoid(_dot(g.astype(BF16), wglu_ref[...]))).astype(BF16)

    pa = _dot(attn_ref[...], wpa_ref[...])
    ps = _dot(ssm_scr[...], wps_ref[...])
    o_ref[...] = (_sigmoid(ga_ref[...].astype(F32)) * pa + _sigmoid(gs_ref[...].astype(F32)) * ps).astype(o_ref.dtype)


def _merge(attn, g, z, w_glu, w_pa, w_ps):
    n = attn.shape[0]
    tm = min(n, 1024)
    tn = 1024
    nj = D_MODEL // tn
    return pl.pallas_call(
        _merge_kernel,
        grid=(n // tm, nj),
        in_specs=[
            pl.BlockSpec((tm, N_HEADS * V_HEAD), lambda i, j: (i, 0)),
            pl.BlockSpec((tm, SSM_WIDTH), lambda i, j: (i, 0)),
            pl.BlockSpec((tm, tn), lambda i, j: (i, 1 + j)),
            pl.BlockSpec((tm, tn), lambda i, j: (i, 1 + nj + j)),
            pl.BlockSpec((SSM_WIDTH, SSM_WIDTH), lambda i, j: (0, 0)),
            pl.BlockSpec((N_HEADS * V_HEAD, tn), lambda i, j: (0, j)),
            pl.BlockSpec((SSM_WIDTH, tn), lambda i, j: (0, j)),
        ],
        out_specs=pl.BlockSpec((tm, tn), lambda i, j: (i, j)),
        out_shape=jax.ShapeDtypeStruct((n, D_MODEL), BF16),
        scratch_shapes=[pltpu.VMEM((tm, SSM_WIDTH), BF16)],
        compiler_params=_params(("parallel", "arbitrary")),
        name="glu_merge",
    )(attn, g, z, z, w_glu, w_pa, w_ps)


def _outproj_kernel(m_ref, x_ref, g1_ref, b1_ref, g2_ref, b2_ref, w_ref, h2_ref, h2t_ref):
    y = _dot(m_ref[...], w_ref[...])
    h = _ln(x_ref[...], g1_ref[...], b1_ref[...])
    h2 = _ln(DEEPNORM_ALPHA * h + y, g2_ref[...], b2_ref[...])
    h2_ref[...] = h2.astype(h2_ref.dtype)
    h2t_ref[...] = h2.T.astype(h2t_ref.dtype)


def _outproj(merged, x2, g1, b1, g2, b2, w_out):
    n = merged.shape[0]
    tm = min(n, 512)
    vec = pl.BlockSpec((1, D_MODEL), lambda i: (0, 0))
    return pl.pallas_call(
        _outproj_kernel,
        grid=(n // tm,),
        in_specs=[
            pl.BlockSpec((tm, D_MODEL), lambda i: (i, 0)),
            pl.BlockSpec((tm, D_MODEL), lambda i: (i, 0)),
            vec, vec, vec, vec,
            pl.BlockSpec((D_MODEL, D_MODEL), lambda i: (0, 0)),
        ],
        out_specs=[
            pl.BlockSpec((tm, D_MODEL), lambda i: (i, 0)),
            pl.BlockSpec((D_MODEL, tm), lambda i: (0, i)),
        ],
        out_shape=[
            jax.ShapeDtypeStruct((n, D_MODEL), BF16),
            jax.ShapeDtypeStruct((D_MODEL, n), BF16),
        ],
        compiler_params=_params(("parallel",)),
        name="outproj_ln",
    )(merged, x2, g1, b1, g2, b2, w_out)


def _peer_wcomb_kernel(keys_ref, wpq_ref, o_ref):
    o_ref[...] = lax.dot_general(keys_ref[0].astype(BF16), wpq_ref[...], (((1,), (1,)), ((), ())),
                                 preferred_element_type=F32).astype(o_ref.dtype)


def _peer_wcomb(sub_keys, w_pq):
    nhc = PEER_HEADS * 2
    keys = sub_keys.reshape(nhc, N_KEYS, PEER_KEY_HALF)
    return pl.pallas_call(
        _peer_wcomb_kernel,
        grid=(nhc,),
        in_specs=[
            pl.BlockSpec((1, N_KEYS, PEER_KEY_HALF), lambda i: (i, 0, 0)),
            pl.BlockSpec((D_MODEL, PEER_KEY_HALF), lambda i: (0, i)),
        ],
        out_specs=pl.BlockSpec((N_KEYS, D_MODEL), lambda i: (i, 0)),
        out_shape=jax.ShapeDtypeStruct((nhc * N_KEYS, D_MODEL), BF16),
        compiler_params=_params(("parallel",)),
        name="peer_wcomb",
    )(keys, w_pq)


PEER_TOK = 1024
PEER_THRESH_TOK = 2048


def _peer_scores_kernel(w_ref, xt_ref, s3_ref):
    rows = w_ref.shape[0] // 2
    cpt = xt_ref.shape[1] // LANES
    for g in range(2):
        st = _dot(w_ref[g * rows:(g + 1) * rows, :], xt_ref[...])
        for c in range(cpt):
            s3_ref[0, pl.ds(g * rows * cpt + c, rows, stride=cpt), :] = st[:, c * LANES:(c + 1) * LANES]


def _peer_scores(wcomb, h2t):
    n = h2t.shape[1]
    tl = min(n, PEER_TOK)
    rows = wcomb.shape[0]
    cpt = tl // LANES
    return pl.pallas_call(
        _peer_scores_kernel,
        grid=(n // tl,),
        in_specs=[
            pl.BlockSpec((rows, D_MODEL), lambda i: (0, 0)),
            pl.BlockSpec((D_MODEL, tl), lambda i: (0, i)),
        ],
        out_specs=pl.BlockSpec((1, rows * cpt, LANES), lambda i: (i, 0, 0)),
        out_shape=jax.ShapeDtypeStruct((n // tl, rows * cpt, LANES), F32),
        compiler_params=_params(("parallel",)),
        name="peer_scores",
    )(wcomb, h2t)


PEER_LIST = PEER_TOPK + 1
PEER_EXPERT_BLOCK = 16 * N_KEYS
PEER_SUB = 8 * N_KEYS


PEER_ROW = 4


def _bitonic_merge(x, desc):
    n = len(x)
    if n == 1:
        return x
    h = n // 2
    big = [jnp.maximum(x[i], x[i + h]) for i in range(h)]
    small = [jnp.minimum(x[i], x[i + h]) for i in range(h)]
    first, second = (big, small) if desc else (small, big)
    return _bitonic_merge(first, desc) + _bitonic_merge(second, desc)


def _bitonic_sort(x, desc):
    n = len(x)
    if n == 1:
        return x
    h = n // 2
    return _bitonic_merge(_bitonic_sort(x[:h], True) + _bitonic_sort(x[h:], False), desc)


def _merge_top(x, y):
    n = len(x)
    big = [jnp.maximum(x[i], y[n - 1 - i]) for i in range(n)]
    rest = functools.reduce(jnp.maximum, [jnp.minimum(x[i], y[n - 1 - i]) for i in range(n)])
    return _bitonic_merge(big, True), rest


def _count_above(vals, x, strict):
    above = (lambda v: v > x) if strict else (lambda v: v >= x)
    pick = lambda m, hi, lo: jnp.where(m, hi, lo)
    m8 = above(vals[7])
    m4 = above(pick(m8, vals[11], vals[3]))
    m2 = above(pick(m8, pick(m4, vals[13], vals[9]), pick(m4, vals[5], vals[1])))
    hi = pick(m4, pick(m2, vals[14], vals[12]), pick(m2, vals[10], vals[8]))
    lo = pick(m4, pick(m2, vals[6], vals[4]), pick(m2, vals[2], vals[0]))
    m1 = above(pick(m8, hi, lo))
    cnt = (pick(m8, 8.0, 0.0) + pick(m4, 4.0, 0.0)) + (pick(m2, 2.0, 0.0) + pick(m1, 1.0, 0.0))
    return pick(above(vals[15]), 16.0, cnt)


def _peer_thresh_kernel(s_ref, c_ref, e1_ref, q_ref, e2_ref, q_scr, e2_scr):
    steps, cpt = s_ref.shape[0], s_ref.shape[1] // (2 * N_KEYS)
    shape = (steps * cpt, LANES)
    neg = jnp.full(shape, -jnp.inf, F32)

    def key(n):
        r0 = n * cpt if isinstance(n, int) else pl.multiple_of(n * cpt, cpt)
        return jnp.concatenate([s_ref[s, pl.ds(r0, cpt), :] for s in range(steps)], axis=0)

    def top_list(load, count):
        def tree(lo, hi):
            if hi - lo == PEER_TOPK:
                return _bitonic_sort([load(n) for n in range(lo, hi)], True), neg
            mid = (lo + hi) // 2
            (x, dx), (y, dy) = tree(lo, mid), tree(mid, hi)
            top, rest = _merge_top(x, y)
            return top, jnp.maximum(jnp.maximum(dx, dy), rest)

        top, nxt = tree(0, count)
        return top + [nxt]

    a = top_list(key, N_KEYS)
    b = top_list(lambda n: key(N_KEYS + n), N_KEYS)
    sums = [a[i - 1] + b[j - 1] for i in range(1, PEER_LIST + 1) for j in range(1, PEER_LIST // i + 1)]
    sums = sums + [neg] * (4 * PEER_TOPK - len(sums))
    t = top_list(lambda n: sums[n], len(sums))
    tau = 0.5 * (t[PEER_TOPK - 1] + t[PEER_TOPK])
    top = a[0] + b[0]
    z = jnp.exp(t[0] - top)
    for r in range(1, PEER_TOPK):
        z = z + jnp.exp(t[r] - top)
    cv = b[0] + jnp.log(2.0 * z)
    a16 = a[:PEER_TOPK]

    def write(n, carry):
        s1 = key(n)
        s2 = key(N_KEYS + n)
        c = 1.0 + _count_above(a16, s1, True)
        e1 = jnp.exp(s1 - a16[0])
        for gq in range(shape[0] // PEER_ROW):
            c_ref[n, gq] = c[gq * PEER_ROW:(gq + 1) * PEER_ROW, :]
            e1_ref[n, gq] = e1[gq * PEER_ROW:(gq + 1) * PEER_ROW, :]
        row0 = pl.multiple_of(n * nsub, nsub)
        q_scr[pl.ds(row0, nsub), :] = _count_above(a16, tau - s2, False)
        e2_scr[pl.ds(row0, nsub), :] = jnp.exp(s2 - cv)
        return carry

    nsub = shape[0]
    lax.fori_loop(0, N_KEYS, write, 0, unroll=2)
    for c in range(nsub):
        q_ref[:, c * LANES:(c + 1) * LANES] = q_scr[pl.ds(c, N_KEYS, stride=nsub), :].astype(q_ref.dtype)
        e2_ref[:, c * LANES:(c + 1) * LANES] = e2_scr[pl.ds(c, N_KEYS, stride=nsub), :].astype(e2_ref.dtype)


def _peer_thresh(s3):
    nsteps, rc, _ = s3.shape
    rows = 2 * N_KEYS * PEER_HEADS
    cpt = rc // rows
    nl = nsteps * cpt
    n = nl * LANES
    steps = min(max(PEER_THRESH_TOK // (cpt * LANES), 1), nsteps)
    nb = steps * cpt
    half = rows // 2
    ng = nb // PEER_ROW
    row_spec = pl.BlockSpec((N_KEYS, ng, PEER_ROW, LANES), lambda i, h: (h, i, 0, 0))
    row_shape = jax.ShapeDtypeStruct((half, nl // PEER_ROW, PEER_ROW, LANES), F32)
    tile_spec = pl.BlockSpec((N_KEYS, nb * LANES), lambda i, h: (h, i))
    tile_shape = jax.ShapeDtypeStruct((half, n), BF16)
    c, e1, q, e2 = pl.pallas_call(
        _peer_thresh_kernel,
        grid=(nl // nb, PEER_HEADS),
        in_specs=[pl.BlockSpec((steps, 2 * N_KEYS * cpt, LANES), lambda i, h: (i, h, 0))],
        out_specs=[row_spec, row_spec, tile_spec, tile_spec],
        out_shape=[row_shape, row_shape, tile_shape, tile_shape],
        scratch_shapes=[pltpu.VMEM((N_KEYS * nb, LANES), F32), pltpu.VMEM((N_KEYS * nb, LANES), F32)],
        compiler_params=_params(("parallel", "parallel")),
        name="peer_thresh",
    )(s3)
    shape5 = (PEER_HEADS, N_KEYS, nl // PEER_ROW, PEER_ROW, LANES)
    tiles = lambda v: v.reshape(PEER_HEADS, N_KEYS, n)
    return c.reshape(shape5), e1.reshape(shape5), tiles(q), tiles(e2)


def _peer_dense_kernel(xt_ref, u_ref, vt_ref, q_ref, e2_ref, c_ref, e1_ref, o_ref, acc_scr, a_scr, *, sub):
    e = pl.program_id(1)
    eb = u_ref.shape[0]

    @pl.when(e == 0)
    def _():
        acc_scr[...] = jnp.zeros(acc_scr.shape, F32)

    def row(ref, h, il):
        return jnp.concatenate([ref[h, il, 0, c:c + 1, :] for c in range(ref.shape[3])], axis=1).astype(BF16)

    zero = jnp.zeros((), BF16)

    xt = xt_ref[...]
    nsub = eb // sub
    score = lambda s: _dot(u_ref[s * sub:(s + 1) * sub, :], xt)
    st_next = score(0)
    for s in range(nsub):
        st, st_next = st_next, (score(s + 1) if s + 1 < nsub else None)
        for ii in range(sub // N_KEYS):
            il = s * (sub // N_KEYS) + ii
            gate = None
            for h in range(PEER_HEADS):
                term = row(e1_ref, h, il) * jnp.where(q_ref[h] >= row(c_ref, h, il), e2_ref[h], zero)
                gate = term if gate is None else gate + term
            a = _gelu_x2(st[ii * N_KEYS:(ii + 1) * N_KEYS, :]).astype(BF16) * gate
            a_scr[s * sub + ii * N_KEYS:s * sub + (ii + 1) * N_KEYS, :] = a
        acc_scr[...] += _dot(vt_ref[:, s * sub:(s + 1) * sub], a_scr[s * sub:(s + 1) * sub, :])

    @pl.when(e == pl.num_programs(1) - 1)
    def _():
        o_ref[...] = acc_scr[...].astype(o_ref.dtype)


def _peer_dense(h2t, q, e2, c, e1, u_tab, vt_tab):
    n = h2t.shape[1]
    t = min(n, PEER_ROW * LANES)
    eb = PEER_EXPERT_BLOCK
    row_spec = pl.BlockSpec((PEER_HEADS, eb // N_KEYS, 1, PEER_ROW, LANES), lambda i, e: (0, e, i, 0, 0))
    return pl.pallas_call(
        functools.partial(_peer_dense_kernel, sub=PEER_SUB),
        grid=(n // t, N_EXPERTS // eb),
        in_specs=[
            pl.BlockSpec((D_MODEL, t), lambda i, e: (0, i)),
            pl.BlockSpec((eb, D_MODEL), lambda i, e: (e, 0)),
            pl.BlockSpec((D_MODEL, eb), lambda i, e: (0, e)),
            pl.BlockSpec((PEER_HEADS, N_KEYS, t), lambda i, e: (0, 0, i)),
            pl.BlockSpec((PEER_HEADS, N_KEYS, t), lambda i, e: (0, 0, i)),
            row_spec, row_spec,
        ],
        out_specs=pl.BlockSpec((D_MODEL, t), lambda i, e: (0, i)),
        out_shape=jax.ShapeDtypeStruct((D_MODEL, n), BF16),
        scratch_shapes=[
            pltpu.VMEM((D_MODEL, t), F32),
            pltpu.VMEM((eb, t), BF16),
        ],
        compiler_params=_params(("parallel", "arbitrary")),
        name="peer_dense",
    )(h2t, u_tab, vt_tab, q, e2, c, e1)


def _final_ln_kernel(yt_ref, h2_ref, g_ref, b_ref, o_ref):
    y = yt_ref[...].astype(F32).T
    o_ref[...] = _ln(DEEPNORM_ALPHA * h2_ref[...].astype(F32) + y, g_ref[...], b_ref[...])


def _final_ln(yt, h2, ln_g, ln_b):
    n = h2.shape[0]
    tm = min(n, 512)
    vec = pl.BlockSpec((1, D_MODEL), lambda i: (0, 0))
    return pl.pallas_call(
        _final_ln_kernel,
        grid=(n // tm,),
        in_specs=[
            pl.BlockSpec((D_MODEL, tm), lambda i: (0, i)),
            pl.BlockSpec((tm, D_MODEL), lambda i: (i, 0)),
            vec, vec,
        ],
        out_specs=pl.BlockSpec((tm, D_MODEL), lambda i: (i, 0)),
        out_shape=jax.ShapeDtypeStruct((n, D_MODEL), F32),
        compiler_params=_params(("parallel",)),
        name="final_ln",
    )(yt, h2, ln_g, ln_b)


def _rot_half_cols(w):
    half = QK_ROPE // 2
    return jnp.concatenate([-w[..., half:], w[..., :half]], axis=-1)


def kernel(x, ln_in_g, ln_in_b, w_in, q_norm_g, w_uq, kv_norm_g, w_ukv, ssm_lam_re, ssm_lam_im, ssm_log_dt,
           ssm_b_re, ssm_b_im, ssm_c_re, ssm_c_im, ssm_d, w_glu, w_proj_attn, w_proj_ssm, w_out, ln_mix_g, ln_mix_b,
           w_peer_q, peer_sub_keys, peer_u, peer_v, ln_ffn_g, ln_ffn_b):
    batch, seq, _ = x.shape
    n = batch * seq
    x2 = x.reshape(n, D_MODEL)
    row = lambda v: v.reshape(1, -1).astype(F32)
    l = 0

    wi = w_in[l]
    w_main = jnp.concatenate([wi[:, OFF_SSM:OFF_KR], wi[:, OFF_GA:IN_WIDTH]], axis=1).astype(BF16)
    w_krc = wi[:, OFF_KR:OFF_GA]
    w_kr = jnp.concatenate([w_krc, _rot_half_cols(w_krc)], axis=1).astype(BF16)
    wq = w_uq[l].reshape(Q_LORA, N_HEADS, QK_HEAD)
    wq_ext = jnp.concatenate([wq, _rot_half_cols(wq[..., QK_NOPE:])], axis=-1)
    wq_ext = wq_ext.transpose(1, 0, 2).astype(BF16)
    wkv = w_ukv[l].reshape(KV_LORA, N_HEADS, QK_NOPE + V_HEAD).transpose(1, 0, 2).astype(BF16)

    cs = _rope_table(seq)
    u, z, kr = _inproj(x2, row(ln_in_g), row(ln_in_b), w_main, w_kr)

    q, k, v = _qkv(z, kr, cs, row(q_norm_g[l]), row(kv_norm_g[l]), wq_ext, wkv, batch, seq)
    attn = _attention(q, k, v).reshape(n, N_HEADS * V_HEAD)

    w, ws, wx, a16 = _ssm_wgen(ssm_lam_re[l], ssm_lam_im[l], ssm_log_dt[l], ssm_b_re[l], ssm_b_im[l],
                               ssm_c_re[l], ssm_c_im[l])
    g = _ssm_branch(u, batch, seq, w, ws, wx, a16, row(ssm_d[l]))

    merged = _merge(attn, g, z, w_glu[l].astype(BF16), w_proj_attn[l].astype(BF16), w_proj_ssm[l].astype(BF16))
    h2, h2t = _outproj(merged, x2, row(ln_in_g), row(ln_in_b), row(ln_mix_g[l]), row(ln_mix_b[l]),
                       w_out[l].astype(BF16))

    wcomb = _peer_wcomb(peer_sub_keys[l], w_peer_q[l].astype(BF16))
    s3 = _peer_scores(wcomb, h2t)
    c, e1, q, e2 = _peer_thresh(s3)
    yt = _peer_dense(h2t, q, e2, c, e1, peer_u[l].astype(BF16), peer_v[l].T.astype(BF16))
    out = _final_ln(yt, h2, row(ln_ffn_g[l]), row(ln_ffn_b[l]))
    return out.reshape(batch, seq, D_MODEL)
```
